```python
import math
import jax, jax.numpy as jnp
from jax import lax
import numpy as np

D_MODEL = 1024
BATCH = 4
SEQ = 8192
DEPTH = 2
DEC_BATCH = 32
DEC_SEQ = 64
PAST_LEN = 4096

CHUNK = 64
QBLOCK = 128
ROPE_THETA = 10000.0
EPS = 1e-6

H_A = 4
DH_A = 64
DV_A = 2 * DH_A

H_B = 8
Q_LORA = 384
KV_LORA = 256
QK_NOPE = 64
QK_ROPE = 32
V_B = 64

N_BRANCH = 2

COLS_QA = H_A * 2 * DH_A
COLS_KA = H_A * 2 * DH_A
COLS_VA = H_A * DV_A
COLS_CQ = Q_LORA
COLS_CKV = KV_LORA
COLS_KR = QK_ROPE
COLS_GATE = N_BRANCH * D_MODEL
IN_COLS = COLS_QA + COLS_KA + COLS_VA + COLS_CQ + COLS_CKV + COLS_KR + COLS_GATE
IN_SPLITS = [COLS_QA,
             COLS_QA + COLS_KA,
             COLS_QA + COLS_KA + COLS_VA,
             COLS_QA + COLS_KA + COLS_VA + COLS_CQ,
             COLS_QA + COLS_KA + COLS_VA + COLS_CQ + COLS_CKV,
             COLS_QA + COLS_KA + COLS_VA + COLS_CQ + COLS_CKV + COLS_KR]

N_GROUPS = 4
EXPERTS_PER_GROUP = 8
N_EXPERTS = N_GROUPS * EXPERTS_PER_GROUP
TOP_K = 2
D_EXPERT = 512
MOE_BLOCK = 128

kernel_name = "hybrid_diffattn_mla_hiermoe_stream_step"


def rmsnorm(x, g):
    xf = x.astype(jnp.float32)
    y = xf * lax.rsqrt(jnp.mean(xf * xf, axis=-1, keepdims=True) + EPS)
    return (y * g.astype(jnp.float32)).astype(x.dtype)


def rope(x, pos):
    d = x.shape[-1]
    half = d // 2
    inv = ROPE_THETA ** (-jnp.arange(half, dtype=jnp.float32) / half)
    ang = pos.astype(jnp.float32)[:, None] * inv[None, :]
    shape = (1, pos.shape[0]) + (1,) * (x.ndim - 3) + (half,)
    cos = jnp.cos(ang).reshape(shape)
    sin = jnp.sin(ang).reshape(shape)
    xf = x.astype(jnp.float32)
    x1, x2 = xf[..., :half], xf[..., half:]
    return jnp.concatenate([x1 * cos - x2 * sin, x2 * cos + x1 * sin], axis=-1).astype(x.dtype)


def diff_attention(q, k, v, lam, q_pos, k_pos):
    b, sq, h, _, d = q.shape
    qb = min(QBLOCK, sq)
    nb = sq // qb
    scale = d ** -0.5
    qs = jnp.moveaxis(q.reshape(b, nb, qb, h, 2, d), 1, 0)
    qp = q_pos.reshape(nb, qb)
    k_chunk = k_pos // CHUNK

    def block(args):
        qblk, qpos = args
        s = jnp.einsum('bqhmd,bkhmd->bhmqk', qblk, k,
                       preferred_element_type=jnp.float32) * scale
        mask = (qpos // CHUNK)[:, None] >= k_chunk[None, :]
        s = jnp.where(mask, s, -jnp.inf)
        p = jax.nn.softmax(s, axis=-1)
        a = p[:, :, 0] - lam * p[:, :, 1]
        return jnp.einsum('bhqk,bkhv->bqhv', a.astype(v.dtype), v)

    o = lax.map(block, (qs, qp))
    return jnp.moveaxis(o, 0, 1).reshape(b, sq, h, v.shape[-1])


def mla_attention(q_lat, q_rope, ckv, krope, q_pos, k_pos):
    b, sq, h, c = q_lat.shape
    qb = min(QBLOCK, sq)
    nb = sq // qb
    scale = (QK_NOPE + QK_ROPE) ** -0.5
    ql = jnp.moveaxis(q_lat.reshape(b, nb, qb, h, c), 1, 0)
    qr = jnp.moveaxis(q_rope.reshape(b, nb, qb, h, q_rope.shape[-1]), 1, 0)
    qp = q_pos.reshape(nb, qb)
    k_chunk = k_pos // CHUNK

    def block(args):
        qlb, qrb, qpos = args
        s = (jnp.einsum('bqhc,bkc->bhqk', qlb, ckv, preferred_element_type=jnp.float32)
             + jnp.einsum('bqhr,bkr->bhqk', qrb, krope, preferred_element_type=jnp.float32)) * scale
        mask = (qpos // CHUNK)[:, None] >= k_chunk[None, :]
        s = jnp.where(mask, s, -jnp.inf)
        p = jax.nn.softmax(s, axis=-1)
        return jnp.einsum('bhqk,bkc->bqhc', p.astype(ckv.dtype), ckv)

    o = lax.map(block, (ql, qr, qp))
    return jnp.moveaxis(o, 0, 1).reshape(b, sq, h, c)


def attention_block(x, pos, k_pos, past, P, l):
    b, s, _ = x.shape
    h = rmsnorm(x, P['g_attn'][l])
    proj = jnp.einsum('bsd,dc->bsc', h, P['w_in'][l])
    qa, ka, va, cq, ckv, kr, gates = jnp.split(proj, IN_SPLITS, axis=-1)

    qa = rope(qa.reshape(b, s, H_A, 2, DH_A), pos)
    ka = rope(ka.reshape(b, s, H_A, 2, DH_A), pos)
    va = va.reshape(b, s, H_A, DV_A)

    cq = rmsnorm(cq, P['g_q_lat'][l])
    qfull = jnp.einsum('bsc,ce->bse', cq, P['w_uq'][l]).reshape(b, s, H_B, QK_NOPE + QK_ROPE)
    q_rope = rope(qfull[..., QK_NOPE:], pos)
    q_lat = jnp.einsum('bshn,chn->bshc', qfull[..., :QK_NOPE], P['w_uk'][l])
    ckv = rmsnorm(ckv, P['g_kv_lat'][l])
    kr = rope(kr, pos)

    new = (ka, va, ckv, kr)
    if past is None:
        k_all, v_all, ckv_all, kr_all = ka, va, ckv, kr
    else:
        k_all = jnp.concatenate([past[0], ka], axis=1)
        v_all = jnp.concatenate([past[1], va], axis=1)
        ckv_all = jnp.concatenate([past[2], ckv], axis=1)
        kr_all = jnp.concatenate([past[3], kr], axis=1)

    lam_init = 0.8 - 0.6 * math.exp(-0.3 * l)
    lam = (jnp.exp(jnp.sum(P['lambda_q1'][l].astype(jnp.float32) * P['lambda_k1'][l].astype(jnp.float32)))
           - jnp.exp(jnp.sum(P['lambda_q2'][l].astype(jnp.float32) * P['lambda_k2'][l].astype(jnp.float32)))
           + lam_init)
    oa = diff_attention(qa, k_all, v_all, lam, pos, k_pos)
    oa = rmsnorm(oa, P['g_subln'][l]) * (1.0 - lam_init)
    ya = jnp.einsum('bsf,fd->bsd', oa.reshape(b, s, H_A * DV_A), P['w_a'][l])

    ob_lat = mla_attention(q_lat, q_rope, ckv_all, kr_all, pos, k_pos)
    ob = jnp.einsum('bshc,chv->bshv', ob_lat, P['w_uv'][l]).reshape(b, s, H_B * V_B)
    yb = jnp.einsum('bsf,fd->bsd', ob, P['w_b'][l])

    g = jax.nn.sigmoid(gates.astype(jnp.float32)).reshape(b, s, N_BRANCH, D_MODEL)
    merged = (g[..., 0, :] * ya + g[..., 1, :] * yb).astype(x.dtype)
    return x + jnp.einsum('bsd,de->bse', merged, P['w_o'][l]), new


def expert_mlp(xb, w_gu, w_dn):
    gu = xb @ w_gu
    gate, up = gu[:, :D_EXPERT], gu[:, D_EXPERT:]
    return (jax.nn.silu(gate) * up) @ w_dn


def moe_dispatch(h, expert, weight, w_gu, w_dn):
    n_tok, d = h.shape
    n_asg = n_tok * TOP_K
    flat_e = expert.reshape(n_asg)
    order = jnp.argsort(flat_e, stable=True)
    sorted_e = flat_e[order]
    counts = jnp.bincount(flat_e, length=N_EXPERTS)
    padded = (counts + MOE_BLOCK - 1) // MOE_BLOCK * MOE_BLOCK
    pad_end = jnp.cumsum(padded)
    pad_start = pad_end - padded
    start = jnp.cumsum(counts) - counts
    dest_sorted = (pad_start[sorted_e] + jnp.arange(n_asg, dtype=jnp.int32) - start[sorted_e]).astype(jnp.int32)
    n_slots = -(-n_asg // MOE_BLOCK) * MOE_BLOCK + N_EXPERTS * MOE_BLOCK
    n_blk = n_slots // MOE_BLOCK
    slot_tok = jnp.full((n_slots,), n_tok, jnp.int32).at[dest_sorted].set((order // TOP_K).astype(jnp.int32))
    h_pad = jnp.concatenate([h, jnp.zeros((1, d), h.dtype)], axis=0)
    xb = h_pad[slot_tok].reshape(n_blk, MOE_BLOCK, d)
    blk_e = jnp.minimum(jnp.searchsorted(pad_end, jnp.arange(n_blk) * MOE_BLOCK, side='right'),
                        N_EXPERTS - 1)
    yb = lax.map(lambda a: expert_mlp(a[0], w_gu[a[1]], w_dn[a[1]]), (xb, blk_e))
    y_slots = yb.reshape(n_slots, d)
    dest = jnp.zeros((n_asg,), jnp.int32).at[order].set(dest_sorted)
    y = y_slots[dest].reshape(n_tok, TOP_K, d)
    return jnp.einsum('tk,tkd->td', weight, y.astype(jnp.float32)).astype(h.dtype)


def moe_block(x, P, l):
    b, s, d = x.shape
    h = rmsnorm(x, P['g_ffn'][l]).reshape(b * s, d)
    g_logits = jnp.einsum('td,dg->tg', h, P['w_group'][l]).astype(jnp.float32) + P['b_group'][l].astype(jnp.float32)
    g_prob = jax.nn.softmax(g_logits, axis=-1)
    g_w, g_idx = lax.top_k(g_prob, 1)
    e_all = jnp.einsum('td,gde->tge', h, P['w_router'][l]).astype(jnp.float32) + P['b_router'][l].astype(jnp.float32)
    idx = jnp.broadcast_to(g_idx[:, :, None], (b * s, 1, EXPERTS_PER_GROUP))
    e_logits = jnp.take_along_axis(e_all, idx, axis=1)[:, 0]
    top_l, e_idx = lax.top_k(e_logits, TOP_K)
    weight = jax.nn.softmax(top_l, axis=-1) * g_w
    expert = g_idx * EXPERTS_PER_GROUP + e_idx
    y = moe_dispatch(h, expert, weight, P['w_gate_up'][l], P['w_down'][l])
    return x + y.reshape(b, s, d)


def trunk(x, pos, k_pos, caches, P):
    ks, vs, cs, rs = [], [], [], []
    for l in range(DEPTH):
        past = None if caches is None else (caches[0][l], caches[1][l], caches[2][l], caches[3][l])
        x, new = attention_block(x, pos, k_pos, past, P, l)
        x = moe_block(x, P, l)
        ks.append(new[0]); vs.append(new[1]); cs.append(new[2]); rs.append(new[3])
    y = rmsnorm(x, P['g_final'])
    return y, jnp.stack(ks), jnp.stack(vs), jnp.stack(cs), jnp.stack(rs)


def setup_inputs(seed: int = 0) -> dict:
    key = jax.random.key(seed)
    ks = jax.random.split(key, 32)
    f32 = jnp.float32

    def nrm(i, shape, scale):
        return scale * jax.random.normal(ks[i], shape, f32)

    def gain(i, shape):
        return 1.0 + 0.02 * jax.random.normal(ks[i], shape, f32)

    return {
        'x_prompt': nrm(0, (BATCH, SEQ, D_MODEL), 1.0),
        'x_sample': nrm(1, (DEC_BATCH, DEC_SEQ, D_MODEL), 1.0),
        'cache_diff_k': nrm(2, (DEPTH, DEC_BATCH, PAST_LEN, H_A, 2, DH_A), 1.0),
        'cache_diff_v': nrm(3, (DEPTH, DEC_BATCH, PAST_LEN, H_A, DV_A), 1.0),
        'cache_mla_ckv': nrm(4, (DEPTH, DEC_BATCH, PAST_LEN, KV_LORA), 1.0),
        'cache_mla_krope': nrm(5, (DEPTH, DEC_BATCH, PAST_LEN, QK_ROPE), 1.0),
        'g_attn': gain(6, (DEPTH, D_MODEL)),
        'w_in': nrm(7, (DEPTH, D_MODEL, IN_COLS), D_MODEL ** -0.5),
        'lambda_q1': nrm(8, (DEPTH, DH_A), 0.1),
        'lambda_k1': nrm(9, (DEPTH, DH_A), 0.1),
        'lambda_q2': nrm(10, (DEPTH, DH_A), 0.1),
        'lambda_k2': nrm(11, (DEPTH, DH_A), 0.1),
        'g_subln': gain(12, (DEPTH, DV_A)),
        'w_a': nrm(13, (DEPTH, H_A * DV_A, D_MODEL), (H_A * DV_A) ** -0.5),
        'g_q_lat': gain(14, (DEPTH, Q_LORA)),
        'w_uq': nrm(15, (DEPTH, Q_LORA, H_B * (QK_NOPE + QK_ROPE)), Q_LORA ** -0.5),
        'g_kv_lat': gain(16, (DEPTH, KV_LORA)),
        'w_uk': nrm(17, (DEPTH, KV_LORA, H_B, QK_NOPE), KV_LORA ** -0.5),
        'w_uv': nrm(18, (DEPTH, KV_LORA, H_B, V_B), KV_LORA ** -0.5),
        'w_b': nrm(19, (DEPTH, H_B * V_B, D_MODEL), (H_B * V_B) ** -0.5),
        'w_o': nrm(20, (DEPTH, D_MODEL, D_MODEL), D_MODEL ** -0.5),
        'g_ffn': gain(21, (DEPTH, D_MODEL)),
        'w_group': nrm(22, (DEPTH, D_MODEL, N_GROUPS), D_MODEL ** -0.5),
        'b_group': nrm(23, (DEPTH, N_GROUPS), 0.01),
        'w_router': nrm(24, (DEPTH, N_GROUPS, D_MODEL, EXPERTS_PER_GROUP), D_MODEL ** -0.5),
        'b_router': nrm(25, (DEPTH, N_GROUPS, EXPERTS_PER_GROUP), 0.01),
        'w_gate_up': nrm(26, (DEPTH, N_EXPERTS, D_MODEL, 2 * D_EXPERT), D_MODEL ** -0.5),
        'w_down': nrm(27, (DEPTH, N_EXPERTS, D_EXPERT, D_MODEL), D_EXPERT ** -0.5),
        'g_final': gain(28, (D_MODEL,)),
    }


def reference(x_prompt, x_sample, cache_diff_k, cache_diff_v, cache_mla_ckv, cache_mla_krope,
              g_attn, w_in, lambda_q1, lambda_k1, lambda_q2, lambda_k2, g_subln, w_a,
              g_q_lat, w_uq, g_kv_lat, w_uk, w_uv, w_b, w_o, g_ffn, w_group, b_group,
              w_router, b_router, w_gate_up, w_down, g_final):
    P = dict(g_attn=g_attn, w_in=w_in, lambda_q1=lambda_q1, lambda_k1=lambda_k1,
             lambda_q2=lambda_q2, lambda_k2=lambda_k2, g_subln=g_subln, w_a=w_a,
             g_q_lat=g_q_lat, w_uq=w_uq, g_kv_lat=g_kv_lat, w_uk=w_uk, w_uv=w_uv, w_b=w_b,
             w_o=w_o, g_ffn=g_ffn, w_group=w_group, b_group=b_group, w_router=w_router,
             b_router=b_router, w_gate_up=w_gate_up, w_down=w_down, g_final=g_final)

    pos_p = jnp.arange(x_prompt.shape[1], dtype=jnp.int32)
    y_prompt, p_k, p_v, p_c, p_r = trunk(x_prompt, pos_p, pos_p, None, P)

    past_len = cache_diff_k.shape[2]
    s_new = x_sample.shape[1]
    pos_s = past_len + jnp.arange(s_new, dtype=jnp.int32)
    k_pos_s = jnp.arange(past_len + s_new, dtype=jnp.int32)
    y_sample, s_k, s_v, s_c, s_r = trunk(
        x_sample, pos_s, k_pos_s, (cache_diff_k, cache_diff_v, cache_mla_ckv, cache_mla_krope), P)

    return (y_prompt, y_sample, p_k, p_v, p_c, p_r, s_k, s_v, s_c, s_r)
```

```python
import functools
import math

import jax
import jax.numpy as jnp
from jax import lax
from jax.experimental import pallas as pl
from jax.experimental.pallas import tpu as pltpu

D_MODEL = 1024
CHUNK = 64
ROPE_THETA = 10000.0
EPS = 1e-6
H_A = 4
DH_A = 64
DV_A = 2 * DH_A
H_B = 8
Q_LORA = 384
KV_LORA = 256
QK_NOPE = 64
QK_ROPE = 32
V_B = 64
N_GROUPS = 4
EXPERTS_PER_GROUP = 8
N_EXPERTS = N_GROUPS * EXPERTS_PER_GROUP
TOP_K = 2
D_EXPERT = 512

COLS_A = H_A * 2 * DH_A
COLS_QR = H_B * QK_ROPE
COLS_QN = H_B * QK_NOPE
LANES = 128
KR_PAD = LANES
PACK_COLS = 3 * COLS_A + Q_LORA + KV_LORA + KR_PAD + 2 * D_MODEL
ROUTE_COLS = LANES
VMEM_LIMIT = 56 * 1024 * 1024

F32 = jnp.float32
BF16 = jnp.bfloat16
NEG_INF = float("-inf")


def _cparams(sem):
    return pltpu.CompilerParams(dimension_semantics=sem, vmem_limit_bytes=VMEM_LIMIT)


def _rms(x, g):
    return x * lax.rsqrt(jnp.mean(x * x, axis=-1, keepdims=True) + EPS) * g


def _rope(x, cos, sin_signed, chunk):
    n = x.shape[-1]
    half = chunk // 2
    lane = lax.broadcasted_iota(jnp.int32, x.shape, 1)
    fwd = pltpu.roll(x, n - half, 1)
    bwd = pltpu.roll(x, half, 1)
    swapped = jnp.where((lane & (chunk - 1)) < half, fwd, bwd)
    return x * cos + swapped * sin_signed


def _dot_nt(a, b):
    return lax.dot_general(a, b, (((1,), (1,)), ((), ())), preferred_element_type=F32)


def _flash_step(s, v, m_sc, l_sc, acc_sc):
    m_prev = m_sc[...]
    m_cur = jnp.max(s, axis=-1, keepdims=True)
    m_new = jnp.maximum(m_prev, m_cur)
    alpha = jnp.exp(m_prev - m_new)
    p = jnp.exp(s - m_new[:, :1])
    l_sc[...] = alpha * l_sc[...] + jnp.sum(p, axis=-1, keepdims=True)
    pv = jnp.dot(p.astype(BF16), v, preferred_element_type=F32)
    reps = acc_sc.shape[-1] // LANES
    a = alpha if reps == 1 else jnp.concatenate([alpha] * reps, axis=-1)
    acc_sc[...] = a * acc_sc[...] + pv
    m_sc[...] = m_new


def _chunk_mask(rows, cols, tq):
    r = lax.broadcasted_iota(jnp.int32, (rows, cols), 0)
    c = lax.broadcasted_iota(jnp.int32, (rows, cols), 1)
    return ((r & (tq - 1)) // CHUNK) >= (c // CHUNK)


def _init_flash(m_sc, l_sc, acc_sc):
    m_sc[...] = jnp.full(m_sc.shape, NEG_INF, F32)
    l_sc[...] = jnp.zeros(l_sc.shape, F32)
    acc_sc[...] = jnp.zeros(acc_sc.shape, F32)


def _inproj_kernel(x_ref, g_ref, w_ref, gq_ref, gkv_ref, wuq_ref, c64_ref, s64_ref, c32_ref, s32_ref,
                   qa_ref, k_ref, kb_ref, v_ref, vb_ref, ckv_ref, ckvb_ref, kr_ref, krb_ref,
                   qn_ref, qr_ref, gate_ref):
    x = x_ref[...]
    hb = _rms(x, g_ref[...]).astype(BF16)

    def proj(lo, hi):
        return jnp.dot(hb, w_ref[:, lo:hi], preferred_element_type=F32)

    c64, s64 = c64_ref[...], s64_ref[...]
    c32, s32 = c32_ref[...], s32_ref[...]
    o = 0
    qa = _rope(proj(o, o + COLS_A), c64, s64, DH_A)
    qa_ref[...] = (qa * (DH_A ** -0.5)).astype(BF16)
    o += COLS_A
    ka = _rope(proj(o, o + COLS_A), c64, s64, DH_A)
    k_ref[...] = ka
    kb_ref[...] = ka.astype(BF16)
    o += COLS_A
    va = proj(o, o + COLS_A)
    v_ref[...] = va
    vb_ref[...] = va.astype(BF16)
    o += COLS_A
    cq = _rms(proj(o, o + Q_LORA), gq_ref[...]).astype(BF16)
    o += Q_LORA
    qfull = jnp.dot(cq, wuq_ref[...], preferred_element_type=F32) * ((QK_NOPE + QK_ROPE) ** -0.5)
    qn_ref[...] = qfull[:, :COLS_QN].astype(BF16)
    qr_ref[...] = _rope(qfull[:, COLS_QN:], c32, s32, QK_ROPE).astype(BF16)
    ckv = _rms(proj(o, o + KV_LORA), gkv_ref[...])
    ckv_ref[...] = ckv
    ckvb_ref[...] = ckv.astype(BF16)
    o += KV_LORA
    kr = _rope(proj(o, o + KR_PAD), c32[:, :KR_PAD], s32[:, :KR_PAD], QK_ROPE)[:, :QK_ROPE]
    kr_ref[...] = kr
    krb_ref[...] = kr.astype(BF16)
    o += KR_PAD
    gate_ref[...] = jax.nn.sigmoid(proj(o, o + 2 * D_MODEL)).astype(BF16)


def _inproj(x, g_attn, w_pack, g_q, g_kv, w_uq, tabs, *, tm, n_pos_tiles):
    t = x.shape[0]
    tok = lambda i: (i, 0)
    full = lambda i: (0, 0)
    pos = lambda i: (i % n_pos_tiles, 0)
    c64, s64, c32, s32 = tabs
    out_cols = [(COLS_A, BF16), (COLS_A, F32), (COLS_A, BF16), (COLS_A, F32), (COLS_A, BF16),
                (KV_LORA, F32), (KV_LORA, BF16), (QK_ROPE, F32), (QK_ROPE, BF16),
                (COLS_QN, BF16), (COLS_QR, BF16), (2 * D_MODEL, BF16)]
    return pl.pallas_call(
        _inproj_kernel,
        grid=(t // tm,),
        in_specs=[pl.BlockSpec((tm, D_MODEL), tok),
                  pl.BlockSpec((1, D_MODEL), full),
                  pl.BlockSpec((D_MODEL, PACK_COLS), full),
                  pl.BlockSpec((1, Q_LORA), full),
                  pl.BlockSpec((1, KV_LORA), full),
                  pl.BlockSpec((Q_LORA, COLS_QN + COLS_QR), full),
                  pl.BlockSpec((tm, COLS_A), pos), pl.BlockSpec((tm, COLS_A), pos),
                  pl.BlockSpec((tm, COLS_QR), pos), pl.BlockSpec((tm, COLS_QR), pos)],
        out_specs=[pl.BlockSpec((tm, c), tok) for c, _ in out_cols],
        out_shape=[jax.ShapeDtypeStruct((t, c), d) for c, d in out_cols],
        compiler_params=_cparams(("parallel",)),
        name="inproj",
    )(x, g_attn, w_pack, g_q, g_kv, w_uq, c64, s64, c32, s32)


def _stack_maps(q):
    lane = lax.broadcasted_iota(jnp.int32, q.shape, 1)
    zero = jnp.zeros_like(q)
    return jnp.concatenate([jnp.where(lane < DH_A, q, zero), jnp.where(lane >= DH_A, q, zero)], axis=0)


def _diff_finish(lam_ref, g_ref, o_ref, l_sc, acc_sc, tq, out_scale):
    inv = acc_sc[...] / l_sc[...]
    o = inv[:tq] - lam_ref[0] * inv[tq:]
    o_ref[...] = (_rms(o, g_ref[...]) * out_scale).astype(o_ref.dtype)


def _diff_prompt_kernel(lam_ref, q_ref, k_ref, v_ref, g_ref, o_ref, m_sc, l_sc, acc_sc, *, tq, out_scale):
    i = pl.program_id(2)
    qs = _stack_maps(q_ref[...])
    _init_flash(m_sc, l_sc, acc_sc)

    def tile(j):
        rows = pl.ds(pl.multiple_of(j * tq, tq), tq)
        return _dot_nt(qs, k_ref[rows, :]), v_ref[rows, :]

    def body(j, carry):
        s, v = tile(j)
        _flash_step(s, v, m_sc, l_sc, acc_sc)
        return carry

    lax.fori_loop(0, i, body, 0)
    s, v = tile(i)
    s = jnp.where(_chunk_mask(2 * tq, tq, tq), s, NEG_INF)
    _flash_step(s, v, m_sc, l_sc, acc_sc)
    _diff_finish(lam_ref, g_ref, o_ref, l_sc, acc_sc, tq, out_scale)


def _diff_prompt(lam, qa, kb, vb, g_subln, *, batch, seq, tq, out_scale):
    nq = seq // tq
    smem = pl.BlockSpec(memory_space=pltpu.SMEM)
    return pl.pallas_call(
        functools.partial(_diff_prompt_kernel, tq=tq, out_scale=out_scale),
        grid=(batch, H_A, nq),
        in_specs=[smem,
                  pl.BlockSpec((tq, DV_A), lambda b, h, i: (b * nq + i, h)),
                  pl.BlockSpec((seq, DV_A), lambda b, h, i: (b, h)),
                  pl.BlockSpec((seq, DV_A), lambda b, h, i: (b, h)),
                  pl.BlockSpec((1, DV_A), lambda b, h, i: (0, 0))],
        out_specs=pl.BlockSpec((tq, DV_A), lambda b, h, i: (b * nq + i, h)),
        out_shape=jax.ShapeDtypeStruct((batch * seq, COLS_A), BF16),
        scratch_shapes=[pltpu.VMEM((2 * tq, LANES), F32), pltpu.VMEM((2 * tq, LANES), F32),
                        pltpu.VMEM((2 * tq, DV_A), F32)],
        compiler_params=_cparams(("parallel", "parallel", "arbitrary")),
        name="diff_attn_prompt",
    )(lam, qa, kb, vb, g_subln)


def _diff_sample_kernel(lam_ref, q_ref, kc_ref, vc_ref, kn_ref, vn_ref, g_ref, o_ref, m_sc, l_sc, acc_sc,
                        *, tq, tk, n_cache_tiles, out_scale):
    qs = _stack_maps(q_ref[...])
    _init_flash(m_sc, l_sc, acc_sc)

    def body(j, carry):
        rows = pl.ds(pl.multiple_of(j * tk, tk), tk)
        k = kc_ref[0, rows, :].astype(BF16)
        v = vc_ref[0, rows, :].astype(BF16)
        _flash_step(_dot_nt(qs, k), v, m_sc, l_sc, acc_sc)
        return carry

    lax.fori_loop(0, n_cache_tiles, body, 0)
    _flash_step(_dot_nt(qs, kn_ref[...]), vn_ref[...], m_sc, l_sc, acc_sc)
    _diff_finish(lam_ref, g_ref, o_ref, l_sc, acc_sc, tq, out_scale)


def _diff_sample(lam, qa, kcache, vcache, kb, vb, g_subln, *, layer, batch, seq, past, out_scale):
    tk = min(512, past)
    smem = pl.BlockSpec(memory_space=pltpu.SMEM)
    return pl.pallas_call(
        functools.partial(_diff_sample_kernel, tq=seq, tk=tk, n_cache_tiles=past // tk, out_scale=out_scale),
        grid=(batch, H_A),
        in_specs=[smem,
                  pl.BlockSpec((seq, DV_A), lambda b, h: (b, h)),
                  pl.BlockSpec((1, past, DV_A), lambda b, h: (layer * batch + b, 0, h)),
                  pl.BlockSpec((1, past, DV_A), lambda b, h: (layer * batch + b, 0, h)),
                  pl.BlockSpec((seq, DV_A), lambda b, h: (b, h)),
                  pl.BlockSpec((seq, DV_A), lambda b, h: (b, h)),
                  pl.BlockSpec((1, DV_A), lambda b, h: (0, 0))],
        out_specs=pl.BlockSpec((seq, DV_A), lambda b, h: (b, h)),
        out_shape=jax.ShapeDtypeStruct((batch * seq, COLS_A), BF16),
        scratch_shapes=[pltpu.VMEM((2 * seq, LANES), F32), pltpu.VMEM((2 * seq, LANES), F32),
                        pltpu.VMEM((2 * seq, DV_A), F32)],
        compiler_params=_cparams(("parallel", "parallel")),
        name="diff_attn_sample",
    )(lam, qa, kcache, vcache, kb, vb, g_subln)


def _mla_prologue(qn_ref, qr_ref, wuk_ref, ql_sc, qr_sc, tq):
    qn = qn_ref[...]
    qr = qr_ref[...].astype(F32)
    for h in range(H_B):
        ql = jnp.dot(qn[:, h * QK_NOPE:(h + 1) * QK_NOPE], wuk_ref[h], preferred_element_type=F32)
        ql_sc[h * tq:(h + 1) * tq, :] = ql.astype(BF16)
        qr_sc[h * tq:(h + 1) * tq, :] = qr[:, h * QK_ROPE:(h + 1) * QK_ROPE].astype(BF16)


def _mla_finish(wz_ref, o_ref, l_sc, acc_sc, tq):
    l = l_sc[...]
    o_lat = (acc_sc[...] / jnp.concatenate([l, l], axis=-1)).astype(BF16)
    ob = jnp.zeros(o_ref.shape, F32)
    for h in range(H_B):
        ob += jnp.dot(o_lat[h * tq:(h + 1) * tq, :], wz_ref[h], preferred_element_type=F32)
    o_ref[...] = ob.astype(o_ref.dtype)


def _mla_prompt_kernel(qn_ref, qr_ref, ckv_ref, kr_ref, wuk_ref, wz_ref, o_ref,
                       ql_sc, qr_sc, m_sc, l_sc, acc_sc, *, tq):
    i = pl.program_id(1)
    _mla_prologue(qn_ref, qr_ref, wuk_ref, ql_sc, qr_sc, tq)
    _init_flash(m_sc, l_sc, acc_sc)

    def tile(j):
        rows = pl.ds(pl.multiple_of(j * tq, tq), tq)
        c = ckv_ref[rows, :]
        return _dot_nt(ql_sc[...], c) + _dot_nt(qr_sc[...], kr_ref[rows, :]), c

    def body(j, carry):
        s, c = tile(j)
        _flash_step(s, c, m_sc, l_sc, acc_sc)
        return carry

    lax.fori_loop(0, i, body, 0)
    s, c = tile(i)
    s = jnp.where(_chunk_mask(H_B * tq, tq, tq), s, NEG_INF)
    _flash_step(s, c, m_sc, l_sc, acc_sc)
    _mla_finish(wz_ref, o_ref, l_sc, acc_sc, tq)


def _mla_prompt(qn, qr, ckvb, krb, wuk_t, wz, *, batch, seq, tq):
    nq = seq // tq
    rows = H_B * tq
    return pl.pallas_call(
        functools.partial(_mla_prompt_kernel, tq=tq),
        grid=(batch, nq),
        in_specs=[pl.BlockSpec((tq, COLS_QN), lambda b, i: (b * nq + i, 0)),
                  pl.BlockSpec((tq, COLS_QR), lambda b, i: (b * nq + i, 0)),
                  pl.BlockSpec((seq, KV_LORA), lambda b, i: (b, 0)),
                  pl.BlockSpec((seq, QK_ROPE), lambda b, i: (b, 0)),
                  pl.BlockSpec((H_B, QK_NOPE, KV_LORA), lambda b, i: (0, 0, 0)),
                  pl.BlockSpec((H_B, KV_LORA, H_B * V_B), lambda b, i: (0, 0, 0))],
        out_specs=pl.BlockSpec((tq, H_B * V_B), lambda b, i: (b * nq + i, 0)),
        out_shape=jax.ShapeDtypeStruct((batch * seq, H_B * V_B), BF16),
        scratch_shapes=[pltpu.VMEM((rows, KV_LORA), BF16), pltpu.VMEM((rows, QK_ROPE), BF16),
                        pltpu.VMEM((rows, LANES), F32), pltpu.VMEM((rows, LANES), F32),
                        pltpu.VMEM((rows, KV_LORA), F32)],
        compiler_params=_cparams(("parallel", "arbitrary")),
        name="mla_attn_prompt",
    )(qn, qr, ckvb, krb, wuk_t, wz)


def _mla_sample_kernel(qn_ref, qr_ref, cc_ref, rc_ref, cn_ref, rn_ref, wuk_ref, wz_ref, o_ref,
                       ql_sc, qr_sc, m_sc, l_sc, acc_sc, *, tq, tk, n_cache_tiles):
    _mla_prologue(qn_ref, qr_ref, wuk_ref, ql_sc, qr_sc, tq)
    _init_flash(m_sc, l_sc, acc_sc)

    def body(j, carry):
        rows = pl.ds(pl.multiple_of(j * tk, tk), tk)
        c = cc_ref[0, rows, :].astype(BF16)
        r = rc_ref[0, rows, :].astype(BF16)
        _flash_step(_dot_nt(ql_sc[...], c) + _dot_nt(qr_sc[...], r), c, m_sc, l_sc, acc_sc)
        return carry

    lax.fori_loop(0, n_cache_tiles, body, 0)
    c = cn_ref[...]
    _flash_step(_dot_nt(ql_sc[...], c) + _dot_nt(qr_sc[...], rn_ref[...]), c, m_sc, l_sc, acc_sc)
    _mla_finish(wz_ref, o_ref, l_sc, acc_sc, tq)


def _mla_sample(qn, qr, ccache, rcache, ckvb, krb, wuk_t, wz, *, layer, batch, seq, past):
    tk = min(512, past)
    rows = H_B * seq
    return pl.pallas_call(
        functools.partial(_mla_sample_kernel, tq=seq, tk=tk, n_cache_tiles=past // tk),
        grid=(batch,),
        in_specs=[pl.BlockSpec((seq, COLS_QN), lambda b: (b, 0)),
                  pl.BlockSpec((seq, COLS_QR), lambda b: (b, 0)),
                  pl.BlockSpec((1, past, KV_LORA), lambda b: (layer * batch + b, 0, 0)),
                  pl.BlockSpec((1, past, QK_ROPE), lambda b: (layer * batch + b, 0, 0)),
                  pl.BlockSpec((seq, KV_LORA), lambda b: (b, 0)),
                  pl.BlockSpec((seq, QK_ROPE), lambda b: (b, 0)),
                  pl.BlockSpec((H_B, QK_NOPE, KV_LORA), lambda b: (0, 0, 0)),
                  pl.BlockSpec((H_B, KV_LORA, H_B * V_B), lambda b: (0, 0, 0))],
        out_specs=pl.BlockSpec((seq, H_B * V_B), lambda b: (b, 0)),
        out_shape=jax.ShapeDtypeStruct((batch * seq, H_B * V_B), BF16),
        scratch_shapes=[pltpu.VMEM((rows, KV_LORA), BF16), pltpu.VMEM((rows, QK_ROPE), BF16),
                        pltpu.VMEM((rows, LANES), F32), pltpu.VMEM((rows, LANES), F32),
                        pltpu.VMEM((rows, KV_LORA), F32)],
        compiler_params=_cparams(("parallel",)),
        name="mla_attn_sample",
    )(qn, qr, ccache, rcache, ckvb, krb, wuk_t, wz)


def _merge_kernel(oa_ref, ob_ref, gate_ref, x_ref, wa_ref, wb_ref, wo_ref, gf_ref, wr_ref, br_ref,
                  x1_ref, h_ref, route_ref):
    ya = jnp.dot(oa_ref[...], wa_ref[...], preferred_element_type=F32)
    yb = jnp.dot(ob_ref[...], wb_ref[...], preferred_element_type=F32)
    gates = gate_ref[...].astype(F32)
    merged = (gates[:, :D_MODEL] * ya + gates[:, D_MODEL:] * yb).astype(BF16)
    x1 = x_ref[...] + jnp.dot(merged, wo_ref[...], preferred_element_type=F32)
    x1_ref[...] = x1
    h = _rms(x1, gf_ref[...])
    h_ref[...] = h.astype(BF16)

    logits = jnp.dot(h, wr_ref[...], preferred_element_type=F32, precision=lax.Precision.HIGHEST) + br_ref[...]
    lane = lax.broadcasted_iota(jnp.int32, logits.shape, 1)
    big = jnp.int32(ROUTE_COLS)

    def top1(mask):
        v = jnp.max(jnp.where(mask, logits, NEG_INF), axis=-1, keepdims=True)
        idx = jnp.min(jnp.where(mask & (logits == v), lane, big), axis=-1, keepdims=True)
        return v, idx

    gmask = lane < N_GROUPS
    gmax, gidx = top1(gmask)
    g_w = 1.0 / jnp.sum(jnp.where(gmask, jnp.exp(logits - gmax), 0.0), axis=-1, keepdims=True)
    lo = N_GROUPS + gidx * EXPERTS_PER_GROUP
    emask = (lane >= lo) & (lane < lo + EXPERTS_PER_GROUP)
    v1, i1 = top1(emask)
    v2, i2 = top1(emask & (lane != i1))
    e2 = jnp.exp(v2 - v1)
    w1 = g_w / (1.0 + e2)
    w2 = g_w * e2 / (1.0 + e2)
    route = jnp.where(lane == 0, (i1 - N_GROUPS).astype(F32),
                      jnp.where(lane == 1, (i2 - N_GROUPS).astype(F32),
                                jnp.where(lane == 2, w1, jnp.where(lane == 3, w2, 0.0))))
    route_ref[...] = route


def _merge(oa, ob, gates, x, wa, wb, wo, g_ffn, w_route, b_route, *, tm):
    t = x.shape[0]
    tok = lambda i: (i, 0)
    full = lambda i: (0, 0)
    return pl.pallas_call(
        _merge_kernel,
        grid=(t // tm,),
        in_specs=[pl.BlockSpec((tm, COLS_A), tok), pl.BlockSpec((tm, H_B * V_B), tok),
                  pl.BlockSpec((tm, 2 * D_MODEL), tok), pl.BlockSpec((tm, D_MODEL), tok),
                  pl.BlockSpec((COLS_A, D_MODEL), full), pl.BlockSpec((H_B * V_B, D_MODEL), full),
                  pl.BlockSpec((D_MODEL, D_MODEL), full), pl.BlockSpec((1, D_MODEL), full),
                  pl.BlockSpec((D_MODEL, ROUTE_COLS), full), pl.BlockSpec((1, ROUTE_COLS), full)],
        out_specs=[pl.BlockSpec((tm, D_MODEL), tok), pl.BlockSpec((tm, D_MODEL), tok),
                   pl.BlockSpec((tm, ROUTE_COLS), tok)],
        out_shape=[jax.ShapeDtypeStruct((t, D_MODEL), F32), jax.ShapeDtypeStruct((t, D_MODEL), BF16),
                   jax.ShapeDtypeStruct((t, ROUTE_COLS), F32)],
        compiler_params=_cparams(("parallel",)),
        name="merge_router",
    )(oa, ob, gates, x, wa, wb, wo, g_ffn, w_route, b_route)


def _expert_kernel(blk_e_ref, n_used_ref, x_ref, wgu_ref, wdn_ref, y_ref):
    i = pl.program_id(0)

    @pl.when(i < n_used_ref[0])
    def _():
        gu = jnp.dot(x_ref[...], wgu_ref[0], preferred_element_type=F32)
        gate, up = gu[:, :D_EXPERT], gu[:, D_EXPERT:]
        a = (gate * jax.nn.sigmoid(gate) * up).astype(BF16)
        y_ref[...] = jnp.dot(a, wdn_ref[0], preferred_element_type=F32)

    @pl.when(i >= n_used_ref[0])
    def _():
        y_ref[...] = jnp.zeros(y_ref.shape, y_ref.dtype)


def _experts(blk_e, n_used, xs, wgu, wdn, *, layer, bm):
    n_slots = xs.shape[0]
    grid_spec = pltpu.PrefetchScalarGridSpec(
        num_scalar_prefetch=2,
        grid=(n_slots // bm,),
        in_specs=[pl.BlockSpec((bm, D_MODEL), lambda i, be, nu: (i, 0)),
                  pl.BlockSpec((1, D_MODEL, 2 * D_EXPERT), lambda i, be, nu: (layer * N_EXPERTS + be[i], 0, 0)),
                  pl.BlockSpec((1, D_EXPERT, D_MODEL), lambda i, be, nu: (layer * N_EXPERTS + be[i], 0, 0))],
        out_specs=pl.BlockSpec((bm, D_MODEL), lambda i, be, nu: (i, 0)),
    )
    return pl.pallas_call(
        _expert_kernel,
        grid_spec=grid_spec,
        out_shape=jax.ShapeDtypeStruct((n_slots, D_MODEL), F32),
        compiler_params=_cparams(("arbitrary",)),
        name="experts",
    )(blk_e, n_used, xs, wgu, wdn)


def _combine_kernel(x_ref, y0_ref, y1_ref, route_ref, g_ref, o_ref, *, final_norm):
    r = route_ref[...]
    y = x_ref[...] + r[:, 2:3] * y0_ref[...] + r[:, 3:4] * y1_ref[...]
    if final_norm:
        y = _rms(y, g_ref[...])
    o_ref[...] = y


def _combine(x1, y0, y1, route, g_final, *, tm, final_norm):
    t = x1.shape[0]
    tok = lambda i: (i, 0)
    return pl.pallas_call(
        functools.partial(_combine_kernel, final_norm=final_norm),
        grid=(t // tm,),
        in_specs=[pl.BlockSpec((tm, D_MODEL), tok), pl.BlockSpec((tm, D_MODEL), tok),
                  pl.BlockSpec((tm, D_MODEL), tok), pl.BlockSpec((tm, ROUTE_COLS), tok),
                  pl.BlockSpec((1, D_MODEL), lambda i: (0, 0))],
        out_specs=pl.BlockSpec((tm, D_MODEL), tok),
        out_shape=jax.ShapeDtypeStruct((t, D_MODEL), F32),
        compiler_params=_cparams(("parallel",)),
        name="combine",
    )(x1, y0, y1, route, g_final)


def _rope_tables(pos, chunk, cols):
    half = chunk // 2
    inv = ROPE_THETA ** (-jnp.arange(half, dtype=F32) / half)
    ang = pos.astype(F32)[:, None] * inv[None, :]
    cos = jnp.concatenate([jnp.cos(ang), jnp.cos(ang)], axis=-1)
    sin = jnp.concatenate([-jnp.sin(ang), jnp.sin(ang)], axis=-1)
    reps = cols // chunk
    return jnp.tile(cos, (1, reps)), jnp.tile(sin, (1, reps))


def _moe_plan(route, bm):
    n_tok = route.shape[0]
    n_asg = n_tok * TOP_K
    flat_e = route[:, :TOP_K].astype(jnp.int32).reshape(n_asg)
    onehot = (flat_e[:, None] == jnp.arange(N_EXPERTS, dtype=jnp.int32)[None, :]).astype(jnp.int32)
    csum = jnp.cumsum(onehot, axis=0)
    rank = jnp.take_along_axis(csum, flat_e[:, None], axis=1)[:, 0] - 1
    counts = csum[-1]
    padded = (counts + bm - 1) // bm * bm
    pad_end = jnp.cumsum(padded)
    pad_start = pad_end - padded
    dest = (pad_start[flat_e] + rank).astype(jnp.int32)
    n_slots = n_asg + N_EXPERTS * bm
    n_blk = n_slots // bm
    slot_tok = jnp.zeros((n_slots,), jnp.int32).at[dest].set(jnp.arange(n_asg, dtype=jnp.int32) // TOP_K)
    blk_e = jnp.minimum(jnp.searchsorted(pad_end, jnp.arange(n_blk, dtype=jnp.int32) * bm, side='right'),
                        N_EXPERTS - 1).astype(jnp.int32)
    n_used = (pad_end[-1:] // bm).astype(jnp.int32)
    return slot_tok, blk_e, n_used, dest.reshape(n_tok, TOP_K)


def _layer_weights(l, g_attn, w_in, g_subln, w_a, g_q_lat, w_uq, g_kv_lat, w_uk, w_uv, w_b, w_o, g_ffn,
                   w_group, b_group, w_router, b_router):
    w = w_in[l]
    o_kr = 3 * COLS_A + Q_LORA + KV_LORA
    w_pack = jnp.concatenate(
        [w[:, :o_kr], jnp.pad(w[:, o_kr:o_kr + QK_ROPE], ((0, 0), (0, KR_PAD - QK_ROPE))), w[:, o_kr + QK_ROPE:]],
        axis=1).astype(BF16)
    wq = w_uq[l].reshape(Q_LORA, H_B, QK_NOPE + QK_ROPE)
    w_uq_p = jnp.concatenate([wq[:, :, :QK_NOPE].reshape(Q_LORA, COLS_QN),
                              wq[:, :, QK_NOPE:].reshape(Q_LORA, COLS_QR)], axis=1).astype(BF16)
    wuk_t = jnp.transpose(w_uk[l], (1, 2, 0)).astype(BF16)
    wuv = jnp.transpose(w_uv[l], (1, 0, 2))
    eye = jnp.eye(H_B, dtype=F32)
    wz = (wuv[:, :, None, :] * eye[:, None, :, None]).reshape(H_B, KV_LORA, H_B * V_B).astype(BF16)
    w_route = jnp.concatenate(
        [w_group[l], jnp.transpose(w_router[l], (1, 0, 2)).reshape(D_MODEL, N_EXPERTS),
         jnp.zeros((D_MODEL, ROUTE_COLS - N_GROUPS - N_EXPERTS), F32)], axis=1)
    b_route = jnp.concatenate([b_group[l], b_router[l].reshape(N_EXPERTS),
                               jnp.zeros((ROUTE_COLS - N_GROUPS - N_EXPERTS,), F32)])[None, :]
    return dict(g_attn=g_attn[l][None, :], w_pack=w_pack, g_q=g_q_lat[l][None, :], g_kv=g_kv_lat[l][None, :],
                w_uq=w_uq_p, wuk_t=wuk_t, wz=wz, g_subln=g_subln[l][None, :],
                wa=w_a[l].astype(BF16), wb=w_b[l].astype(BF16), wo=w_o[l].astype(BF16),
                g_ffn=g_ffn[l][None, :], w_route=w_route, b_route=b_route)


def _trunk(x, pos_tile, n_pos_tiles, tm, caches, lws, lams, wgu, wdn, g_final, *, batch, seq, bm, tq):
    depth = len(lws)
    t = batch * seq
    x = x.reshape(t, D_MODEL)
    tabs = _rope_tables(pos_tile, DH_A, COLS_A) + _rope_tables(pos_tile, QK_ROPE, COLS_QR)
    ks, vs, cs, rs = [], [], [], []
    for l in range(depth):
        w = lws[l]
        lam_init = 0.8 - 0.6 * math.exp(-0.3 * l)
        qa, k, kb, v, vb, ckv, ckvb, kr, krb, qn, qr, gates = _inproj(
            x, w['g_attn'], w['w_pack'], w['g_q'], w['g_kv'], w['w_uq'], tabs, tm=tm, n_pos_tiles=n_pos_tiles)
        if caches is None:
            oa = _diff_prompt(lams[l], qa, kb, vb, w['g_subln'], batch=batch, seq=seq, tq=tq,
                              out_scale=1.0 - lam_init)
            ob = _mla_prompt(qn, qr, ckvb, krb, w['wuk_t'], w['wz'], batch=batch, seq=seq, tq=tq)
        else:
            kc, vc, cc, rc, past = caches
            oa = _diff_sample(lams[l], qa, kc, vc, kb, vb, w['g_subln'], layer=l, batch=batch, seq=seq,
                              past=past, out_scale=1.0 - lam_init)
            ob = _mla_sample(qn, qr, cc, rc, ckvb, krb, w['wuk_t'], w['wz'], layer=l, batch=batch, seq=seq,
                             past=past)
        x1, h, route = _merge(oa, ob, gates, x, w['wa'], w['wb'], w['wo'], w['g_ffn'], w['w_route'],
                              w['b_route'], tm=tm)
        slot_tok, blk_e, n_used, dest = _moe_plan(route, bm)
        xs = jnp.take(h, slot_tok, axis=0)
        y_slots = _experts(blk_e, n_used, xs, wgu, wdn, layer=l, bm=bm)
        y0 = jnp.take(y_slots, dest[:, 0], axis=0)
        y1 = jnp.take(y_slots, dest[:, 1], axis=0)
        x = _combine(x1, y0, y1, route, g_final, tm=tm, final_norm=(l == depth - 1))
        ks.append(k)
        vs.append(v)
        cs.append(ckv)
        rs.append(kr)
    y = x.reshape(batch, seq, D_MODEL)
    return (y,
            jnp.stack(ks).reshape(depth, batch, seq, H_A, 2, DH_A),
            jnp.stack(vs).reshape(depth, batch, seq, H_A, DV_A),
            jnp.stack(cs).reshape(depth, batch, seq, KV_LORA),
            jnp.stack(rs).reshape(depth, batch, seq, QK_ROPE))


def kernel(x_prompt, x_sample, cache_diff_k, cache_diff_v, cache_mla_ckv, cache_mla_krope, g_attn, w_in,
           lambda_q1, lambda_k1, lambda_q2, lambda_k2, g_subln, w_a, g_q_lat, w_uq, g_kv_lat, w_uk, w_uv, w_b,
           w_o, g_ffn, w_group, b_group, w_router, b_router, w_gate_up, w_down, g_final):
    depth = w_in.shape[0]
    batch, seq, _ = x_prompt.shape
    dec_batch, dec_seq, _ = x_sample.shape
    past = cache_diff_k.shape[2]
    assert dec_seq == CHUNK and past % CHUNK == 0, "sample frames must form exactly one new chunk"

    lws = [_layer_weights(l, g_attn, w_in, g_subln, w_a, g_q_lat, w_uq, g_kv_lat, w_uk, w_uv, w_b, w_o, g_ffn,
                          w_group, b_group, w_router, b_router) for l in range(depth)]
    lams = []
    for l in range(depth):
        lam_init = 0.8 - 0.6 * math.exp(-0.3 * l)
        lam = (jnp.exp(jnp.sum(lambda_q1[l] * lambda_k1[l])) - jnp.exp(jnp.sum(lambda_q2[l] * lambda_k2[l]))
               + lam_init)
        lams.append(lam.reshape(1).astype(F32))
    wgu = w_gate_up.reshape(depth * N_EXPERTS, D_MODEL, 2 * D_EXPERT).astype(BF16)
    wdn = w_down.reshape(depth * N_EXPERTS, D_EXPERT, D_MODEL).astype(BF16)
    g_fin = g_final[None, :]

    tm_p = min(512, seq)
    tq = min(256, seq)
    outs_p = _trunk(x_prompt, jnp.arange(seq, dtype=jnp.int32), seq // tm_p, tm_p, None, lws, lams, wgu, wdn,
                    g_fin, batch=batch, seq=seq, bm=min(512, batch * seq), tq=tq)

    tm_s = min(512, dec_batch * dec_seq)
    pos_tile = jnp.tile(past + jnp.arange(dec_seq, dtype=jnp.int32), tm_s // dec_seq)
    caches = (cache_diff_k.reshape(depth * dec_batch, past, COLS_A),
              cache_diff_v.reshape(depth * dec_batch, past, COLS_A),
              cache_mla_ckv.reshape(depth * dec_batch, past, KV_LORA),
              cache_mla_krope.reshape(depth * dec_batch, past, QK_ROPE), past)
    outs_s = _trunk(x_sample, pos_tile, 1, tm_s, caches, lws, lams, wgu, wdn, g_fin,
                    batch=dec_batch, seq=dec_seq, bm=min(128, dec_batch * dec_seq), tq=dec_seq)

    return (outs_p[0], outs_s[0]) + outs_p[1:] + outs_s[1:]
```

```python
import functools
import math

import jax
import jax.numpy as jnp
from jax import lax
from jax.experimental import pallas as pl
from jax.experimental.pallas import tpu as pltpu

D_MODEL = 1024
CHUNK = 64
ROPE_THETA = 10000.0
EPS = 1e-6
H_A = 4
DH_A = 64
DV_A = 2 * DH_A
H_B = 8
Q_LORA = 384
KV_LORA = 256
QK_NOPE = 64
QK_ROPE = 32
V_B = 64
N_GROUPS = 4
EXPERTS_PER_GROUP = 8
N_EXPERTS = N_GROUPS * EXPERTS_PER_GROUP
TOP_K = 2
D_EXPERT = 512

COLS_A = H_A * 2 * DH_A
COLS_QR = H_B * QK_ROPE
COLS_QN = H_B * QK_NOPE
COLS_VB = H_B * V_B
LANES = 128
COLS_QB = H_B * LANES
KR_PAD = LANES
PACK_COLS = 3 * COLS_A + Q_LORA + KV_LORA + KR_PAD + 2 * D_MODEL
ROUTE_COLS = LANES
TT = 256
VMEM_LIMIT = 56 * 1024 * 1024
LOG2E = 1.4426950408889634
SCALE_A = DH_A ** -0.5 * LOG2E
SCALE_B = (QK_NOPE + QK_ROPE) ** -0.5 * LOG2E

F32 = jnp.float32
BF16 = jnp.bfloat16
NEG_INF = float("-inf")


def _cparams(sem):
    return pltpu.CompilerParams(dimension_semantics=sem, vmem_limit_bytes=VMEM_LIMIT)


def _rms(x, g):
    return x * lax.rsqrt(jnp.mean(x * x, axis=-1, keepdims=True) + EPS) * g


def _widen(tab, cols):
    reps = cols // LANES
    return tab if reps == 1 else jnp.concatenate([tab] * reps, axis=-1)


def _rope(x, cos, sin_signed, chunk):
    n = x.shape[-1]
    half = chunk // 2
    lane = lax.broadcasted_iota(jnp.int32, x.shape, 1)
    fwd = pltpu.roll(x, n - half, 1)
    bwd = pltpu.roll(x, half, 1)
    swapped = jnp.where((lane & (chunk - 1)) < half, fwd, bwd)
    return x * _widen(cos, n) + swapped * _widen(sin_signed, n)


def _dot_nt(a, b):
    return lax.dot_general(a, b, (((1,), (1,)), ((), ())), preferred_element_type=F32)


def _store_tiles_t(ref, val):
    for r in range(ref.shape[0]):
        ref[r] = val[r * TT:(r + 1) * TT, :].T.astype(ref.dtype)


def _load_tiles_t(ref):
    return jnp.concatenate([ref[r].astype(F32).T for r in range(ref.shape[0])], axis=0).astype(BF16)


def _inproj_kernel(*refs, prompt):
    (x_ref, g_ref, w_ref, gq_ref, gkv_ref, wuq_ref, c64_ref, s64_ref, c32_ref, s32_ref) = refs[:10]
    x = x_ref[...]
    hb = _rms(x, g_ref[...]).astype(BF16)

    def proj(lo, hi):
        return jnp.dot(hb, w_ref[:, lo:hi], preferred_element_type=F32)

    c64, s64 = c64_ref[...], s64_ref[...]
    c32, s32 = c32_ref[...], s32_ref[...]
    o = 0
    qa = _rope(proj(o, o + COLS_A), c64, s64, DH_A) * SCALE_A
    o += COLS_A
    ka = _rope(proj(o, o + COLS_A), c64, s64, DH_A)
    o += COLS_A
    va = proj(o, o + COLS_A)
    o += COLS_A
    cq = _rms(proj(o, o + Q_LORA), gq_ref[...]).astype(BF16)
    o += Q_LORA
    qfull = jnp.dot(cq, wuq_ref[...], preferred_element_type=F32) * SCALE_B
    ckv = _rms(proj(o, o + KV_LORA), gkv_ref[...])
    ckvb = ckv.astype(BF16)
    o += KV_LORA
    kr_pad = _rope(proj(o, o + KR_PAD), c32, s32, QK_ROPE)
    o += KR_PAD
    gates = jax.nn.sigmoid(proj(o, o + 2 * D_MODEL)).astype(BF16)

    if prompt:
        (cq_ref, sq_ref, wkn_ref, wkr_ref, wuv_ref,
         k_ref, v_ref, ckv_ref, kr_ref, qat_ref, kb_ref, vat_ref, qbt_ref, kq_ref, vbt_ref, gate_ref) = refs[10:]
        _store_tiles_t(qat_ref, qa)
        kb_ref[...] = ka.astype(BF16)
        _store_tiles_t(vat_ref, va)
        _store_tiles_t(qbt_ref, _rope(qfull, cq_ref[...], sq_ref[...], QK_ROPE))
        kq = (jnp.dot(ckvb, wkn_ref[...], preferred_element_type=F32)
              + jnp.dot(kr_pad.astype(BF16), wkr_ref[...], preferred_element_type=F32))
        kq_ref[...] = kq.astype(BF16)
        _store_tiles_t(vbt_ref, jnp.dot(ckvb, wuv_ref[...], preferred_element_type=F32))
    else:
        (k_ref, v_ref, ckv_ref, kr_ref, qa_ref, kb_ref, vb_ref, ckvb_ref, krb_ref, qn_ref, qr_ref,
         gate_ref) = refs[10:]
        qa_ref[...] = qa.astype(BF16)
        kb_ref[...] = ka.astype(BF16)
        vb_ref[...] = va.astype(BF16)
        ckvb_ref[...] = ckvb
        krb_ref[...] = kr_pad[:, :QK_ROPE].astype(BF16)
        qn_ref[...] = qfull[:, :COLS_QN].astype(BF16)
        qr_ref[...] = _rope(qfull[:, COLS_QN:], c32, s32, QK_ROPE).astype(BF16)
    k_ref[...] = ka
    v_ref[...] = va
    ckv_ref[...] = ckv
    kr_ref[...] = kr_pad[:, :QK_ROPE]
    gate_ref[...] = gates


def _inproj(x, w, tabs, *, tm, n_pos_tiles, prompt):
    t = x.shape[0]
    tok = lambda i: (i, 0)
    full = lambda i: (0, 0)
    pos = lambda i: (i % n_pos_tiles, 0)
    tile3 = lambda i: (i, 0, 0)
    wuq = w['w_uq_g'] if prompt else w['w_uq_p']
    ins = [x, w['g_attn'], w['w_pack'], w['g_q'], w['g_kv'], wuq, tabs['c64'], tabs['s64'], tabs['c32'], tabs['s32']]
    in_specs = [pl.BlockSpec((tm, D_MODEL), tok), pl.BlockSpec((1, D_MODEL), full),
                pl.BlockSpec((D_MODEL, PACK_COLS), full), pl.BlockSpec((1, Q_LORA), full),
                pl.BlockSpec((1, KV_LORA), full), pl.BlockSpec(wuq.shape, full)] + [pl.BlockSpec((tm, LANES), pos)] * 4
    leaves = [(COLS_A, F32), (COLS_A, F32), (KV_LORA, F32), (QK_ROPE, F32)]
    out_specs = [pl.BlockSpec((tm, c), tok) for c, _ in leaves]
    out_shape = [jax.ShapeDtypeStruct((t, c), d) for c, d in leaves]

    def add2d(c):
        out_specs.append(pl.BlockSpec((tm, c), tok))
        out_shape.append(jax.ShapeDtypeStruct((t, c), BF16))

    def add3d(c):
        out_specs.append(pl.BlockSpec((tm // TT, c, TT), tile3))
        out_shape.append(jax.ShapeDtypeStruct((t // TT, c, TT), BF16))

    if prompt:
        ins += [tabs['cq'], tabs['sq'], w['w_kn'], w['w_krp'], w['w_uv_all']]
        in_specs += [pl.BlockSpec((tm, LANES), pos)] * 2 + [pl.BlockSpec(w[n].shape, full)
                                                             for n in ('w_kn', 'w_krp', 'w_uv_all')]
        add3d(COLS_A), add2d(COLS_A), add3d(COLS_A), add3d(COLS_QB), add2d(COLS_QB), add3d(COLS_VB)
    else:
        for c in (COLS_A, COLS_A, COLS_A, KV_LORA, QK_ROPE, COLS_QN, COLS_QR):
            add2d(c)
    add2d(2 * D_MODEL)
    return pl.pallas_call(
        functools.partial(_inproj_kernel, prompt=prompt),
        grid=(t // tm,),
        in_specs=in_specs,
        out_specs=out_specs,
        out_shape=out_shape,
        compiler_params=_cparams(("parallel",)),
        name="inproj_prompt" if prompt else "inproj_sample",
    )(*ins)


def _attn_t_kernel(*refs, n_heads, dv, n_maps, out_scale):
    if n_maps == 2:
        lam_ref, qt_ref, k_ref, vt_ref, g_ref, o_ref, m_sc, l_sc, acc_sc = refs
    else:
        qt_ref, k_ref, vt_ref, o_ref, m_sc, l_sc, acc_sc = refs
    i = pl.program_id(1)
    qsts = []
    for h in range(n_heads):
        qt = qt_ref[0, h * LANES:(h + 1) * LANES, :]
        if n_maps == 2:
            row = lax.broadcasted_iota(jnp.int32, qt.shape, 0)
            zero = jnp.zeros_like(qt)
            qt = jnp.concatenate([jnp.where(row < DH_A, qt, zero), jnp.where(row >= DH_A, qt, zero)], axis=1)
        qsts.append(qt)
    m_sc[...] = jnp.full(m_sc.shape, NEG_INF, F32)
    l_sc[...] = jnp.zeros(l_sc.shape, F32)
    acc_sc[...] = jnp.zeros(acc_sc.shape, F32)

    def step(j, masked):
        rows = pl.ds(pl.multiple_of(j * TT, TT), TT)
        heads = range(n_heads)
        sts = [jnp.dot(k_ref[rows, h * LANES:(h + 1) * LANES], qsts[h], preferred_element_type=F32)
               for h in heads]
        pts, alphas = [], []
        for h in heads:
            st = sts[h]
            if masked:
                kk = lax.broadcasted_iota(jnp.int32, st.shape, 0)
                qq = lax.broadcasted_iota(jnp.int32, st.shape, 1)
                st = jnp.where(((qq & (TT - 1)) // CHUNK) >= (kk // CHUNK), st, NEG_INF)
            m = m_sc[h]
            m_new = jnp.maximum(m, jnp.max(st, axis=0, keepdims=True))
            alpha = jnp.exp2(m - m_new)
            pt = jnp.exp2(st - m_new)
            l_sc[h] = alpha * l_sc[h] + jnp.sum(pt, axis=0, keepdims=True)
            m_sc[h] = m_new
            pts.append(pt.astype(BF16))
            alphas.append(alpha)
        pvs = [jnp.dot(vt_ref[j, h * dv:(h + 1) * dv, :], pts[h], preferred_element_type=F32) for h in heads]
        for h in heads:
            acc_sc[h] = alphas[h] * acc_sc[h] + pvs[h]

    def body(j, carry):
        step(j, False)
        return carry

    lax.fori_loop(0, i, body, 0)
    step(i, True)

    for h in range(n_heads):
        inv = acc_sc[h] / l_sc[h]
        if n_maps == 2:
            o = inv[:, :TT] - lam_ref[0] * inv[:, TT:]
            o = o * lax.rsqrt(jnp.mean(o * o, axis=0, keepdims=True) + EPS) * g_ref[...] * out_scale
        else:
            o = inv
        o_ref[0, h * dv:(h + 1) * dv, :] = o.astype(o_ref.dtype)


def _attn_t(qt, k, vt, *, batch, seq, n_heads, dv, n_maps, lam=None, g=None, out_scale=1.0, name):
    nq = seq // TT
    width = n_maps * TT
    once = pl.Buffered(1)
    ins, in_specs = [], []
    if n_maps == 2:
        ins.append(lam)
        in_specs.append(pl.BlockSpec(memory_space=pltpu.SMEM))
    ins += [qt, k, vt]
    in_specs += [pl.BlockSpec((1, n_heads * LANES, TT), lambda b, i: (b * nq + i, 0, 0)),
                 pl.BlockSpec((seq, n_heads * LANES), lambda b, i: (b, 0), pipeline_mode=once),
                 pl.BlockSpec((nq, n_heads * dv, TT), lambda b, i: (b, 0, 0), pipeline_mode=once)]
    if n_maps == 2:
        ins.append(g)
        in_specs.append(pl.BlockSpec((dv, 1), lambda b, i: (0, 0)))
    return pl.pallas_call(
        functools.partial(_attn_t_kernel, n_heads=n_heads, dv=dv, n_maps=n_maps, out_scale=out_scale),
        grid=(batch, nq),
        in_specs=in_specs,
        out_specs=pl.BlockSpec((1, n_heads * dv, TT), lambda b, i: (b * nq + i, 0, 0)),
        out_shape=jax.ShapeDtypeStruct((batch * nq, n_heads * dv, TT), BF16),
        scratch_shapes=[pltpu.VMEM((n_heads, 1, width), F32), pltpu.VMEM((n_heads, 1, width), F32),
                        pltpu.VMEM((n_heads, dv, width), F32)],
        compiler_params=_cparams(("parallel", "arbitrary")),
        name=name,
    )(*ins)


def _flash_step(s, v, m_sc, l_sc, acc_sc):
    m_prev = m_sc[...]
    m_new = jnp.maximum(m_prev, jnp.max(s, axis=-1, keepdims=True))
    alpha = jnp.exp2(m_prev - m_new)
    cols = s.shape[-1]
    p = jnp.exp2(s - (_widen(m_new, cols) if cols % LANES == 0 else m_new[:, :cols]))
    l_sc[...] = alpha * l_sc[...] + jnp.sum(p, axis=-1, keepdims=True)
    pv = jnp.dot(p.astype(BF16), v, preferred_element_type=F32)
    acc_sc[...] = _widen(alpha, acc_sc.shape[-1]) * acc_sc[...] + pv
    m_sc[...] = m_new


def _init_flash(m_sc, l_sc, acc_sc):
    m_sc[...] = jnp.full(m_sc.shape, NEG_INF, F32)
    l_sc[...] = jnp.zeros(l_sc.shape, F32)
    acc_sc[...] = jnp.zeros(acc_sc.shape, F32)


def _stack_maps(q):
    lane = lax.broadcasted_iota(jnp.int32, q.shape, 1)
    zero = jnp.zeros_like(q)
    return jnp.concatenate([jnp.where(lane < DH_A, q, zero), jnp.where(lane >= DH_A, q, zero)], axis=0)


def _diff_sample_kernel(lam_ref, q_ref, kc_ref, vc_ref, kn_ref, vn_ref, g_ref, o_ref, m_sc, l_sc, acc_sc,
                        *, tq, tk, n_cache_tiles, out_scale):
    qs = _stack_maps(q_ref[...])
    _init_flash(m_sc, l_sc, acc_sc)

    def body(j, carry):
        rows = pl.ds(pl.multiple_of(j * tk, tk), tk)
        k = kc_ref[0, rows, :].astype(BF16)
        v = vc_ref[0, rows, :].astype(BF16)
        _flash_step(_dot_nt(qs, k), v, m_sc, l_sc, acc_sc)
        return carry

    lax.fori_loop(0, n_cache_tiles, body, 0)
    _flash_step(_dot_nt(qs, kn_ref[...]), vn_ref[...], m_sc, l_sc, acc_sc)
    inv = acc_sc[...] / l_sc[...]
    o = inv[:tq] - lam_ref[0] * inv[tq:]
    o_ref[...] = (_rms(o, g_ref[...]) * out_scale).astype(o_ref.dtype)


def _diff_sample(lam, qa, kcache, vcache, kb, vb, g_subln, *, layer, batch, seq, past, out_scale):
    tk = min(512, past)
    smem = pl.BlockSpec(memory_space=pltpu.SMEM)
    return pl.pallas_call(
        functools.partial(_diff_sample_kernel, tq=seq, tk=tk, n_cache_tiles=past // tk, out_scale=out_scale),
        grid=(batch, H_A),
        in_specs=[smem,
                  pl.BlockSpec((seq, DV_A), lambda b, h: (b, h)),
                  pl.BlockSpec((1, past, DV_A), lambda b, h: (layer * batch + b, 0, h)),
                  pl.BlockSpec((1, past, DV_A), lambda b, h: (layer * batch + b, 0, h)),
                  pl.BlockSpec((seq, DV_A), lambda b, h: (b, h)),
                  pl.BlockSpec((seq, DV_A), lambda b, h: (b, h)),
                  pl.BlockSpec((1, DV_A), lambda b, h: (0, 0))],
        out_specs=pl.BlockSpec((seq, DV_A), lambda b, h: (b, h)),
        out_shape=jax.ShapeDtypeStruct((batch * seq, COLS_A), BF16),
        scratch_shapes=[pltpu.VMEM((2 * seq, LANES), F32), pltpu.VMEM((2 * seq, LANES), F32),
                        pltpu.VMEM((2 * seq, DV_A), F32)],
        compiler_params=_cparams(("parallel", "parallel")),
        name="diff_attn_sample",
    )(lam, qa, kcache, vcache, kb, vb, g_subln)


def _mla_sample_kernel(qn_ref, qr_ref, cc_ref, rc_ref, cn_ref, rn_ref, wuk_ref, wz_ref, o_ref,
                       ql_sc, qr_sc, m_sc, l_sc, acc_sc, *, tq, tk, n_cache_tiles):
    qn = qn_ref[...]
    qr = qr_ref[...].astype(F32)
    for h in range(H_B):
        ql = jnp.dot(qn[:, h * QK_NOPE:(h + 1) * QK_NOPE], wuk_ref[h], preferred_element_type=F32)
        ql_sc[h * tq:(h + 1) * tq, :] = ql.astype(BF16)
        qr_sc[h * tq:(h + 1) * tq, :] = qr[:, h * QK_ROPE:(h + 1) * QK_ROPE].astype(BF16)
    _init_flash(m_sc, l_sc, acc_sc)

    def body(j, carry):
        rows = pl.ds(pl.multiple_of(j * tk, tk), tk)
        c = cc_ref[0, rows, :].astype(BF16)
        r = rc_ref[0, rows, :].astype(BF16)
        _flash_step(_dot_nt(ql_sc[...], c) + _dot_nt(qr_sc[...], r), c, m_sc, l_sc, acc_sc)
        return carry

    lax.fori_loop(0, n_cache_tiles, body, 0)
    c = cn_ref[...]
    _flash_step(_dot_nt(ql_sc[...], c) + _dot_nt(qr_sc[...], rn_ref[...]), c, m_sc, l_sc, acc_sc)
    o_lat = (acc_sc[...] / _widen(l_sc[...], KV_LORA)).astype(BF16)
    ob = jnp.zeros(o_ref.shape, F32)
    for h in range(H_B):
        ob += jnp.dot(o_lat[h * tq:(h + 1) * tq, :], wz_ref[h], preferred_element_type=F32)
    o_ref[...] = ob.astype(o_ref.dtype)


def _mla_sample(qn, qr, ccache, rcache, ckvb, krb, wuk_t, wz, *, layer, batch, seq, past):
    tk = min(512, past)
    rows = H_B * seq
    return pl.pallas_call(
        functools.partial(_mla_sample_kernel, tq=seq, tk=tk, n_cache_tiles=past // tk),
        grid=(batch,),
        in_specs=[pl.BlockSpec((seq, COLS_QN), lambda b: (b, 0)),
                  pl.BlockSpec((seq, COLS_QR), lambda b: (b, 0)),
                  pl.BlockSpec((1, past, KV_LORA), lambda b: (layer * batch + b, 0, 0)),
                  pl.BlockSpec((1, past, QK_ROPE), lambda b: (layer * batch + b, 0, 0)),
                  pl.BlockSpec((seq, KV_LORA), lambda b: (b, 0)),
                  pl.BlockSpec((seq, QK_ROPE), lambda b: (b, 0)),
                  pl.BlockSpec((H_B, QK_NOPE, KV_LORA), lambda b: (0, 0, 0)),
                  pl.BlockSpec((H_B, KV_LORA, COLS_VB), lambda b: (0, 0, 0))],
        out_specs=pl.BlockSpec((seq, COLS_VB), lambda b: (b, 0)),
        out_shape=jax.ShapeDtypeStruct((batch * seq, COLS_VB), BF16),
        scratch_shapes=[pltpu.VMEM((rows, KV_LORA), BF16), pltpu.VMEM((rows, QK_ROPE), BF16),
                        pltpu.VMEM((rows, LANES), F32), pltpu.VMEM((rows, LANES), F32),
                        pltpu.VMEM((rows, KV_LORA), F32)],
        compiler_params=_cparams(("parallel",)),
        name="mla_attn_sample",
    )(qn, qr, ccache, rcache, ckvb, krb, wuk_t, wz)


def _merge_kernel(oa_ref, ob_ref, gate_ref, x_ref, wa_ref, wb_ref, wo_ref, gf_ref, wr_ref, br_ref,
                  x1_ref, h_ref, route_ref, *, tiled_t):
    oa = _load_tiles_t(oa_ref) if tiled_t else oa_ref[...]
    ob = _load_tiles_t(ob_ref) if tiled_t else ob_ref[...]
    ya = jnp.dot(oa, wa_ref[...], preferred_element_type=F32)
    yb = jnp.dot(ob, wb_ref[...], preferred_element_type=F32)
    gates = gate_ref[...].astype(F32)
    merged = (gates[:, :D_MODEL] * ya + gates[:, D_MODEL:] * yb).astype(BF16)
    x1 = x_ref[...] + jnp.dot(merged, wo_ref[...], preferred_element_type=F32)
    x1_ref[...] = x1
    h = _rms(x1, gf_ref[...])
    h_ref[...] = h.astype(BF16)

    logits = jnp.dot(h, wr_ref[...], preferred_element_type=F32, precision=lax.Precision.HIGHEST) + br_ref[...]
    lane = lax.broadcasted_iota(jnp.int32, logits.shape, 1)
    big = jnp.int32(ROUTE_COLS)

    def top1(mask):
        v = jnp.max(jnp.where(mask, logits, NEG_INF), axis=-1, keepdims=True)
        idx = jnp.min(jnp.where(mask & (logits == v), lane, big), axis=-1, keepdims=True)
        return v, idx

    gmask = lane < N_GROUPS
    gmax, gidx = top1(gmask)
    g_w = 1.0 / jnp.sum(jnp.where(gmask, jnp.exp(logits - gmax), 0.0), axis=-1, keepdims=True)
    lo = N_GROUPS + gidx * EXPERTS_PER_GROUP
    emask = (lane >= lo) & (lane < lo + EXPERTS_PER_GROUP)
    v1, i1 = top1(emask)
    v2, i2 = top1(emask & (lane != i1))
    e2 = jnp.exp(v2 - v1)
    w1 = g_w / (1.0 + e2)
    w2 = g_w * e2 / (1.0 + e2)
    route = jnp.where(lane == 0, (i1 - N_GROUPS).astype(F32),
                      jnp.where(lane == 1, (i2 - N_GROUPS).astype(F32),
                                jnp.where(lane == 2, w1, jnp.where(lane == 3, w2, 0.0))))
    route_ref[...] = route


def _merge(oa, ob, gates, x, w, *, tm, tiled_t):
    t = x.shape[0]
    tok = lambda i: (i, 0)
    full = lambda i: (0, 0)
    if tiled_t:
        o_specs = [pl.BlockSpec((tm // TT, COLS_A, TT), lambda i: (i, 0, 0)),
                   pl.BlockSpec((tm // TT, COLS_VB, TT), lambda i: (i, 0, 0))]
    else:
        o_specs = [pl.BlockSpec((tm, COLS_A), tok), pl.BlockSpec((tm, COLS_VB), tok)]
    return pl.pallas_call(
        functools.partial(_merge_kernel, tiled_t=tiled_t),
        grid=(t // tm,),
        in_specs=o_specs + [
            pl.BlockSpec((tm, 2 * D_MODEL), tok), pl.BlockSpec((tm, D_MODEL), tok),
            pl.BlockSpec((COLS_A, D_MODEL), full), pl.BlockSpec((COLS_VB, D_MODEL), full),
            pl.BlockSpec((D_MODEL, D_MODEL), full), pl.BlockSpec((1, D_MODEL), full),
            pl.BlockSpec((D_MODEL, ROUTE_COLS), full), pl.BlockSpec((1, ROUTE_COLS), full)],
        out_specs=[pl.BlockSpec((tm, D_MODEL), tok), pl.BlockSpec((tm, D_MODEL), tok),
                   pl.BlockSpec((tm, ROUTE_COLS), tok)],
        out_shape=[jax.ShapeDtypeStruct((t, D_MODEL), F32), jax.ShapeDtypeStruct((t, D_MODEL), BF16),
                   jax.ShapeDtypeStruct((t, ROUTE_COLS), F32)],
        compiler_params=_cparams(("parallel",)),
        name="merge_router",
    )(oa, ob, gates, x, w['wa'], w['wb'], w['wo'], w['g_ffn'], w['w_route'], w['b_route'])


def _expert_kernel(blk_e_ref, n_used_ref, x_ref, wgu_ref, wdn_ref, y_ref):
    i = pl.program_id(0)

    @pl.when(i < n_used_ref[0])
    def _():
        gu = jnp.dot(x_ref[...], wgu_ref[0], preferred_element_type=F32)
        gate, up = gu[:, :D_EXPERT], gu[:, D_EXPERT:]
        a = (gate * jax.nn.sigmoid(gate) * up).astype(BF16)
        y_ref[...] = jnp.dot(a, wdn_ref[0], preferred_element_type=F32)

    @pl.when(i >= n_used_ref[0])
    def _():
        y_ref[...] = jnp.zeros(y_ref.shape, y_ref.dtype)


def _experts(blk_e, n_used, xs, wgu, wdn, *, layer, bm):
    n_slots = xs.shape[0]
    grid_spec = pltpu.PrefetchScalarGridSpec(
        num_scalar_prefetch=2,
        grid=(n_slots // bm,),
        in_specs=[pl.BlockSpec((bm, D_MODEL), lambda i, be, nu: (i, 0)),
                  pl.BlockSpec((1, D_MODEL, 2 * D_EXPERT), lambda i, be, nu: (layer * N_EXPERTS + be[i], 0, 0)),
                  pl.BlockSpec((1, D_EXPERT, D_MODEL), lambda i, be, nu: (layer * N_EXPERTS + be[i], 0, 0))],
        out_specs=pl.BlockSpec((bm, D_MODEL), lambda i, be, nu: (i, 0)),
    )
    return pl.pallas_call(
        _expert_kernel,
        grid_spec=grid_spec,
        out_shape=jax.ShapeDtypeStruct((n_slots, D_MODEL), F32),
        compiler_params=_cparams(("arbitrary",)),
        name="experts",
    )(blk_e, n_used, xs, wgu, wdn)


def _combine_kernel(x_ref, y0_ref, y1_ref, route_ref, g_ref, o_ref, *, final_norm):
    r = route_ref[...]
    y = x_ref[...] + r[:, 2:3] * y0_ref[...] + r[:, 3:4] * y1_ref[...]
    if final_norm:
        y = _rms(y, g_ref[...])
    o_ref[...] = y


def _combine(x1, y0, y1, route, g_final, *, tm, final_norm):
    t = x1.shape[0]
    tok = lambda i: (i, 0)
    return pl.pallas_call(
        functools.partial(_combine_kernel, final_norm=final_norm),
        grid=(t // tm,),
        in_specs=[pl.BlockSpec((tm, D_MODEL), tok), pl.BlockSpec((tm, D_MODEL), tok),
                  pl.BlockSpec((tm, D_MODEL), tok), pl.BlockSpec((tm, ROUTE_COLS), tok),
                  pl.BlockSpec((1, D_MODEL), lambda i: (0, 0))],
        out_specs=pl.BlockSpec((tm, D_MODEL), tok),
        out_shape=jax.ShapeDtypeStruct((t, D_MODEL), F32),
        compiler_params=_cparams(("parallel",)),
        name="combine",
    )(x1, y0, y1, route, g_final)


def _rope_tables(pos):
    def tab(chunk):
        half = chunk // 2
        inv = ROPE_THETA ** (-jnp.arange(half, dtype=F32) / half)
        ang = pos.astype(F32)[:, None] * inv[None, :]
        return (jnp.concatenate([jnp.cos(ang), jnp.cos(ang)], axis=-1),
                jnp.concatenate([-jnp.sin(ang), jnp.sin(ang)], axis=-1))

    c64, s64 = tab(DH_A)
    c32, s32 = tab(QK_ROPE)
    n = pos.shape[0]
    ones = jnp.ones((n, QK_NOPE), F32)
    zeros = jnp.zeros((n, QK_NOPE), F32)
    pad = LANES - QK_NOPE - QK_ROPE
    return dict(c64=jnp.tile(c64, (1, LANES // DH_A)), s64=jnp.tile(s64, (1, LANES // DH_A)),
                c32=jnp.tile(c32, (1, LANES // QK_ROPE)), s32=jnp.tile(s32, (1, LANES // QK_ROPE)),
                cq=jnp.concatenate([ones, c32, ones[:, :pad]], axis=-1),
                sq=jnp.concatenate([zeros, s32, zeros[:, :pad]], axis=-1))


def _moe_plan(route, bm):
    n_tok = route.shape[0]
    n_asg = n_tok * TOP_K
    flat_e = route[:, :TOP_K].astype(jnp.int32).reshape(n_asg)
    onehot = (flat_e[:, None] == jnp.arange(N_EXPERTS, dtype=jnp.int32)[None, :]).astype(jnp.int32)
    csum = jnp.cumsum(onehot, axis=0)
    rank = jnp.take_along_axis(csum, flat_e[:, None], axis=1)[:, 0] - 1
    counts = csum[-1]
    padded = (counts + bm - 1) // bm * bm
    pad_end = jnp.cumsum(padded)
    pad_start = pad_end - padded
    dest = (pad_start[flat_e] + rank).astype(jnp.int32)
    n_slots = n_asg + N_EXPERTS * bm
    n_blk = n_slots // bm
    slot_tok = jnp.zeros((n_slots,), jnp.int32).at[dest].set(jnp.arange(n_asg, dtype=jnp.int32) // TOP_K)
    blk_e = jnp.minimum(jnp.searchsorted(pad_end, jnp.arange(n_blk, dtype=jnp.int32) * bm, side='right'),
                        N_EXPERTS - 1).astype(jnp.int32)
    n_used = (pad_end[-1:] // bm).astype(jnp.int32)
    return slot_tok, blk_e, n_used, dest.reshape(n_tok, TOP_K)


def _head_groups(parts):
    rows = parts[0].shape[0]
    used = sum(p.shape[-1] for p in parts)
    pad = jnp.zeros((rows, H_B, LANES - used), parts[0].dtype)
    return jnp.concatenate(list(parts) + [pad], axis=-1).reshape(rows, H_B * LANES)


def _layer_weights(l, g_attn, w_in, g_subln, w_a, g_q_lat, w_uq, g_kv_lat, w_uk, w_uv, w_b, w_o, g_ffn,
                   w_group, b_group, w_router, b_router):
    w = w_in[l]
    o_kr = 3 * COLS_A + Q_LORA + KV_LORA
    w_pack = jnp.concatenate(
        [w[:, :o_kr], jnp.pad(w[:, o_kr:o_kr + QK_ROPE], ((0, 0), (0, KR_PAD - QK_ROPE))), w[:, o_kr + QK_ROPE:]],
        axis=1).astype(BF16)
    wq = w_uq[l].reshape(Q_LORA, H_B, QK_NOPE + QK_ROPE)
    w_uq_p = jnp.concatenate([wq[:, :, :QK_NOPE].reshape(Q_LORA, COLS_QN),
                              wq[:, :, QK_NOPE:].reshape(Q_LORA, COLS_QR)], axis=1).astype(BF16)
    w_uq_g = _head_groups([wq]).astype(BF16)
    w_kn = _head_groups([w_uk[l]]).astype(BF16)
    place = jnp.pad(jnp.eye(QK_ROPE, dtype=F32), ((0, KR_PAD - QK_ROPE), (0, 0)))
    w_krp = _head_groups([jnp.zeros((KR_PAD, H_B, QK_NOPE), F32),
                          jnp.broadcast_to(place[:, None, :], (KR_PAD, H_B, QK_ROPE))]).astype(BF16)
    w_uv_all = w_uv[l].reshape(KV_LORA, COLS_VB).astype(BF16)
    wuk_t = jnp.transpose(w_uk[l], (1, 2, 0)).astype(BF16)
    wuv = jnp.transpose(w_uv[l], (1, 0, 2))
    eye = jnp.eye(H_B, dtype=F32)
    wz = (wuv[:, :, None, :] * eye[:, None, :, None]).reshape(H_B, KV_LORA, COLS_VB).astype(BF16)
    w_route = jnp.concatenate(
        [w_group[l], jnp.transpose(w_router[l], (1, 0, 2)).reshape(D_MODEL, N_EXPERTS),
         jnp.zeros((D_MODEL, ROUTE_COLS - N_GROUPS - N_EXPERTS), F32)], axis=1)
    b_route = jnp.concatenate([b_group[l], b_router[l].reshape(N_EXPERTS),
                               jnp.zeros((ROUTE_COLS - N_GROUPS - N_EXPERTS,), F32)])[None, :]
    return dict(g_attn=g_attn[l][None, :], w_pack=w_pack, g_q=g_q_lat[l][None, :], g_kv=g_kv_lat[l][None, :],
                w_uq_p=w_uq_p, w_uq_g=w_uq_g, w_kn=w_kn, w_krp=w_krp, w_uv_all=w_uv_all, wuk_t=wuk_t, wz=wz,
                g_subln=g_subln[l][None, :], g_subln_col=g_subln[l][:, None],
                wa=w_a[l].astype(BF16), wb=w_b[l].astype(BF16), wo=w_o[l].astype(BF16),
                g_ffn=g_ffn[l][None, :], w_route=w_route, b_route=b_route)


def _trunk(x, pos_tile, n_pos_tiles, tm, caches, lws, lams, wgu, wdn, g_final, *, batch, seq, bm):
    depth = len(lws)
    t = batch * seq
    x = x.reshape(t, D_MODEL)
    tabs = _rope_tables(pos_tile)
    prompt = caches is None
    ks, vs, cs, rs = [], [], [], []
    for l in range(depth):
        w = lws[l]
        out_scale = 1.0 - (0.8 - 0.6 * math.exp(-0.3 * l))
        outs = _inproj(x, w, tabs, tm=tm, n_pos_tiles=n_pos_tiles, prompt=prompt)
        k, v, ckv, kr = outs[:4]
        if prompt:
            qat, kb, vat, qbt, kq, vbt, gates = outs[4:]
            oa = _attn_t(qat, kb, vat, batch=batch, seq=seq, n_heads=H_A, dv=DV_A, n_maps=2, lam=lams[l],
                         g=w['g_subln_col'], out_scale=out_scale, name="diff_attn_prompt")
            ob = _attn_t(qbt, kq, vbt, batch=batch, seq=seq, n_heads=H_B, dv=V_B, n_maps=1,
                         name="mla_attn_prompt")
        else:
            qa, kb, vb, ckvb, krb, qn, qr, gates = outs[4:]
            kc, vc, cc, rc, past = caches
            oa = _diff_sample(lams[l], qa, kc, vc, kb, vb, w['g_subln'], layer=l, batch=batch, seq=seq,
                              past=past, out_scale=out_scale)
            ob = _mla_sample(qn, qr, cc, rc, ckvb, krb, w['wuk_t'], w['wz'], layer=l, batch=batch, seq=seq,
                             past=past)
        x1, h, route = _merge(oa, ob, gates, x, w, tm=tm, tiled_t=prompt)
        slot_tok, blk_e, n_used, dest = _moe_plan(route, bm)
        xs = jnp.take(h, slot_tok, axis=0)
        y_slots = _experts(blk_e, n_used, xs, wgu, wdn, layer=l, bm=bm)
        y0 = jnp.take(y_slots, dest[:, 0], axis=0)
        y1 = jnp.take(y_slots, dest[:, 1], axis=0)
        x = _combine(x1, y0, y1, route, g_final, tm=tm, final_norm=(l == depth - 1))
        ks.append(k)
        vs.append(v)
        cs.append(ckv)
        rs.append(kr)
    y = x.reshape(batch, seq, D_MODEL)
    return (y,
            jnp.stack(ks).reshape(depth, batch, seq, H_A, 2, DH_A),
            jnp.stack(vs).reshape(depth, batch, seq, H_A, DV_A),
            jnp.stack(cs).reshape(depth, batch, seq, KV_LORA),
            jnp.stack(rs).reshape(depth, batch, seq, QK_ROPE))


def kernel(x_prompt, x_sample, cache_diff_k, cache_diff_v, cache_mla_ckv, cache_mla_krope, g_attn, w_in,
           lambda_q1, lambda_k1, lambda_q2, lambda_k2, g_subln, w_a, g_q_lat, w_uq, g_kv_lat, w_uk, w_uv, w_b,
           w_o, g_ffn, w_group, b_group, w_router, b_router, w_gate_up, w_down, g_final):
    depth = w_in.shape[0]
    batch, seq, _ = x_prompt.shape
    dec_batch, dec_seq, _ = x_sample.shape
    past = cache_diff_k.shape[2]
    assert dec_seq == CHUNK and past % CHUNK == 0, "sample frames must form exactly one new chunk"
    assert seq % TT == 0

    lws = [_layer_weights(l, g_attn, w_in, g_subln, w_a, g_q_lat, w_uq, g_kv_lat, w_uk, w_uv, w_b, w_o, g_ffn,
                          w_group, b_group, w_router, b_router) for l in range(depth)]
    lams = []
    for l in range(depth):
        lam_init = 0.8 - 0.6 * math.exp(-0.3 * l)
        lam = (jnp.exp(jnp.sum(lambda_q1[l] * lambda_k1[l])) - jnp.exp(jnp.sum(lambda_q2[l] * lambda_k2[l]))
               + lam_init)
        lams.append(lam.reshape(1).astype(F32))
    wgu = w_gate_up.reshape(depth * N_EXPERTS, D_MODEL, 2 * D_EXPERT).astype(BF16)
    wdn = w_down.reshape(depth * N_EXPERTS, D_EXPERT, D_MODEL).astype(BF16)
    g_fin = g_final[None, :]

    tm_p = min(512, seq)
    outs_p = _trunk(x_prompt, jnp.arange(seq, dtype=jnp.int32), seq // tm_p, tm_p, None, lws, lams, wgu, wdn,
                    g_fin, batch=batch, seq=seq, bm=min(512, batch * seq))

    tm_s = min(512, dec_batch * dec_seq)
    pos_tile = jnp.tile(past + jnp.arange(dec_seq, dtype=jnp.int32), tm_s // dec_seq)
    caches = (cache_diff_k.reshape(depth * dec_batch, past, COLS_A),
              cache_diff_v.reshape(depth * dec_batch, past, COLS_A),
              cache_mla_ckv.reshape(depth * dec_batch, past, KV_LORA),
              cache_mla_krope.reshape(depth * dec_batch, past, QK_ROPE), past)
    outs_s = _trunk(x_sample, pos_tile, 1, tm_s, caches, lws, lams, wgu, wdn, g_fin,
                    batch=dec_batch, seq=dec_seq, bm=min(128, dec_batch * dec_seq))

    return (outs_p[0], outs_s[0]) + outs_p[1:] + outs_s[1:]
```

```python
import functools
import math

import jax
import jax.numpy as jnp
from jax import lax
from jax.experimental import pallas as pl
from jax.experimental.pallas import tpu as pltpu

D_MODEL = 1024
CHUNK = 64
ROPE_THETA = 10000.0
EPS = 1e-6
H_A = 4
DH_A = 64
DV_A = 2 * DH_A
H_B = 8
Q_LORA = 384
KV_LORA = 256
QK_NOPE = 64
QK_ROPE = 32
V_B = 64
N_GROUPS = 4
EXPERTS_PER_GROUP = 8
N_EXPERTS = N_GROUPS * EXPERTS_PER_GROUP
TOP_K = 2
D_EXPERT = 512

COLS_A = H_A * 2 * DH_A
COLS_QR = H_B * QK_ROPE
COLS_QN = H_B * QK_NOPE
COLS_VB = H_B * V_B
LANES = 128
COLS_QB = H_B * LANES
KR_PAD = LANES
PACK_COLS = 3 * COLS_A + Q_LORA + KV_LORA + KR_PAD + 2 * D_MODEL
ROUTE_COLS = LANES
TQ = 512
TK = 256
ONES_ROWS = 16
MERGE_SUB = 256
VMEM_LIMIT = 56 * 1024 * 1024
LOG2E = 1.4426950408889634
SCALE_A = DH_A ** -0.5 * LOG2E
SCALE_B = (QK_NOPE + QK_ROPE) ** -0.5 * LOG2E

F32 = jnp.float32
BF16 = jnp.bfloat16
NEG_INF = float("-inf")


def _cparams(sem):
    return pltpu.CompilerParams(dimension_semantics=sem, vmem_limit_bytes=VMEM_LIMIT)


def _rms(x, g):
    return x * lax.rsqrt(jnp.mean(x * x, axis=-1, keepdims=True) + EPS) * g


def _widen(tab, cols):
    reps = cols // LANES
    return tab if reps == 1 else jnp.concatenate([tab] * reps, axis=-1)


def _rope(x, cos, sin_signed, chunk):
    n = x.shape[-1]
    half = chunk // 2
    lane = lax.broadcasted_iota(jnp.int32, x.shape, 1)
    fwd = pltpu.roll(x, n - half, 1)
    bwd = pltpu.roll(x, half, 1)
    swapped = jnp.where((lane & (chunk - 1)) < half, fwd, bwd)
    return x * _widen(cos, n) + swapped * _widen(sin_signed, n)


def _dot_nt(a, b):
    return lax.dot_general(a, b, (((1,), (1,)), ((), ())), preferred_element_type=F32)


def _store_tiles_t(ref, val):
    tile = ref.shape[2]
    for r in range(ref.shape[0]):
        ref[r] = val[r * tile:(r + 1) * tile, :].T.astype(ref.dtype)


def _store_vt_ones(ref, val, n_heads, dv):
    dva = dv + ONES_ROWS
    ones = jnp.ones((ONES_ROWS, TK), ref.dtype)
    for r in range(ref.shape[0]):
        vt = val[r * TK:(r + 1) * TK, :].T.astype(ref.dtype)
        for h in range(n_heads):
            ref[r, h * dva:h * dva + dv, :] = vt[h * dv:(h + 1) * dv, :]
            ref[r, h * dva + dv:(h + 1) * dva, :] = ones


def _load_tiles_t(ref):
    return jnp.concatenate([ref[r].astype(F32).T for r in range(ref.shape[0])], axis=0).astype(BF16)


def _inproj_kernel(*refs, prompt):
    (x_ref, g_ref, w_ref, gq_ref, gkv_ref, wuq_ref, c64_ref, s64_ref, c32_ref, s32_ref) = refs[:10]
    x = x_ref[...]
    hb = _rms(x, g_ref[...]).astype(BF16)

    def proj(lo, hi):
        return jnp.dot(hb, w_ref[:, lo:hi], preferred_element_type=F32)

    c64, s64 = c64_ref[...], s64_ref[...]
    c32, s32 = c32_ref[...], s32_ref[...]
    o = 0
    qa = _rope(proj(o, o + COLS_A), c64, s64, DH_A) * SCALE_A
    o += COLS_A
    ka = _rope(proj(o, o + COLS_A), c64, s64, DH_A)
    o += COLS_A
    va = proj(o, o + COLS_A)
    o += COLS_A
    cq = _rms(proj(o, o + Q_LORA), gq_ref[...]).astype(BF16)
    o += Q_LORA
    qfull = jnp.dot(cq, wuq_ref[...], preferred_element_type=F32) * SCALE_B
    ckv = _rms(proj(o, o + KV_LORA), gkv_ref[...])
    ckvb = ckv.astype(BF16)
    o += KV_LORA
    kr_pad = _rope(proj(o, o + KR_PAD), c32, s32, QK_ROPE)
    o += KR_PAD
    gates = jax.nn.sigmoid(proj(o, o + 2 * D_MODEL)).astype(BF16)

    if prompt:
        (cq_ref, sq_ref, wkn_ref, wkr_ref, wuv_ref,
         k_ref, v_ref, ckv_ref, kr_ref, qat_ref, kb_ref, vat_ref, qbt_ref, kq_ref, vbt_ref, gate_ref) = refs[10:]
        _store_tiles_t(qat_ref, qa)
        kb_ref[...] = ka.astype(BF16)
        _store_vt_ones(vat_ref, va, H_A, DV_A)
        _store_tiles_t(qbt_ref, _rope(qfull, cq_ref[...], sq_ref[...], QK_ROPE))
        kq = (jnp.dot(ckvb, wkn_ref[...], preferred_element_type=F32)
              + jnp.dot(kr_pad.astype(BF16), wkr_ref[...], preferred_element_type=F32))
        kq_ref[...] = kq.astype(BF16)
        _store_vt_ones(vbt_ref, jnp.dot(ckvb, wuv_ref[...], preferred_element_type=F32), H_B, V_B)
    else:
        (k_ref, v_ref, ckv_ref, kr_ref, qa_ref, kb_ref, vb_ref, ckvb_ref, krb_ref, qn_ref, qr_ref,
         gate_ref) = refs[10:]
        qa_ref[...] = qa.astype(BF16)
        kb_ref[...] = ka.astype(BF16)
        vb_ref[...] = va.astype(BF16)
        ckvb_ref[...] = ckvb
        krb_ref[...] = kr_pad[:, :QK_ROPE].astype(BF16)
        qn_ref[...] = qfull[:, :COLS_QN].astype(BF16)
        qr_ref[...] = _rope(qfull[:, COLS_QN:], c32, s32, QK_ROPE).astype(BF16)
    k_ref[...] = ka
    v_ref[...] = va
    ckv_ref[...] = ckv
    kr_ref[...] = kr_pad[:, :QK_ROPE]
    gate_ref[...] = gates


def _inproj(x, w, tabs, *, tm, n_pos_tiles, prompt):
    t = x.shape[0]
    tok = lambda i: (i, 0)
    full = lambda i: (0, 0)
    pos = lambda i: (i % n_pos_tiles, 0)
    tile3 = lambda i: (i, 0, 0)
    wuq = w['w_uq_g'] if prompt else w['w_uq_p']
    ins = [x, w['g_attn'], w['w_pack'], w['g_q'], w['g_kv'], wuq, tabs['c64'], tabs['s64'], tabs['c32'], tabs['s32']]
    in_specs = [pl.BlockSpec((tm, D_MODEL), tok), pl.BlockSpec((1, D_MODEL), full),
                pl.BlockSpec((D_MODEL, PACK_COLS), full), pl.BlockSpec((1, Q_LORA), full),
                pl.BlockSpec((1, KV_LORA), full), pl.BlockSpec(wuq.shape, full)] + [pl.BlockSpec((tm, LANES), pos)] * 4
    leaves = [(COLS_A, F32), (COLS_A, F32), (KV_LORA, F32), (QK_ROPE, F32)]
    out_specs = [pl.BlockSpec((tm, c), tok) for c, _ in leaves]
    out_shape = [jax.ShapeDtypeStruct((t, c), d) for c, d in leaves]

    def add2d(c):
        out_specs.append(pl.BlockSpec((tm, c), tok))
        out_shape.append(jax.ShapeDtypeStruct((t, c), BF16))

    def add3d(c, tile):
        out_specs.append(pl.BlockSpec((tm // tile, c, tile), tile3))
        out_shape.append(jax.ShapeDtypeStruct((t // tile, c, tile), BF16))

    if prompt:
        ins += [tabs['cq'], tabs['sq'], w['w_kn'], w['w_krp'], w['w_uv_all']]
        in_specs += [pl.BlockSpec((tm, LANES), pos)] * 2 + [pl.BlockSpec(w[n].shape, full)
                                                             for n in ('w_kn', 'w_krp', 'w_uv_all')]
        add3d(COLS_A, TQ), add2d(COLS_A), add3d(H_A * (DV_A + ONES_ROWS), TK)
        add3d(COLS_QB, TQ), add2d(COLS_QB), add3d(H_B * (V_B + ONES_ROWS), TK)
    else:
        for c in (COLS_A, COLS_A, COLS_A, KV_LORA, QK_ROPE, COLS_QN, COLS_QR):
            add2d(c)
    add2d(2 * D_MODEL)
    return pl.pallas_call(
        functools.partial(_inproj_kernel, prompt=prompt),
        grid=(t // tm,),
        in_specs=in_specs,
        out_specs=out_specs,
        out_shape=out_shape,
        compiler_params=_cparams(("parallel",)),
        name="inproj_prompt" if prompt else "inproj_sample",
    )(*ins)


def _attn_t_kernel(*refs, n_heads, dv, n_maps, out_scale):
    if n_maps == 2:
        lam_ref, qt_ref, k_ref, vt_ref, g_ref, o_ref, q_sc, m_sc, acc_sc = refs
    else:
        qt_ref, k_ref, vt_ref, o_ref, m_sc, acc_sc = refs
    i = pl.program_id(1)
    dva = dv + ONES_ROWS
    width = n_maps * TQ
    heads = range(n_heads)
    if n_maps == 2:
        for h in heads:
            qt = qt_ref[0, h * LANES:(h + 1) * LANES, :]
            row = lax.broadcasted_iota(jnp.int32, qt.shape, 0)
            zero = jnp.zeros_like(qt)
            q_sc[h, :, :TQ] = jnp.where(row < DH_A, qt, zero)
            q_sc[h, :, TQ:] = jnp.where(row >= DH_A, qt, zero)
    m_sc[...] = jnp.full(m_sc.shape, NEG_INF, F32)
    acc_sc[...] = jnp.zeros(acc_sc.shape, F32)
    rel = ((lax.broadcasted_iota(jnp.int32, (TK, width), 1) & (TQ - 1)) // CHUNK
           - lax.broadcasted_iota(jnp.int32, (TK, width), 0) // CHUNK)

    def step(j, masked):
        rows = pl.ds(pl.multiple_of(j * TK, TK), TK)
        sts = [jnp.dot(k_ref[rows, h * LANES:(h + 1) * LANES],
                       q_sc[h] if n_maps == 2 else qt_ref[0, h * LANES:(h + 1) * LANES, :],
                       preferred_element_type=F32) for h in heads]
        pts, alphas = [], []
        for h in heads:
            st = sts[h]
            if masked:
                st = jnp.where(rel >= j * (TK // CHUNK) - i * (TQ // CHUNK), st, NEG_INF)
            m = m_sc[h]
            m_new = jnp.maximum(m, jnp.max(st, axis=0, keepdims=True))
            alphas.append(jnp.exp2(m - m_new))
            pts.append(jnp.exp2(st - m_new).astype(BF16))
            m_sc[h] = m_new
        pvs = [jnp.dot(vt_ref[j, h * dva:(h + 1) * dva, :], pts[h], preferred_element_type=F32) for h in heads]
        for h in heads:
            acc_sc[h] = alphas[h] * acc_sc[h] + pvs[h]

    def full_tile(j, carry):
        step(j, False)
        return carry

    def diag_tile(j, carry):
        step(j, True)
        return carry

    n_full = i * (TQ // TK)
    lax.fori_loop(0, n_full, full_tile, 0)
    lax.fori_loop(n_full, n_full + TQ // TK, diag_tile, 0)

    for h in heads:
        acc = acc_sc[h]
        inv = acc[:dv] / acc[dv:dv + 1]
        if n_maps == 2:
            o = inv[:, :TQ] - lam_ref[0] * inv[:, TQ:]
            o = o * lax.rsqrt(jnp.mean(o * o, axis=0, keepdims=True) + EPS) * g_ref[...] * out_scale
        else:
            o = inv
        o_ref[0, h * dv:(h + 1) * dv, :] = o.astype(o_ref.dtype)


def _attn_t(qt, k, vt, *, batch, seq, n_heads, dv, n_maps, lam=None, g=None, out_scale=1.0, name):
    nq = seq // TQ
    dva = dv + ONES_ROWS
    width = n_maps * TQ
    once = pl.Buffered(1)
    ins, in_specs, scratch = [], [], []
    if n_maps == 2:
        ins.append(lam)
        in_specs.append(pl.BlockSpec(memory_space=pltpu.SMEM))
        scratch.append(pltpu.VMEM((n_heads, LANES, width), BF16))
    ins += [qt, k, vt]
    in_specs += [pl.BlockSpec((1, n_heads * LANES, TQ), lambda b, i: (b * nq + i, 0, 0)),
                 pl.BlockSpec((seq, n_heads * LANES), lambda b, i: (b, 0), pipeline_mode=once),
                 pl.BlockSpec((seq // TK, n_heads * dva, TK), lambda b, i: (b, 0, 0), pipeline_mode=once)]
    if n_maps == 2:
        ins.append(g)
        in_specs.append(pl.BlockSpec((dv, 1), lambda b, i: (0, 0)))
    return pl.pallas_call(
        functools.partial(_attn_t_kernel, n_heads=n_heads, dv=dv, n_maps=n_maps, out_scale=out_scale),
        grid=(batch, nq),
        in_specs=in_specs,
        out_specs=pl.BlockSpec((1, n_heads * dv, TQ), lambda b, i: (b * nq + i, 0, 0)),
        out_shape=jax.ShapeDtypeStruct((batch * nq, n_heads * dv, TQ), BF16),
        scratch_shapes=scratch + [pltpu.VMEM((n_heads, 1, width), F32), pltpu.VMEM((n_heads, dva, width), F32)],
        compiler_params=_cparams(("parallel", "arbitrary")),
        name=name,
    )(*ins)


def _flash_steps(ss, vs, m_sc, l_sc, acc_sc):
    ps, alphas = [], []
    for c, s in enumerate(ss):
        m_prev = m_sc[c]
        m_new = jnp.maximum(m_prev, jnp.max(s, axis=-1, keepdims=True))
        alpha = jnp.exp2(m_prev - m_new)
        cols = s.shape[-1]
        p = jnp.exp2(s - (_widen(m_new, cols) if cols % LANES == 0 else m_new[:, :cols]))
        l_sc[c] = alpha * l_sc[c] + jnp.sum(p, axis=-1, keepdims=True)
        m_sc[c] = m_new
        ps.append(p.astype(BF16))
        alphas.append(alpha)
    pvs = [jnp.dot(p, v, preferred_element_type=F32) for p, v in zip(ps, vs)]
    for c, pv in enumerate(pvs):
        acc_sc[c] = _widen(alphas[c], acc_sc.shape[-1]) * acc_sc[c] + pv


def _init_flash(m_sc, l_sc, acc_sc):
    m_sc[...] = jnp.full(m_sc.shape, NEG_INF, F32)
    l_sc[...] = jnp.zeros(l_sc.shape, F32)
    acc_sc[...] = jnp.zeros(acc_sc.shape, F32)


def _stack_maps(q):
    lane = lax.broadcasted_iota(jnp.int32, q.shape, 1)
    zero = jnp.zeros_like(q)
    return jnp.concatenate([jnp.where(lane < DH_A, q, zero), jnp.where(lane >= DH_A, q, zero)], axis=0)


def _diff_sample_kernel(lam_ref, q_ref, kc_ref, vc_ref, kn_ref, vn_ref, g_ref, o_ref, m_sc, l_sc, acc_sc,
                        *, tq, tk, n_cache_tiles, out_scale):
    heads = range(H_A)
    lanes = [slice(h * DV_A, (h + 1) * DV_A) for h in heads]
    qss = [_stack_maps(q_ref[:, lanes[h]]) for h in heads]
    _init_flash(m_sc, l_sc, acc_sc)

    def body(j, carry):
        rows = pl.ds(pl.multiple_of(j * tk, tk), tk)
        ss = [_dot_nt(qss[h], kc_ref[0, rows, lanes[h]]) for h in heads]
        _flash_steps(ss, [vc_ref[0, rows, lanes[h]] for h in heads], m_sc, l_sc, acc_sc)
        return carry

    lax.fori_loop(0, n_cache_tiles, body, 0)
    _flash_steps([_dot_nt(qss[h], kn_ref[:, lanes[h]]) for h in heads], [vn_ref[:, lanes[h]] for h in heads],
                 m_sc, l_sc, acc_sc)
    for h in heads:
        inv = acc_sc[h] / l_sc[h]
        o = inv[:tq] - lam_ref[0] * inv[tq:]
        o_ref[:, lanes[h]] = (_rms(o, g_ref[...]) * out_scale).astype(o_ref.dtype)


def _diff_sample(lam, qa, kcache, vcache, kb, vb, g_subln, *, layer, batch, seq, past, out_scale):
    tk = min(512, past)
    smem = pl.BlockSpec(memory_space=pltpu.SMEM)
    new = pl.BlockSpec((seq, COLS_A), lambda b: (b, 0))
    cache = pl.BlockSpec((1, past, COLS_A), lambda b: (layer * batch + b, 0, 0))
    return pl.pallas_call(
        functools.partial(_diff_sample_kernel, tq=seq, tk=tk, n_cache_tiles=past // tk, out_scale=out_scale),
        grid=(batch,),
        in_specs=[smem, new, cache, cache, new, new, pl.BlockSpec((1, DV_A), lambda b: (0, 0))],
        out_specs=new,
        out_shape=jax.ShapeDtypeStruct((batch * seq, COLS_A), BF16),
        scratch_shapes=[pltpu.VMEM((H_A, 2 * seq, LANES), F32), pltpu.VMEM((H_A, 2 * seq, LANES), F32),
                        pltpu.VMEM((H_A, 2 * seq, DV_A), F32)],
        compiler_params=_cparams(("parallel",)),
        name="diff_attn_sample",
    )(lam, qa, kcache, vcache, kb, vb, g_subln)


def _mla_sample_kernel(qn_ref, qr_ref, cc_ref, rc_ref, cn_ref, rn_ref, wuk_ref, wz_ref, o_ref,
                       ql_sc, qr_sc, m_sc, l_sc, acc_sc, *, tq, tk, n_cache_tiles):
    qn = qn_ref[...]
    qr = qr_ref[...].astype(F32)
    for h in range(H_B):
        ql = jnp.dot(qn[:, h * QK_NOPE:(h + 1) * QK_NOPE], wuk_ref[h], preferred_element_type=F32)
        ql_sc[h * tq:(h + 1) * tq, :] = ql.astype(BF16)
        qr_sc[h * tq:(h + 1) * tq, :] = qr[:, h * QK_ROPE:(h + 1) * QK_ROPE].astype(BF16)
    _init_flash(m_sc, l_sc, acc_sc)
    n_chains = m_sc.shape[0]
    half = H_B * tq // n_chains
    parts = [slice(c * half, (c + 1) * half) for c in range(n_chains)]

    def update(c, r):
        ss = [_dot_nt(ql_sc[p, :], c) + _dot_nt(qr_sc[p, :], r) for p in parts]
        _flash_steps(ss, [c] * n_chains, m_sc, l_sc, acc_sc)

    def body(j, carry):
        rows = pl.ds(pl.multiple_of(j * tk, tk), tk)
        update(cc_ref[0, rows, :].astype(BF16), rc_ref[0, rows, :].astype(BF16))
        return carry

    lax.fori_loop(0, n_cache_tiles, body, 0)
    update(cn_ref[...], rn_ref[...])
    ob = jnp.zeros(o_ref.shape, F32)
    heads_per_chain = H_B // n_chains
    for h in range(H_B):
        c, r = divmod(h, heads_per_chain)
        o_lat = (acc_sc[c, r * tq:(r + 1) * tq, :] / _widen(l_sc[c, r * tq:(r + 1) * tq, :], KV_LORA)).astype(BF16)
        ob += jnp.dot(o_lat, wz_ref[h], preferred_element_type=F32)
    o_ref[...] = ob.astype(o_ref.dtype)


def _mla_sample(qn, qr, ccache, rcache, ckvb, krb, wuk_t, wz, *, layer, batch, seq, past):
    tk = min(512, past)
    rows = H_B * seq
    return pl.pallas_call(
        functools.partial(_mla_sample_kernel, tq=seq, tk=tk, n_cache_tiles=past // tk),
        grid=(batch,),
        in_specs=[pl.BlockSpec((seq, COLS_QN), lambda b: (b, 0)),
                  pl.BlockSpec((seq, COLS_QR), lambda b: (b, 0)),
                  pl.BlockSpec((1, past, KV_LORA), lambda b: (layer * batch + b, 0, 0)),
                  pl.BlockSpec((1, past, QK_ROPE), lambda b: (layer * batch + b, 0, 0)),
                  pl.BlockSpec((seq, KV_LORA), lambda b: (b, 0)),
                  pl.BlockSpec((seq, QK_ROPE), lambda b: (b, 0)),
                  pl.BlockSpec((H_B, QK_NOPE, KV_LORA), lambda b: (0, 0, 0)),
                  pl.BlockSpec((H_B, KV_LORA, COLS_VB), lambda b: (0, 0, 0))],
        out_specs=pl.BlockSpec((seq, COLS_VB), lambda b: (b, 0)),
        out_shape=jax.ShapeDtypeStruct((batch * seq, COLS_VB), BF16),
        scratch_shapes=[pltpu.VMEM((rows, KV_LORA), BF16), pltpu.VMEM((rows, QK_ROPE), BF16),
                        pltpu.VMEM((2, rows // 2, LANES), F32), pltpu.VMEM((2, rows // 2, LANES), F32),
                        pltpu.VMEM((2, rows // 2, KV_LORA), F32)],
        compiler_params=_cparams(("parallel",)),
        name="mla_attn_sample",
    )(qn, qr, ccache, rcache, ckvb, krb, wuk_t, wz)


def _merge_kernel(oa_ref, ob_ref, gate_ref, x_ref, wa_ref, wb_ref, wo_ref, gf_ref, wr_ref, br_ref, tri_ref,
                  x1_ref, h_ref, route_ref, count_ref, run_sc, *, tiled_t):
    oa = _load_tiles_t(oa_ref) if tiled_t else oa_ref[...]
    ob = _load_tiles_t(ob_ref) if tiled_t else ob_ref[...]
    tm = x_ref.shape[0]
    subs = [slice(r * MERGE_SUB, (r + 1) * MERGE_SUB) for r in range(tm // MERGE_SUB)]
    yas = [jnp.dot(oa[s], wa_ref[...], preferred_element_type=F32) for s in subs]
    ybs = [jnp.dot(ob[s], wb_ref[...], preferred_element_type=F32) for s in subs]
    merged = []
    for s, ya, yb in zip(subs, yas, ybs):
        gates = gate_ref[s, :].astype(F32)
        merged.append((gates[:, :D_MODEL] * ya + gates[:, D_MODEL:] * yb).astype(BF16))
    x1s = [x_ref[s, :] + jnp.dot(mg, wo_ref[...], preferred_element_type=F32) for s, mg in zip(subs, merged)]
    logit_parts = []
    for s, x1 in zip(subs, x1s):
        x1_ref[s, :] = x1
        h = _rms(x1, gf_ref[...])
        h_hi = h.astype(BF16)
        h_ref[s, :] = h_hi
        h_lo = (h - h_hi.astype(F32)).astype(BF16)
        logit_parts.append(jnp.dot(h_hi, wr_ref[0], preferred_element_type=F32)
                           + jnp.dot(h_lo, wr_ref[0], preferred_element_type=F32)
                           + jnp.dot(h_hi, wr_ref[1], preferred_element_type=F32))

    logits = jnp.concatenate(logit_parts, axis=0) + br_ref[...]
    lane = lax.broadcasted_iota(jnp.int32, logits.shape, 1)
    big = jnp.int32(ROUTE_COLS)

    def top1(mask):
        v = jnp.max(jnp.where(mask, logits, NEG_INF), axis=-1, keepdims=True)
        idx = jnp.min(jnp.where(mask & (logits == v), lane, big), axis=-1, keepdims=True)
        return v, idx

    gmask = lane < N_GROUPS
    gmax, gidx = top1(gmask)
    g_w = 1.0 / jnp.sum(jnp.where(gmask, jnp.exp(logits - gmax), 0.0), axis=-1, keepdims=True)
    lo = N_GROUPS + gidx * EXPERTS_PER_GROUP
    emask = (lane >= lo) & (lane < lo + EXPERTS_PER_GROUP)
    v1, i1 = top1(emask)
    v2, i2 = top1(emask & (lane != i1))
    e2 = jnp.exp(v2 - v1)
    w1 = g_w / (1.0 + e2)
    w2 = g_w * e2 / (1.0 + e2)
    @pl.when(pl.program_id(0) == 0)
    def _():
        run_sc[...] = jnp.zeros(run_sc.shape, F32)

    e1 = i1 - N_GROUPS
    e2i = i2 - N_GROUPS
    picks = jnp.where((lane == e1) | (lane == e2i), 1.0, 0.0)
    before = jnp.dot(tri_ref[...], picks.astype(BF16), preferred_element_type=F32) + run_sc[...]
    rank1 = jnp.sum(jnp.where(lane == e1, before, 0.0), axis=-1, keepdims=True)
    rank2 = jnp.sum(jnp.where(lane == e2i, before, 0.0), axis=-1, keepdims=True)
    run_sc[...] = run_sc[...] + jnp.sum(picks, axis=0, keepdims=True)
    count_ref[...] = run_sc[...]

    vals = [e1.astype(F32), e2i.astype(F32), w1, w2, rank1, rank2]
    route = jnp.zeros(logits.shape, F32)
    for n, v in enumerate(vals):
        route = jnp.where(lane == n, v, route)
    route_ref[...] = route


def _merge(oa, ob, gates, x, w, *, tm, tiled_t):
    t = x.shape[0]
    tok = lambda i: (i, 0)
    full = lambda i: (0, 0)
    if tiled_t:
        o_specs = [pl.BlockSpec((tm // TQ, COLS_A, TQ), lambda i: (i, 0, 0)),
                   pl.BlockSpec((tm // TQ, COLS_VB, TQ), lambda i: (i, 0, 0))]
    else:
        o_specs = [pl.BlockSpec((tm, COLS_A), tok), pl.BlockSpec((tm, COLS_VB), tok)]
    return pl.pallas_call(
        functools.partial(_merge_kernel, tiled_t=tiled_t),
        grid=(t // tm,),
        in_specs=o_specs + [
            pl.BlockSpec((tm, 2 * D_MODEL), tok), pl.BlockSpec((tm, D_MODEL), tok),
            pl.BlockSpec((COLS_A, D_MODEL), full), pl.BlockSpec((COLS_VB, D_MODEL), full),
            pl.BlockSpec((D_MODEL, D_MODEL), full), pl.BlockSpec((1, D_MODEL), full),
            pl.BlockSpec((2, D_MODEL, ROUTE_COLS), lambda i: (0, 0, 0)), pl.BlockSpec((1, ROUTE_COLS), full),
            pl.BlockSpec((tm, tm), full)],
        out_specs=[pl.BlockSpec((tm, D_MODEL), tok), pl.BlockSpec((tm, D_MODEL), tok),
                   pl.BlockSpec((tm, ROUTE_COLS), tok), pl.BlockSpec((1, ROUTE_COLS), full)],
        out_shape=[jax.ShapeDtypeStruct((t, D_MODEL), F32), jax.ShapeDtypeStruct((t, D_MODEL), BF16),
                   jax.ShapeDtypeStruct((t, ROUTE_COLS), F32), jax.ShapeDtypeStruct((1, ROUTE_COLS), F32)],
        scratch_shapes=[pltpu.VMEM((1, ROUTE_COLS), F32)],
        compiler_params=_cparams(("arbitrary",)),
        name="merge_router",
    )(oa, ob, gates, x, w['wa'], w['wb'], w['wo'], w['g_ffn'], w['w_route'], w['b_route'],
      jnp.tril(jnp.ones((tm, tm), BF16), -1))


def _expert_kernel(blk_e_ref, n_used_ref, x_ref, wgu_ref, wdn_ref, y_ref):
    i = pl.program_id(0)

    @pl.when(i < n_used_ref[0])
    def _():
        gu = jnp.dot(x_ref[...], wgu_ref[0], preferred_element_type=F32)
        gate, up = gu[:, :D_EXPERT], gu[:, D_EXPERT:]
        a = (gate * jax.nn.sigmoid(gate) * up).astype(BF16)
        y_ref[...] = jnp.dot(a, wdn_ref[0], preferred_element_type=F32)

    @pl.when(i >= n_used_ref[0])
    def _():
        y_ref[...] = jnp.zeros(y_ref.shape, y_ref.dtype)


def _experts(blk_e, n_used, xs, wgu, wdn, *, layer, bm):
    n_slots = xs.shape[0]
    grid_spec = pltpu.PrefetchScalarGridSpec(
        num_scalar_prefetch=2,
        grid=(n_slots // bm,),
        in_specs=[pl.BlockSpec((bm, D_MODEL), lambda i, be, nu: (i, 0)),
                  pl.BlockSpec((1, D_MODEL, 2 * D_EXPERT), lambda i, be, nu: (layer * N_EXPERTS + be[i], 0, 0)),
                  pl.BlockSpec((1, D_EXPERT, D_MODEL), lambda i, be, nu: (layer * N_EXPERTS + be[i], 0, 0))],
        out_specs=pl.BlockSpec((bm, D_MODEL), lambda i, be, nu: (i, 0)),
    )
    return pl.pallas_call(
        _expert_kernel,
        grid_spec=grid_spec,
        out_shape=jax.ShapeDtypeStruct((n_slots, D_MODEL), F32),
        compiler_params=_cparams(("arbitrary",)),
        name="experts",
    )(blk_e, n_used, xs, wgu, wdn)


def _combine_kernel(x_ref, y0_ref, y1_ref, route_ref, g_ref, o_ref, *, final_norm):
    r = route_ref[...]
    y = x_ref[...] + r[:, 2:3] * y0_ref[...] + r[:, 3:4] * y1_ref[...]
    if final_norm:
        y = _rms(y, g_ref[...])
    o_ref[...] = y


def _combine(x1, y0, y1, route, g_final, *, tm, final_norm):
    t = x1.shape[0]
    tok = lambda i: (i, 0)
    return pl.pallas_call(
        functools.partial(_combine_kernel, final_norm=final_norm),
        grid=(t // tm,),
        in_specs=[pl.BlockSpec((tm, D_MODEL), tok), pl.BlockSpec((tm, D_MODEL), tok),
                  pl.BlockSpec((tm, D_MODEL), tok), pl.BlockSpec((tm, ROUTE_COLS), tok),
                  pl.BlockSpec((1, D_MODEL), lambda i: (0, 0))],
        out_specs=pl.BlockSpec((tm, D_MODEL), tok),
        out_shape=jax.ShapeDtypeStruct((t, D_MODEL), F32),
        compiler_params=_cparams(("parallel",)),
        name="combine",
    )(x1, y0, y1, route, g_final)


def _rope_tables(pos):
    def tab(chunk):
        half = chunk // 2
        inv = ROPE_THETA ** (-jnp.arange(half, dtype=F32) / half)
        ang = pos.astype(F32)[:, None] * inv[None, :]
        return (jnp.concatenate([jnp.cos(ang), jnp.cos(ang)], axis=-1),
                jnp.concatenate([-jnp.sin(ang), jnp.sin(ang)], axis=-1))

    c64, s64 = tab(DH_A)
    c32, s32 = tab(QK_ROPE)
    n = pos.shape[0]
    ones = jnp.ones((n, QK_NOPE), F32)
    zeros = jnp.zeros((n, QK_NOPE), F32)
    pad = LANES - QK_NOPE - QK_ROPE
    return dict(c64=jnp.tile(c64, (1, LANES // DH_A)), s64=jnp.tile(s64, (1, LANES // DH_A)),
                c32=jnp.tile(c32, (1, LANES // QK_ROPE)), s32=jnp.tile(s32, (1, LANES // QK_ROPE)),
                cq=jnp.concatenate([ones, c32, ones[:, :pad]], axis=-1),
                sq=jnp.concatenate([zeros, s32, zeros[:, :pad]], axis=-1))


def _moe_plan(route, counts, bm):
    n_tok = route.shape[0]
    n_asg = n_tok * TOP_K
    flat_e = route[:, :TOP_K].astype(jnp.int32).reshape(n_asg)
    rank = route[:, 4:4 + TOP_K].astype(jnp.int32).reshape(n_asg)
    counts = counts[0, :N_EXPERTS].astype(jnp.int32)
    padded = (counts + bm - 1) // bm * bm
    pad_end = jnp.cumsum(padded)
    pad_start = pad_end - padded
    dest = (pad_start[flat_e] + rank).astype(jnp.int32)
    n_slots = n_asg + N_EXPERTS * bm
    n_blk = n_slots // bm
    slot_tok = jnp.zeros((n_slots,), jnp.int32).at[dest].set(jnp.arange(n_asg, dtype=jnp.int32) // TOP_K)
    blk_e = jnp.minimum(jnp.searchsorted(pad_end, jnp.arange(n_blk, dtype=jnp.int32) * bm, side='right'),
                        N_EXPERTS - 1).astype(jnp.int32)
    n_used = (pad_end[-1:] // bm).astype(jnp.int32)
    return slot_tok, blk_e, n_used, dest.reshape(n_tok, TOP_K)


def _head_groups(parts):
    rows = parts[0].shape[0]
    used = sum(p.shape[-1] for p in parts)
    pad = jnp.zeros((rows, H_B, LANES - used), parts[0].dtype)
    return jnp.concatenate(list(parts) + [pad], axis=-1).reshape(rows, H_B * LANES)


def _layer_weights(l, g_attn, w_in, g_subln, w_a, g_q_lat, w_uq, g_kv_lat, w_uk, w_uv, w_b, w_o, g_ffn,
                   w_group, b_group, w_router, b_router):
    w = w_in[l]
    o_kr = 3 * COLS_A + Q_LORA + KV_LORA
    w_pack = jnp.concatenate(
        [w[:, :o_kr], jnp.pad(w[:, o_kr:o_kr + QK_ROPE], ((0, 0), (0, KR_PAD - QK_ROPE))), w[:, o_kr + QK_ROPE:]],
        axis=1).astype(BF16)
    wq = w_uq[l].reshape(Q_LORA, H_B, QK_NOPE + QK_ROPE)
    w_uq_p = jnp.concatenate([wq[:, :, :QK_NOPE].reshape(Q_LORA, COLS_QN),
                              wq[:, :, QK_NOPE:].reshape(Q_LORA, COLS_QR)], axis=1).astype(BF16)
    w_uq_g = _head_groups([wq]).astype(BF16)
    w_kn = _head_groups([w_uk[l]]).astype(BF16)
    place = jnp.pad(jnp.eye(QK_ROPE, dtype=F32), ((0, KR_PAD - QK_ROPE), (0, 0)))
    w_krp = _head_groups([jnp.zeros((KR_PAD, H_B, QK_NOPE), F32),
                          jnp.broadcast_to(place[:, None, :], (KR_PAD, H_B, QK_ROPE))]).astype(BF16)
    w_uv_all = w_uv[l].reshape(KV_LORA, COLS_VB).astype(BF16)
    wuk_t = jnp.transpose(w_uk[l], (1, 2, 0)).astype(BF16)
    wuv = jnp.transpose(w_uv[l], (1, 0, 2))
    eye = jnp.eye(H_B, dtype=F32)
    wz = (wuv[:, :, None, :] * eye[:, None, :, None]).reshape(H_B, KV_LORA, COLS_VB).astype(BF16)
    w_route = jnp.concatenate(
        [w_group[l], jnp.transpose(w_router[l], (1, 0, 2)).reshape(D_MODEL, N_EXPERTS),
         jnp.zeros((D_MODEL, ROUTE_COLS - N_GROUPS - N_EXPERTS), F32)], axis=1)
    w_route_hi = w_route.astype(BF16)
    w_route = jnp.stack([w_route_hi, (w_route - w_route_hi.astype(F32)).astype(BF16)])
    b_route = jnp.concatenate([b_group[l], b_router[l].reshape(N_EXPERTS),
                               jnp.zeros((ROUTE_COLS - N_GROUPS - N_EXPERTS,), F32)])[None, :]
    return dict(g_attn=g_attn[l][None, :], w_pack=w_pack, g_q=g_q_lat[l][None, :], g_kv=g_kv_lat[l][None, :],
                w_uq_p=w_uq_p, w_uq_g=w_uq_g, w_kn=w_kn, w_krp=w_krp, w_uv_all=w_uv_all, wuk_t=wuk_t, wz=wz,
                g_subln=g_subln[l][None, :], g_subln_col=g_subln[l][:, None],
                wa=w_a[l].astype(BF16), wb=w_b[l].astype(BF16), wo=w_o[l].astype(BF16),
                g_ffn=g_ffn[l][None, :], w_route=w_route, b_route=b_route)


def _trunk(x, pos_tile, n_pos_tiles, tm, caches, lws, lams, wgu, wdn, g_final, *, batch, seq, bm):
    depth = len(lws)
    t = batch * seq
    x = x.reshape(t, D_MODEL)
    tabs = _rope_tables(pos_tile)
    prompt = caches is None
    ks, vs, cs, rs = [], [], [], []
    for l in range(depth):
        w = lws[l]
        out_scale = 1.0 - (0.8 - 0.6 * math.exp(-0.3 * l))
        outs = _inproj(x, w, tabs, tm=tm, n_pos_tiles=n_pos_tiles, prompt=prompt)
        k, v, ckv, kr = outs[:4]
        if prompt:
            qat, kb, vat, qbt, kq, vbt, gates = outs[4:]
            oa = _attn_t(qat, kb, vat, batch=batch, seq=seq, n_heads=H_A, dv=DV_A, n_maps=2, lam=lams[l],
                         g=w['g_subln_col'], out_scale=out_scale, name="diff_attn_prompt")
            ob = _attn_t(qbt, kq, vbt, batch=batch, seq=seq, n_heads=H_B, dv=V_B, n_maps=1,
                         name="mla_attn_prompt")
        else:
            qa, kb, vb, ckvb, krb, qn, qr, gates = outs[4:]
            kc, vc, cc, rc, past = caches
            oa = _diff_sample(lams[l], qa, kc, vc, kb, vb, w['g_subln'], layer=l, batch=batch, seq=seq,
                              past=past, out_scale=out_scale)
            ob = _mla_sample(qn, qr, cc, rc, ckvb, krb, w['wuk_t'], w['wz'], layer=l, batch=batch, seq=seq,
                             past=past)
        x1, h, route, counts = _merge(oa, ob, gates, x, w, tm=tm, tiled_t=prompt)
        slot_tok, blk_e, n_used, dest = _moe_plan(route, counts, bm)
        xs = jnp.take(h, slot_tok, axis=0)
        y_slots = _experts(blk_e, n_used, xs, wgu, wdn, layer=l, bm=bm)
        y0 = jnp.take(y_slots, dest[:, 0], axis=0)
        y1 = jnp.take(y_slots, dest[:, 1], axis=0)
        x = _combine(x1, y0, y1, route, g_final, tm=tm, final_norm=(l == depth - 1))
        ks.append(k)
        vs.append(v)
        cs.append(ckv)
        rs.append(kr)
    y = x.reshape(batch, seq, D_MODEL)
    return (y,
            jnp.stack(ks).reshape(depth, batch, seq, H_A, 2, DH_A),
            jnp.stack(vs).reshape(depth, batch, seq, H_A, DV_A),
            jnp.stack(cs).reshape(depth, batch, seq, KV_LORA),
            jnp.stack(rs).reshape(depth, batch, seq, QK_ROPE))


def kernel(x_prompt, x_sample, cache_diff_k, cache_diff_v, cache_mla_ckv, cache_mla_krope, g_attn, w_in,
           lambda_q1, lambda_k1, lambda_q2, lambda_k2, g_subln, w_a, g_q_lat, w_uq, g_kv_lat, w_uk, w_uv, w_b,
           w_o, g_ffn, w_group, b_group, w_router, b_router, w_gate_up, w_down, g_final):
    depth = w_in.shape[0]
    batch, seq, _ = x_prompt.shape
    dec_batch, dec_seq, _ = x_sample.shape
    past = cache_diff_k.shape[2]
    assert dec_seq == CHUNK and past % CHUNK == 0, "sample frames must form exactly one new chunk"
    assert seq % TQ == 0

    lws = [_layer_weights(l, g_attn, w_in, g_subln, w_a, g_q_lat, w_uq, g_kv_lat, w_uk, w_uv, w_b, w_o, g_ffn,
                          w_group, b_group, w_router, b_router) for l in range(depth)]
    lams = []
    for l in range(depth):
        lam_init = 0.8 - 0.6 * math.exp(-0.3 * l)
        lam = (jnp.exp(jnp.sum(lambda_q1[l] * lambda_k1[l])) - jnp.exp(jnp.sum(lambda_q2[l] * lambda_k2[l]))
               + lam_init)
        lams.append(lam.reshape(1).astype(F32))
    wgu = w_gate_up.reshape(depth * N_EXPERTS, D_MODEL, 2 * D_EXPERT).astype(BF16)
    wdn = w_down.reshape(depth * N_EXPERTS, D_EXPERT, D_MODEL).astype(BF16)
    g_fin = g_final[None, :]

    tm_p = min(512, seq)
    outs_p = _trunk(x_prompt, jnp.arange(seq, dtype=jnp.int32), seq // tm_p, tm_p, None, lws, lams, wgu, wdn,
                    g_fin, batch=batch, seq=seq, bm=min(512, batch * seq))

    tm_s = min(512, dec_batch * dec_seq)
    pos_tile = jnp.tile(past + jnp.arange(dec_seq, dtype=jnp.int32), tm_s // dec_seq)
    caches = (cache_diff_k.reshape(depth * dec_batch, past, COLS_A).astype(BF16),
              cache_diff_v.reshape(depth * dec_batch, past, COLS_A).astype(BF16),
              cache_mla_ckv.reshape(depth * dec_batch, past, KV_LORA),
              cache_mla_krope.reshape(depth * dec_batch, past, QK_ROPE), past)
    outs_s = _trunk(x_sample, pos_tile, 1, tm_s, caches, lws, lams, wgu, wdn, g_fin,
                    batch=dec_batch, seq=dec_seq, bm=min(128, dec_batch * dec_seq))

    return (outs_p[0], outs_s[0]) + outs_p[1:] + outs_s[1:]
```

```python
import functools
import math

import jax
import jax.numpy as jnp
from jax import lax
from jax.experimental import pallas as pl
from jax.experimental.pallas import tpu as pltpu

D_MODEL = 1024
CHUNK = 64
ROPE_THETA = 10000.0
EPS = 1e-6
H_A = 4
DH_A = 64
DV_A = 2 * DH_A
H_B = 8
Q_LORA = 384
KV_LORA = 256
QK_NOPE = 64
QK_ROPE = 32
V_B = 64
N_GROUPS = 4
EXPERTS_PER_GROUP = 8
N_EXPERTS = N_GROUPS * EXPERTS_PER_GROUP
TOP_K = 2
D_EXPERT = 512

COLS_A = H_A * 2 * DH_A
COLS_QR = H_B * QK_ROPE
COLS_QN = H_B * QK_NOPE
COLS_VB = H_B * V_B
LANES = 128
COLS_QB = H_B * LANES
KR_PAD = LANES
PACK_COLS = 3 * COLS_A + Q_LORA + KV_LORA + KR_PAD + 2 * D_MODEL
ROUTE_COLS = LANES
TQ = 512
TK = 256
ONES_ROWS = 16
MERGE_SUB = 256
VMEM_LIMIT = 56 * 1024 * 1024
LOG2E = 1.4426950408889634
SCALE_A = DH_A ** -0.5 * LOG2E
SCALE_B = (QK_NOPE + QK_ROPE) ** -0.5 * LOG2E

F32 = jnp.float32
BF16 = jnp.bfloat16
NEG_INF = float("-inf")


def _cparams(sem):
    return pltpu.CompilerParams(dimension_semantics=sem, vmem_limit_bytes=VMEM_LIMIT)


def _rms(x, g):
    return x * lax.rsqrt(jnp.mean(x * x, axis=-1, keepdims=True) + EPS) * g


def _widen(tab, cols):
    reps = cols // LANES
    return tab if reps == 1 else jnp.concatenate([tab] * reps, axis=-1)


def _rope(x, cos, sin_signed, chunk):
    n = x.shape[-1]
    half = chunk // 2
    lane = lax.broadcasted_iota(jnp.int32, x.shape, 1)
    fwd = pltpu.roll(x, n - half, 1)
    bwd = pltpu.roll(x, half, 1)
    swapped = jnp.where((lane & (chunk - 1)) < half, fwd, bwd)
    return x * _widen(cos, n) + swapped * _widen(sin_signed, n)


def _dot_nt(a, b):
    return lax.dot_general(a, b, (((1,), (1,)), ((), ())), preferred_element_type=F32)


def _store_tiles_t(ref, val):
    tile = ref.shape[2]
    for r in range(ref.shape[0]):
        ref[r] = val[r * tile:(r + 1) * tile, :].T.astype(ref.dtype)


def _store_vt_ones(ref, val, n_heads, dv):
    dva = dv + ONES_ROWS
    ones = jnp.ones((ONES_ROWS, TK), ref.dtype)
    for r in range(ref.shape[0]):
        vt = val[r * TK:(r + 1) * TK, :].T.astype(ref.dtype)
        for h in range(n_heads):
            ref[r, h * dva:h * dva + dv, :] = vt[h * dv:(h + 1) * dv, :]
            ref[r, h * dva + dv:(h + 1) * dva, :] = ones


def _load_tiles_t(ref):
    return jnp.concatenate([ref[r].astype(F32).T for r in range(ref.shape[0])], axis=0).astype(BF16)


def _inproj_kernel(*refs, prompt):
    (x_ref, g_ref, w_ref, gq_ref, gkv_ref, wuq_ref, c64_ref, s64_ref, c32_ref, s32_ref) = refs[:10]
    x = x_ref[...]
    hb = _rms(x, g_ref[...]).astype(BF16)

    def proj(lo, hi):
        return jnp.dot(hb, w_ref[:, lo:hi], preferred_element_type=F32)

    c64, s64 = c64_ref[...], s64_ref[...]
    c32, s32 = c32_ref[...], s32_ref[...]
    o = 0
    qa = _rope(proj(o, o + COLS_A), c64, s64, DH_A) * SCALE_A
    o += COLS_A
    ka = _rope(proj(o, o + COLS_A), c64, s64, DH_A)
    o += COLS_A
    va = proj(o, o + COLS_A)
    o += COLS_A
    cq = _rms(proj(o, o + Q_LORA), gq_ref[...]).astype(BF16)
    o += Q_LORA
    qfull = jnp.dot(cq, wuq_ref[...], preferred_element_type=F32) * SCALE_B
    ckv = _rms(proj(o, o + KV_LORA), gkv_ref[...])
    ckvb = ckv.astype(BF16)
    o += KV_LORA
    kr_pad = _rope(proj(o, o + KR_PAD), c32, s32, QK_ROPE)
    o += KR_PAD
    gates = jax.nn.sigmoid(proj(o, o + 2 * D_MODEL)).astype(BF16)

    if prompt:
        cq_ref, sq_ref, wkn_ref, wkr_ref, wuv_ref = refs[10:15]
        (k_ref, v_ref, ckv_ref, kr_ref, qat_ref, kb_ref, vat_ref, qbt_ref, kq_ref, vbt_ref,
         gate_ref) = refs[len(refs) - 11:]
        k_ref[...] = ka.T
        for h in range(H_A):
            v_ref[pl.ds(h, va.shape[0], stride=H_A), :] = va[:, h * DV_A:(h + 1) * DV_A]
        ckv_ref[...] = ckv
        kr_ref[...] = kr_pad.T[:QK_ROPE, :]
        _store_tiles_t(qat_ref, qa)
        kb_ref[...] = ka.astype(BF16)
        _store_vt_ones(vat_ref, va, H_A, DV_A)
        _store_tiles_t(qbt_ref, _rope(qfull, cq_ref[...], sq_ref[...], QK_ROPE))
        kq = (jnp.dot(ckvb, wkn_ref[...], preferred_element_type=F32)
              + jnp.dot(kr_pad.astype(BF16), wkr_ref[...], preferred_element_type=F32))
        kq_ref[...] = kq.astype(BF16)
        _store_vt_ones(vbt_ref, jnp.dot(ckvb, wuv_ref[...], preferred_element_type=F32), H_B, V_B)
    else:
        (k_ref, v_ref, ckv_ref, kr_ref, qa_ref, kb_ref, vb_ref, ckvb_ref, krb_ref, qn_ref, qr_ref,
         gate_ref) = refs[10:]
        qa_ref[...] = qa.astype(BF16)
        kb_ref[...] = ka.astype(BF16)
        vb_ref[...] = va.astype(BF16)
        ckvb_ref[...] = ckvb
        krb_ref[...] = kr_pad[:, :QK_ROPE].astype(BF16)
        qn_ref[...] = qfull[:, :COLS_QN].astype(BF16)
        qr_ref[...] = _rope(qfull[:, COLS_QN:], c32, s32, QK_ROPE).astype(BF16)
        k_ref[...] = ka
        v_ref[...] = va
        ckv_ref[...] = ckv
        kr_ref[...] = kr_pad[:, :QK_ROPE]
    gate_ref[...] = gates


def _inproj(x, w, tabs, *, tm, n_pos_tiles, prompt, layer=0, depth=1, batch=1, prev=None):
    t = x.shape[0]
    nt = t // tm
    tok = lambda i: (i, 0)
    full = lambda i: (0, 0)
    pos = lambda i: (i % n_pos_tiles, 0)
    tile3 = lambda i: (i, 0, 0)
    wuq = w['w_uq_g'] if prompt else w['w_uq_p']
    ins = [x, w['g_attn'], w['w_pack'], w['g_q'], w['g_kv'], wuq, tabs['c64'], tabs['s64'], tabs['c32'], tabs['s32']]
    in_specs = [pl.BlockSpec((tm, D_MODEL), tok), pl.BlockSpec((1, D_MODEL), full),
                pl.BlockSpec((D_MODEL, PACK_COLS), full), pl.BlockSpec((1, Q_LORA), full),
                pl.BlockSpec((1, KV_LORA), full), pl.BlockSpec(wuq.shape, full)] + [pl.BlockSpec((tm, LANES), pos)] * 4
    if prompt:
        seq = t // batch
        per_b = seq // tm
        seq_minor = lambda i: (layer * batch + i // per_b, i % per_b)
        tok_l = lambda i: (layer * nt + i, 0)
        out_specs = [pl.BlockSpec((COLS_A, tm), seq_minor), pl.BlockSpec((tm * H_A, DV_A), tok_l),
                     pl.BlockSpec((tm, KV_LORA), tok_l), pl.BlockSpec((QK_ROPE, tm), seq_minor)]
        out_shape = [jax.ShapeDtypeStruct((depth * batch * COLS_A, seq), F32),
                     jax.ShapeDtypeStruct((depth * t * H_A, DV_A), F32),
                     jax.ShapeDtypeStruct((depth * t, KV_LORA), F32),
                     jax.ShapeDtypeStruct((depth * batch * QK_ROPE, seq), F32)]
    else:
        leaves = [(COLS_A, F32), (COLS_A, F32), (KV_LORA, F32), (QK_ROPE, F32)]
        out_specs = [pl.BlockSpec((tm, c), tok) for c, _ in leaves]
        out_shape = [jax.ShapeDtypeStruct((t, c), d) for c, d in leaves]

    def add2d(c):
        out_specs.append(pl.BlockSpec((tm, c), tok))
        out_shape.append(jax.ShapeDtypeStruct((t, c), BF16))

    def add3d(c, tile):
        out_specs.append(pl.BlockSpec((tm // tile, c, tile), tile3))
        out_shape.append(jax.ShapeDtypeStruct((t // tile, c, tile), BF16))

    if prompt:
        ins += [tabs['cq'], tabs['sq'], w['w_kn'], w['w_krp'], w['w_uv_all']]
        in_specs += [pl.BlockSpec((tm, LANES), pos)] * 2 + [pl.BlockSpec(w[n].shape, full)
                                                             for n in ('w_kn', 'w_krp', 'w_uv_all')]
        add3d(COLS_A, TQ), add2d(COLS_A), add3d(H_A * (DV_A + ONES_ROWS), TK)
        add3d(COLS_QB, TQ), add2d(COLS_QB), add3d(H_B * (V_B + ONES_ROWS), TK)
    else:
        for c in (COLS_A, COLS_A, COLS_A, KV_LORA, QK_ROPE, COLS_QN, COLS_QR):
            add2d(c)
    add2d(2 * D_MODEL)
    aliases = {}
    if prev is not None:
        aliases = {len(ins) + n: n for n in range(len(prev))}
        ins += list(prev)
        in_specs += [pl.BlockSpec(memory_space=pl.ANY)] * len(prev)
    return pl.pallas_call(
        functools.partial(_inproj_kernel, prompt=prompt),
        grid=(nt,),
        in_specs=in_specs,
        out_specs=out_specs,
        out_shape=out_shape,
        input_output_aliases=aliases,
        compiler_params=_cparams(("parallel",)),
        name="inproj_prompt" if prompt else "inproj_sample",
    )(*ins)


def _attn_t_kernel(*refs, n_heads, dv, n_maps, out_scale):
    if n_maps == 2:
        lam_ref, qt_ref, k_ref, vt_ref, g_ref, o_ref, q_sc, m_sc, acc_sc = refs
    else:
        qt_ref, k_ref, vt_ref, o_ref, m_sc, acc_sc = refs
    i = pl.program_id(1)
    dva = dv + ONES_ROWS
    width = n_maps * TQ
    heads = range(n_heads)
    if n_maps == 2:
        for h in heads:
            qt = qt_ref[0, h * LANES:(h + 1) * LANES, :]
            row = lax.broadcasted_iota(jnp.int32, qt.shape, 0)
            zero = jnp.zeros_like(qt)
            q_sc[h, :, :TQ] = jnp.where(row < DH_A, qt, zero)
            q_sc[h, :, TQ:] = jnp.where(row >= DH_A, qt, zero)
    m_sc[...] = jnp.full(m_sc.shape, NEG_INF, F32)
    acc_sc[...] = jnp.zeros(acc_sc.shape, F32)
    rel = ((lax.broadcasted_iota(jnp.int32, (TK, width), 1) & (TQ - 1)) // CHUNK
           - lax.broadcasted_iota(jnp.int32, (TK, width), 0) // CHUNK)

    def step(j, masked):
        rows = pl.ds(pl.multiple_of(j * TK, TK), TK)
        sts = [jnp.dot(k_ref[rows, h * LANES:(h + 1) * LANES],
                       q_sc[h] if n_maps == 2 else qt_ref[0, h * LANES:(h + 1) * LANES, :],
                       preferred_element_type=F32) for h in heads]
        pts, alphas = [], []
        for h in heads:
            st = sts[h]
            if masked:
                st = jnp.where(rel >= j * (TK // CHUNK) - i * (TQ // CHUNK), st, NEG_INF)
            m = m_sc[h]
            m_new = jnp.maximum(m, jnp.max(st, axis=0, keepdims=True))
            alphas.append(jnp.exp2(m - m_new))
            pts.append(jnp.exp2(st - m_new).astype(BF16))
            m_sc[h] = m_new
        pvs = [jnp.dot(vt_ref[j, h * dva:(h + 1) * dva, :], pts[h], preferred_element_type=F32) for h in heads]
        for h in heads:
            acc_sc[h] = alphas[h] * acc_sc[h] + pvs[h]

    def full_tile(j, carry):
        step(j, False)
        return carry

    def diag_tile(j, carry):
        step(j, True)
        return carry

    n_full = i * (TQ // TK)
    lax.fori_loop(0, n_full, full_tile, 0)
    lax.fori_loop(n_full, n_full + TQ // TK, diag_tile, 0)

    for h in heads:
        acc = acc_sc[h]
        inv = acc[:dv] / acc[dv:dv + 1]
        if n_maps == 2:
            o = inv[:, :TQ] - lam_ref[0] * inv[:, TQ:]
            o = o * lax.rsqrt(jnp.mean(o * o, axis=0, keepdims=True) + EPS) * g_ref[...] * out_scale
        else:
            o = inv
        o_ref[0, h * dv:(h + 1) * dv, :] = o.astype(o_ref.dtype)


def _attn_t(qt, k, vt, *, batch, seq, n_heads, dv, n_maps, lam=None, g=None, out_scale=1.0, name):
    nq = seq // TQ
    dva = dv + ONES_ROWS
    width = n_maps * TQ
    once = pl.Buffered(1)
    ins, in_specs, scratch = [], [], []
    if n_maps == 2:
        ins.append(lam)
        in_specs.append(pl.BlockSpec(memory_space=pltpu.SMEM))
        scratch.append(pltpu.VMEM((n_heads, LANES, width), BF16))
    ins += [qt, k, vt]
    in_specs += [pl.BlockSpec((1, n_heads * LANES, TQ), lambda b, i: (b * nq + i, 0, 0)),
                 pl.BlockSpec((seq, n_heads * LANES), lambda b, i: (b, 0), pipeline_mode=once),
                 pl.BlockSpec((seq // TK, n_heads * dva, TK), lambda b, i: (b, 0, 0), pipeline_mode=once)]
    if n_maps == 2:
        ins.append(g)
        in_specs.append(pl.BlockSpec((dv, 1), lambda b, i: (0, 0)))
    return pl.pallas_call(
        functools.partial(_attn_t_kernel, n_heads=n_heads, dv=dv, n_maps=n_maps, out_scale=out_scale),
        grid=(batch, nq),
        in_specs=in_specs,
        out_specs=pl.BlockSpec((1, n_heads * dv, TQ), lambda b, i: (b * nq + i, 0, 0)),
        out_shape=jax.ShapeDtypeStruct((batch * nq, n_heads * dv, TQ), BF16),
        scratch_shapes=scratch + [pltpu.VMEM((n_heads, 1, width), F32), pltpu.VMEM((n_heads, dva, width), F32)],
        compiler_params=_cparams(("parallel", "arbitrary")),
        name=name,
    )(*ins)


def _flash_steps(ss, vs, m_sc, l_sc, acc_sc):
    ps, alphas = [], []
    for c, s in enumerate(ss):
        m_prev = m_sc[c]
        m_new = jnp.maximum(m_prev, jnp.max(s, axis=-1, keepdims=True))
        alpha = jnp.exp2(m_prev - m_new)
        cols = s.shape[-1]
        p = jnp.exp2(s - (_widen(m_new, cols) if cols % LANES == 0 else m_new[:, :cols]))
        l_sc[c] = alpha * l_sc[c] + jnp.sum(p, axis=-1, keepdims=True)
        m_sc[c] = m_new
        ps.append(p.astype(BF16))
        alphas.append(alpha)
    pvs = [jnp.dot(p, v, preferred_element_type=F32) for p, v in zip(ps, vs)]
    for c, pv in enumerate(pvs):
        acc_sc[c] = _widen(alphas[c], acc_sc.shape[-1]) * acc_sc[c] + pv


def _init_flash(m_sc, l_sc, acc_sc):
    m_sc[...] = jnp.full(m_sc.shape, NEG_INF, F32)
    l_sc[...] = jnp.zeros(l_sc.shape, F32)
    acc_sc[...] = jnp.zeros(acc_sc.shape, F32)


def _stack_maps(q):
    lane = lax.broadcasted_iota(jnp.int32, q.shape, 1)
    zero = jnp.zeros_like(q)
    return jnp.concatenate([jnp.where(lane < DH_A, q, zero), jnp.where(lane >= DH_A, q, zero)], axis=0)


def _diff_sample_kernel(lam_ref, q_ref, kc_ref, vc_ref, kn_ref, vn_ref, g_ref, o_ref, m_sc, l_sc, acc_sc,
                        *, tq, tk, n_cache_tiles, out_scale):
    heads = range(H_A)
    lanes = [slice(h * DV_A, (h + 1) * DV_A) for h in heads]
    qss = [_stack_maps(q_ref[:, lanes[h]]) for h in heads]
    _init_flash(m_sc, l_sc, acc_sc)

    for j in range(n_cache_tiles):
        ss = [jnp.dot(qss[h], kc_ref[lanes[h], j * tk:(j + 1) * tk].astype(BF16), preferred_element_type=F32)
              for h in heads]
        vs = [vc_ref[0, pl.ds(j * tk * H_A + h, tk, stride=H_A), :].astype(BF16) for h in heads]
        _flash_steps(ss, vs, m_sc, l_sc, acc_sc)
    _flash_steps([_dot_nt(qss[h], kn_ref[:, lanes[h]]) for h in heads], [vn_ref[:, lanes[h]] for h in heads],
                 m_sc, l_sc, acc_sc)
    for h in heads:
        inv = acc_sc[h] / l_sc[h]
        o = inv[:tq] - lam_ref[0] * inv[tq:]
        o_ref[:, lanes[h]] = (_rms(o, g_ref[...]) * out_scale).astype(o_ref.dtype)


def _diff_sample(lam, qa, kcache, vcache, kb, vb, g_subln, *, layer, batch, seq, past, out_scale):
    tk = min(512, past)
    smem = pl.BlockSpec(memory_space=pltpu.SMEM)
    new = pl.BlockSpec((seq, COLS_A), lambda b: (b, 0))
    kcache_spec = pl.BlockSpec((COLS_A, past), lambda b: (layer * batch + b, 0))
    vcache_spec = pl.BlockSpec((1, past * H_A, DV_A), lambda b: (layer * batch + b, 0, 0))
    return pl.pallas_call(
        functools.partial(_diff_sample_kernel, tq=seq, tk=tk, n_cache_tiles=past // tk, out_scale=out_scale),
        grid=(batch,),
        in_specs=[smem, new, kcache_spec, vcache_spec, new, new, pl.BlockSpec((1, DV_A), lambda b: (0, 0))],
        out_specs=new,
        out_shape=jax.ShapeDtypeStruct((batch * seq, COLS_A), BF16),
        scratch_shapes=[pltpu.VMEM((H_A, 2 * seq, LANES), F32), pltpu.VMEM((H_A, 2 * seq, LANES), F32),
                        pltpu.VMEM((H_A, 2 * seq, DV_A), F32)],
        compiler_params=_cparams(("parallel",)),
        name="diff_attn_sample",
    )(lam, qa, kcache, vcache, kb, vb, g_subln)


def _mla_sample_kernel(qn_ref, qr_ref, cc_ref, rc_ref, cn_ref, rn_ref, wuk_ref, wz_ref, o_ref,
                       ql_sc, qr_sc, m_sc, l_sc, acc_sc, *, tq, tk, n_cache_tiles):
    qn = qn_ref[...]
    qr = qr_ref[...].astype(F32)
    for h in range(H_B):
        ql = jnp.dot(qn[:, h * QK_NOPE:(h + 1) * QK_NOPE], wuk_ref[h], preferred_element_type=F32)
        ql_sc[h * tq:(h + 1) * tq, :] = ql.astype(BF16)
        qr_sc[h * tq:(h + 1) * tq, :] = qr[:, h * QK_ROPE:(h + 1) * QK_ROPE].astype(BF16)
    _init_flash(m_sc, l_sc, acc_sc)
    n_chains = m_sc.shape[0]
    half = H_B * tq // n_chains
    parts = [slice(c * half, (c + 1) * half) for c in range(n_chains)]

    for j in range(n_cache_tiles):
        c = cc_ref[0, j * tk:(j + 1) * tk, :].astype(BF16)
        rt = rc_ref[:, j * tk:(j + 1) * tk].astype(BF16)
        ss = [_dot_nt(ql_sc[p, :], c) + jnp.dot(qr_sc[p, :], rt, preferred_element_type=F32) for p in parts]
        _flash_steps(ss, [c] * n_chains, m_sc, l_sc, acc_sc)
    c = cn_ref[...]
    ss = [_dot_nt(ql_sc[p, :], c) + _dot_nt(qr_sc[p, :], rn_ref[...]) for p in parts]
    _flash_steps(ss, [c] * n_chains, m_sc, l_sc, acc_sc)
    ob = jnp.zeros(o_ref.shape, F32)
    heads_per_chain = H_B // n_chains
    for h in range(H_B):
        c, r = divmod(h, heads_per_chain)
        o_lat = (acc_sc[c, r * tq:(r + 1) * tq, :] / _widen(l_sc[c, r * tq:(r + 1) * tq, :], KV_LORA)).astype(BF16)
        ob += jnp.dot(o_lat, wz_ref[h], preferred_element_type=F32)
    o_ref[...] = ob.astype(o_ref.dtype)


def _mla_sample(qn, qr, ccache, rcache, ckvb, krb, wuk_t, wz, *, layer, batch, seq, past):
    tk = min(512, past)
    rows = H_B * seq
    return pl.pallas_call(
        functools.partial(_mla_sample_kernel, tq=seq, tk=tk, n_cache_tiles=past // tk),
        grid=(batch,),
        in_specs=[pl.BlockSpec((seq, COLS_QN), lambda b: (b, 0)),
                  pl.BlockSpec((seq, COLS_QR), lambda b: (b, 0)),
                  pl.BlockSpec((1, past, KV_LORA), lambda b: (layer * batch + b, 0, 0)),
                  pl.BlockSpec((QK_ROPE, past), lambda b: (layer * batch + b, 0)),
                  pl.BlockSpec((seq, KV_LORA), lambda b: (b, 0)),
                  pl.BlockSpec((seq, QK_ROPE), lambda b: (b, 0)),
                  pl.BlockSpec((H_B, QK_NOPE, KV_LORA), lambda b: (0, 0, 0)),
                  pl.BlockSpec((H_B, KV_LORA, COLS_VB), lambda b: (0, 0, 0))],
        out_specs=pl.BlockSpec((seq, COLS_VB), lambda b: (b, 0)),
        out_shape=jax.ShapeDtypeStruct((batch * seq, COLS_VB), BF16),
        scratch_shapes=[pltpu.VMEM((rows, KV_LORA), BF16), pltpu.VMEM((rows, QK_ROPE), BF16),
                        pltpu.VMEM((2, rows // 2, LANES), F32), pltpu.VMEM((2, rows // 2, LANES), F32),
                        pltpu.VMEM((2, rows // 2, KV_LORA), F32)],
        compiler_params=_cparams(("parallel",)),
        name="mla_attn_sample",
    )(qn, qr, ccache, rcache, ckvb, krb, wuk_t, wz)


def _merge_kernel(oa_ref, ob_ref, gate_ref, x_ref, wa_ref, wb_ref, wo_ref, gf_ref, wr_ref, br_ref, tri_ref,
                  x1_ref, h_ref, route_ref, count_ref, run_sc, *, tiled_t):
    oa = _load_tiles_t(oa_ref) if tiled_t else oa_ref[...]
    ob = _load_tiles_t(ob_ref) if tiled_t else ob_ref[...]
    tm = x_ref.shape[0]
    subs = [slice(r * MERGE_SUB, (r + 1) * MERGE_SUB) for r in range(tm // MERGE_SUB)]
    yas = [jnp.dot(oa[s], wa_ref[...], preferred_element_type=F32) for s in subs]
    ybs = [jnp.dot(ob[s], wb_ref[...], preferred_element_type=F32) for s in subs]
    merged = []
    for s, ya, yb in zip(subs, yas, ybs):
        gates = gate_ref[s, :].astype(F32)
        merged.append((gates[:, :D_MODEL] * ya + gates[:, D_MODEL:] * yb).astype(BF16))
    x1s = [x_ref[s, :] + jnp.dot(mg, wo_ref[...], preferred_element_type=F32) for s, mg in zip(subs, merged)]
    logit_parts = []
    for s, x1 in zip(subs, x1s):
        x1_ref[s, :] = x1
        h = _rms(x1, gf_ref[...])
        h_hi = h.astype(BF16)
        h_ref[s, :] = h_hi
        h_lo = (h - h_hi.astype(F32)).astype(BF16)
        logit_parts.append(jnp.dot(h_hi, wr_ref[0], preferred_element_type=F32)
                           + jnp.dot(h_lo, wr_ref[0], preferred_element_type=F32)
                           + jnp.dot(h_hi, wr_ref[1], preferred_element_type=F32))

    logits = jnp.concatenate(logit_parts, axis=0) + br_ref[...]
    lane = lax.broadcasted_iota(jnp.int32, logits.shape, 1)
    big = jnp.int32(ROUTE_COLS)

    def top1(mask):
        v = jnp.max(jnp.where(mask, logits, NEG_INF), axis=-1, keepdims=True)
        idx = jnp.min(jnp.where(mask & (logits == v), lane, big), axis=-1, keepdims=True)
        return v, idx

    gmask = lane < N_GROUPS
    gmax, gidx = top1(gmask)
    g_w = 1.0 / jnp.sum(jnp.where(gmask, jnp.exp(logits - gmax), 0.0), axis=-1, keepdims=True)
    lo = N_GROUPS + gidx * EXPERTS_PER_GROUP
    emask = (lane >= lo) & (lane < lo + EXPERTS_PER_GROUP)
    v1, i1 = top1(emask)
    v2, i2 = top1(emask & (lane != i1))
    e2 = jnp.exp(v2 - v1)
    w1 = g_w / (1.0 + e2)
    w2 = g_w * e2 / (1.0 + e2)
    @pl.when(pl.program_id(0) == 0)
    def _():
        run_sc[...] = jnp.zeros(run_sc.shape, F32)

    e1 = i1 - N_GROUPS
    e2i = i2 - N_GROUPS
    picks = jnp.where((lane == e1) | (lane == e2i), 1.0, 0.0)
    before = jnp.dot(tri_ref[...], picks.astype(BF16), preferred_element_type=F32) + run_sc[...]
    rank1 = jnp.sum(jnp.where(lane == e1, before, 0.0), axis=-1, keepdims=True)
    rank2 = jnp.sum(jnp.where(lane == e2i, before, 0.0), axis=-1, keepdims=True)
    run_sc[...] = run_sc[...] + jnp.sum(picks, axis=0, keepdims=True)
    count_ref[...] = run_sc[...]

    vals = [e1.astype(F32), e2i.astype(F32), w1, w2, rank1, rank2]
    route = jnp.zeros(logits.shape, F32)
    for n, v in enumerate(vals):
        route = jnp.where(lane == n, v, route)
    route_ref[...] = route


def _merge(oa, ob, gates, x, w, *, tm, tiled_t):
    t = x.shape[0]
    tok = lambda i: (i, 0)
    full = lambda i: (0, 0)
    if tiled_t:
        o_specs = [pl.BlockSpec((tm // TQ, COLS_A, TQ), lambda i: (i, 0, 0)),
                   pl.BlockSpec((tm // TQ, COLS_VB, TQ), lambda i: (i, 0, 0))]
    else:
        o_specs = [pl.BlockSpec((tm, COLS_A), tok), pl.BlockSpec((tm, COLS_VB), tok)]
    return pl.pallas_call(
        functools.partial(_merge_kernel, tiled_t=tiled_t),
        grid=(t // tm,),
        in_specs=o_specs + [
            pl.BlockSpec((tm, 2 * D_MODEL), tok), pl.BlockSpec((tm, D_MODEL), tok),
            pl.BlockSpec((COLS_A, D_MODEL), full), pl.BlockSpec((COLS_VB, D_MODEL), full),
            pl.BlockSpec((D_MODEL, D_MODEL), full), pl.BlockSpec((1, D_MODEL), full),
            pl.BlockSpec((2, D_MODEL, ROUTE_COLS), lambda i: (0, 0, 0)), pl.BlockSpec((1, ROUTE_COLS), full),
            pl.BlockSpec((tm, tm), full)],
        out_specs=[pl.BlockSpec((tm, D_MODEL), tok), pl.BlockSpec((tm, D_MODEL), tok),
                   pl.BlockSpec((tm, ROUTE_COLS), tok), pl.BlockSpec((1, ROUTE_COLS), full)],
        out_shape=[jax.ShapeDtypeStruct((t, D_MODEL), F32), jax.ShapeDtypeStruct((t, D_MODEL), BF16),
                   jax.ShapeDtypeStruct((t, ROUTE_COLS), F32), jax.ShapeDtypeStruct((1, ROUTE_COLS), F32)],
        scratch_shapes=[pltpu.VMEM((1, ROUTE_COLS), F32)],
        compiler_params=_cparams(("arbitrary",)),
        name="merge_router",
    )(oa, ob, gates, x, w['wa'], w['wb'], w['wo'], w['g_ffn'], w['w_route'], w['b_route'],
      jnp.tril(jnp.ones((tm, tm), BF16), -1))


def _expert_kernel(blk_e_ref, n_used_ref, x_ref, wgu_ref, wdn_ref, y_ref):
    i = pl.program_id(0)

    @pl.when(i < n_used_ref[0])
    def _():
        gu = jnp.dot(x_ref[...], wgu_ref[0].astype(BF16), preferred_element_type=F32)
        gate, up = gu[:, :D_EXPERT], gu[:, D_EXPERT:]
        a = (gate * jax.nn.sigmoid(gate) * up).astype(BF16)
        y_ref[...] = jnp.dot(a, wdn_ref[0].astype(BF16), preferred_element_type=F32)

    @pl.when(i >= n_used_ref[0])
    def _():
        y_ref[...] = jnp.zeros(y_ref.shape, y_ref.dtype)


def _experts(blk_e, n_used, xs, wgu, wdn, *, layer, bm):
    n_slots = xs.shape[0]
    grid_spec = pltpu.PrefetchScalarGridSpec(
        num_scalar_prefetch=2,
        grid=(n_slots // bm,),
        in_specs=[pl.BlockSpec((bm, D_MODEL), lambda i, be, nu: (i, 0)),
                  pl.BlockSpec((1, D_MODEL, 2 * D_EXPERT), lambda i, be, nu: (layer * N_EXPERTS + be[i], 0, 0)),
                  pl.BlockSpec((1, D_EXPERT, D_MODEL), lambda i, be, nu: (layer * N_EXPERTS + be[i], 0, 0))],
        out_specs=pl.BlockSpec((bm, D_MODEL), lambda i, be, nu: (i, 0)),
    )
    return pl.pallas_call(
        _expert_kernel,
        grid_spec=grid_spec,
        out_shape=jax.ShapeDtypeStruct((n_slots, D_MODEL), F32),
        compiler_params=_cparams(("arbitrary",)),
        name="experts",
    )(blk_e, n_used, xs, wgu, wdn)


def _combine_kernel(x_ref, y0_ref, y1_ref, route_ref, g_ref, o_ref, *, final_norm):
    r = route_ref[...]
    y = x_ref[...] + r[:, 2:3] * y0_ref[...] + r[:, 3:4] * y1_ref[...]
    if final_norm:
        y = _rms(y, g_ref[...])
    o_ref[...] = y


def _combine(x1, y0, y1, route, g_final, *, tm, final_norm):
    t = x1.shape[0]
    tok = lambda i: (i, 0)
    return pl.pallas_call(
        functools.partial(_combine_kernel, final_norm=final_norm),
        grid=(t // tm,),
        in_specs=[pl.BlockSpec((tm, D_MODEL), tok), pl.BlockSpec((tm, D_MODEL), tok),
                  pl.BlockSpec((tm, D_MODEL), tok), pl.BlockSpec((tm, ROUTE_COLS), tok),
                  pl.BlockSpec((1, D_MODEL), lambda i: (0, 0))],
        out_specs=pl.BlockSpec((tm, D_MODEL), tok),
        out_shape=jax.ShapeDtypeStruct((t, D_MODEL), F32),
        compiler_params=_cparams(("parallel",)),
        name="combine",
    )(x1, y0, y1, route, g_final)


def _rope_tables(pos):
    lane = jnp.arange(LANES, dtype=jnp.int32)

    def tab(chunk):
        half = chunk // 2
        inv = ROPE_THETA ** (-(lane % half).astype(F32) / half)
        ang = pos.astype(F32)[:, None] * inv[None, :]
        first = (lane % chunk) < half
        return jnp.cos(ang), jnp.where(first[None, :], -jnp.sin(ang), jnp.sin(ang))

    c64, s64 = tab(DH_A)
    c32, s32 = tab(QK_ROPE)
    rope_lane = ((lane >= QK_NOPE) & (lane < QK_NOPE + QK_ROPE))[None, :]
    return dict(c64=c64, s64=s64, c32=c32, s32=s32,
                cq=jnp.where(rope_lane, c32, 1.0), sq=jnp.where(rope_lane, s32, 0.0))


def _moe_plan(route, counts, bm):
    n_tok = route.shape[0]
    n_asg = n_tok * TOP_K
    flat_e = route[:, :TOP_K].astype(jnp.int32).reshape(n_asg)
    rank = route[:, 4:4 + TOP_K].astype(jnp.int32).reshape(n_asg)
    counts = counts[0, :N_EXPERTS].astype(jnp.int32)
    padded = (counts + bm - 1) // bm * bm
    pad_end = jnp.cumsum(padded)
    pad_start = pad_end - padded
    dest = (pad_start[flat_e] + rank).astype(jnp.int32)
    n_slots = n_asg + N_EXPERTS * bm
    n_blk = n_slots // bm
    slot_tok = jnp.zeros((n_slots,), jnp.int32).at[dest].set(jnp.arange(n_asg, dtype=jnp.int32) // TOP_K)
    blk_start = jnp.arange(n_blk, dtype=jnp.int32) * bm
    blk_e = jnp.minimum(jnp.sum((pad_end[None, :] <= blk_start[:, None]).astype(jnp.int32), axis=1),
                        N_EXPERTS - 1)
    n_used = (pad_end[-1:] // bm).astype(jnp.int32)
    return slot_tok, blk_e, n_used, dest.reshape(n_tok, TOP_K)


def _head_groups(parts):
    rows = parts[0].shape[0]
    used = sum(p.shape[-1] for p in parts)
    pad = jnp.zeros((rows, H_B, LANES - used), parts[0].dtype)
    return jnp.concatenate(list(parts) + [pad], axis=-1).reshape(rows, H_B * LANES)


def _layer_weights(l, g_attn, w_in, g_subln, w_a, g_q_lat, w_uq, g_kv_lat, w_uk, w_uv, w_b, w_o, g_ffn,
                   w_group, b_group, w_router, b_router):
    w = w_in[l]
    o_kr = 3 * COLS_A + Q_LORA + KV_LORA
    w_pack = jnp.concatenate(
        [w[:, :o_kr], jnp.pad(w[:, o_kr:o_kr + QK_ROPE], ((0, 0), (0, KR_PAD - QK_ROPE))), w[:, o_kr + QK_ROPE:]],
        axis=1).astype(BF16)
    wq = w_uq[l].reshape(Q_LORA, H_B, QK_NOPE + QK_ROPE)
    w_uq_p = jnp.concatenate([wq[:, :, :QK_NOPE].reshape(Q_LORA, COLS_QN),
                              wq[:, :, QK_NOPE:].reshape(Q_LORA, COLS_QR)], axis=1).astype(BF16)
    w_uq_g = _head_groups([wq]).astype(BF16)
    w_kn = _head_groups([w_uk[l]]).astype(BF16)
    place = jnp.pad(jnp.eye(QK_ROPE, dtype=F32), ((0, KR_PAD - QK_ROPE), (0, 0)))
    w_krp = _head_groups([jnp.zeros((KR_PAD, H_B, QK_NOPE), F32),
                          jnp.broadcast_to(place[:, None, :], (KR_PAD, H_B, QK_ROPE))]).astype(BF16)
    w_uv_all = w_uv[l].reshape(KV_LORA, COLS_VB).astype(BF16)
    wuk_t = jnp.transpose(w_uk[l], (1, 2, 0)).astype(BF16)
    wuv = jnp.transpose(w_uv[l], (1, 0, 2))
    eye = jnp.eye(H_B, dtype=F32)
    wz = (wuv[:, :, None, :] * eye[:, None, :, None]).reshape(H_B, KV_LORA, COLS_VB).astype(BF16)
    w_route = jnp.concatenate(
        [w_group[l], jnp.transpose(w_router[l], (1, 0, 2)).reshape(D_MODEL, N_EXPERTS),
         jnp.zeros((D_MODEL, ROUTE_COLS - N_GROUPS - N_EXPERTS), F32)], axis=1)
    w_route_hi = w_route.astype(BF16)
    w_route = jnp.stack([w_route_hi, (w_route - w_route_hi.astype(F32)).astype(BF16)])
    b_route = jnp.concatenate([b_group[l], b_router[l].reshape(N_EXPERTS),
                               jnp.zeros((ROUTE_COLS - N_GROUPS - N_EXPERTS,), F32)])[None, :]
    return dict(g_attn=g_attn[l][None, :], w_pack=w_pack, g_q=g_q_lat[l][None, :], g_kv=g_kv_lat[l][None, :],
                w_uq_p=w_uq_p, w_uq_g=w_uq_g, w_kn=w_kn, w_krp=w_krp, w_uv_all=w_uv_all, wuk_t=wuk_t, wz=wz,
                g_subln=g_subln[l][None, :], g_subln_col=g_subln[l][:, None],
                wa=w_a[l].astype(BF16), wb=w_b[l].astype(BF16), wo=w_o[l].astype(BF16),
                g_ffn=g_ffn[l][None, :], w_route=w_route, b_route=b_route)


def _trunk(x, pos_tile, n_pos_tiles, tm, caches, lws, lams, wgu, wdn, g_final, *, batch, seq, bm):
    depth = len(lws)
    t = batch * seq
    x = x.reshape(t, D_MODEL)
    tabs = _rope_tables(pos_tile)
    prompt = caches is None
    ks, vs, cs, rs = [], [], [], []
    new_cache = None
    for l in range(depth):
        w = lws[l]
        out_scale = 1.0 - (0.8 - 0.6 * math.exp(-0.3 * l))
        outs = _inproj(x, w, tabs, tm=tm, n_pos_tiles=n_pos_tiles, prompt=prompt, layer=l, depth=depth,
                       batch=batch, prev=new_cache)
        k, v, ckv, kr = outs[:4]
        if prompt:
            new_cache = (k, v, ckv, kr)
            qat, kb, vat, qbt, kq, vbt, gates = outs[4:]
            oa = _attn_t(qat, kb, vat, batch=batch, seq=seq, n_heads=H_A, dv=DV_A, n_maps=2, lam=lams[l],
                         g=w['g_subln_col'], out_scale=out_scale, name="diff_attn_prompt")
            ob = _attn_t(qbt, kq, vbt, batch=batch, seq=seq, n_heads=H_B, dv=V_B, n_maps=1,
                         name="mla_attn_prompt")
        else:
            qa, kb, vb, ckvb, krb, qn, qr, gates = outs[4:]
            kc, vc, cc, rc, past = caches
            oa = _diff_sample(lams[l], qa, kc, vc, kb, vb, w['g_subln'], layer=l, batch=batch, seq=seq,
                              past=past, out_scale=out_scale)
            ob = _mla_sample(qn, qr, cc, rc, ckvb, krb, w['wuk_t'], w['wz'], layer=l, batch=batch, seq=seq,
                             past=past)
        x1, h, route, counts = _merge(oa, ob, gates, x, w, tm=tm, tiled_t=prompt)
        slot_tok, blk_e, n_used, dest = _moe_plan(route, counts, bm)
        xs = h.at[slot_tok].get(mode="promise_in_bounds")
        y_slots = _experts(blk_e, n_used, xs, wgu, wdn, layer=l, bm=bm)
        y0 = y_slots.at[dest[:, 0]].get(mode="promise_in_bounds")
        y1 = y_slots.at[dest[:, 1]].get(mode="promise_in_bounds")
        x = _combine(x1, y0, y1, route, g_final, tm=tm, final_norm=(l == depth - 1))
        ks.append(k)
        vs.append(v)
        cs.append(ckv)
        rs.append(kr)
    y = x.reshape(batch, seq, D_MODEL)
    if prompt:
        k, v, ckv, kr = new_cache
        return (y,
                jnp.transpose(k.reshape(depth, batch, H_A, 2, DH_A, seq), (0, 1, 5, 2, 3, 4)),
                v.reshape(depth, batch, seq, H_A, DV_A),
                ckv.reshape(depth, batch, seq, KV_LORA),
                jnp.transpose(kr.reshape(depth, batch, QK_ROPE, seq), (0, 1, 3, 2)))
    return (y,
            jnp.stack(ks).reshape(depth, batch, seq, H_A, 2, DH_A),
            jnp.stack(vs).reshape(depth, batch, seq, H_A, DV_A),
            jnp.stack(cs).reshape(depth, batch, seq, KV_LORA),
            jnp.stack(rs).reshape(depth, batch, seq, QK_ROPE))


def kernel(x_prompt, x_sample, cache_diff_k, cache_diff_v, cache_mla_ckv, cache_mla_krope, g_attn, w_in,
           lambda_q1, lambda_k1, lambda_q2, lambda_k2, g_subln, w_a, g_q_lat, w_uq, g_kv_lat, w_uk, w_uv, w_b,
           w_o, g_ffn, w_group, b_group, w_router, b_router, w_gate_up, w_down, g_final):
    depth = w_in.shape[0]
    batch, seq, _ = x_prompt.shape
    dec_batch, dec_seq, _ = x_sample.shape
    past = cache_diff_k.shape[2]
    assert dec_seq == CHUNK and past % CHUNK == 0, "sample frames must form exactly one new chunk"
    assert seq % TQ == 0

    lws = [_layer_weights(l, g_attn, w_in, g_subln, w_a, g_q_lat, w_uq, g_kv_lat, w_uk, w_uv, w_b, w_o, g_ffn,
                          w_group, b_group, w_router, b_router) for l in range(depth)]
    lams = []
    for l in range(depth):
        lam_init = 0.8 - 0.6 * math.exp(-0.3 * l)
        lam = (jnp.exp(jnp.sum(lambda_q1[l] * lambda_k1[l])) - jnp.exp(jnp.sum(lambda_q2[l] * lambda_k2[l]))
               + lam_init)
        lams.append(lam.reshape(1).astype(F32))
    wgu = w_gate_up.reshape(depth * N_EXPERTS, D_MODEL, 2 * D_EXPERT)
    wdn = w_down.reshape(depth * N_EXPERTS, D_EXPERT, D_MODEL)
    g_fin = g_final[None, :]

    tm_p = min(512, seq)
    outs_p = _trunk(x_prompt, jnp.arange(seq, dtype=jnp.int32), seq // tm_p, tm_p, None, lws, lams, wgu, wdn,
                    g_fin, batch=batch, seq=seq, bm=min(512, batch * seq))

    tm_s = min(512, dec_batch * dec_seq)
    pos_tile = jnp.tile(past + jnp.arange(dec_seq, dtype=jnp.int32), tm_s // dec_seq)
    caches = (jnp.transpose(cache_diff_k, (0, 1, 3, 4, 5, 2)).reshape(depth * dec_batch * COLS_A, past),
              cache_diff_v.reshape(depth * dec_batch, past * H_A, DV_A),
              cache_mla_ckv.reshape(depth * dec_batch, past, KV_LORA),
              jnp.transpose(cache_mla_krope, (0, 1, 3, 2)).reshape(depth * dec_batch * QK_ROPE, past), past)
    outs_s = _trunk(x_sample, pos_tile, 1, tm_s, caches, lws, lams, wgu, wdn, g_fin,
                    batch=dec_batch, seq=dec_seq, bm=min(128, dec_batch * dec_seq))

    return (outs_p[0], outs_s[0]) + outs_p[1:] + outs_s[1:]
```

```python
import functools
import math

import jax
import jax.numpy as jnp
from jax import lax
from jax.experimental import pallas as pl
from jax.experimental.pallas import tpu as pltpu

D_MODEL = 1024
CHUNK = 64
ROPE_THETA = 10000.0
EPS = 1e-6
H_A = 4
DH_A = 64
DV_A = 2 * DH_A
H_B = 8
Q_LORA = 384
KV_LORA = 256
QK_NOPE = 64
QK_ROPE = 32
V_B = 64
N_GROUPS = 4
EXPERTS_PER_GROUP = 8
N_EXPERTS = N_GROUPS * EXPERTS_PER_GROUP
TOP_K = 2
D_EXPERT = 512

COLS_A = H_A * 2 * DH_A
COLS_QR = H_B * QK_ROPE
COLS_QN = H_B * QK_NOPE
COLS_VB = H_B * V_B
LANES = 128
COLS_QB = H_B * LANES
KR_PAD = LANES
PACK_COLS = 3 * COLS_A + Q_LORA + KV_LORA + KR_PAD + 2 * D_MODEL
ROUTE_COLS = LANES
TQ = 512
TK = 256
ONES_ROWS = 16
MERGE_SUB = 256
VMEM_LIMIT = 56 * 1024 * 1024
LOG2E = 1.4426950408889634
SCALE_A = DH_A ** -0.5 * LOG2E
SCALE_B = (QK_NOPE + QK_ROPE) ** -0.5 * LOG2E

F32 = jnp.float32
BF16 = jnp.bfloat16
NEG_INF = float("-inf")


def _cparams(sem):
    return pltpu.CompilerParams(dimension_semantics=sem, vmem_limit_bytes=VMEM_LIMIT)


def _rms(x, g):
    return x * lax.rsqrt(jnp.mean(x * x, axis=-1, keepdims=True) + EPS) * g


def _widen(tab, cols):
    reps = cols // LANES
    return tab if reps == 1 else jnp.concatenate([tab] * reps, axis=-1)


def _rope(x, cos, sin_signed, chunk):
    n = x.shape[-1]
    half = chunk // 2
    lane = lax.broadcasted_iota(jnp.int32, x.shape, 1)
    fwd = pltpu.roll(x, n - half, 1)
    bwd = pltpu.roll(x, half, 1)
    swapped = jnp.where((lane & (chunk - 1)) < half, fwd, bwd)
    return x * _widen(cos, n) + swapped * _widen(sin_signed, n)


def _dot_nt(a, b):
    return lax.dot_general(a, b, (((1,), (1,)), ((), ())), preferred_element_type=F32)


def _store_tiles_t(ref, val):
    tile = ref.shape[2]
    for r in range(ref.shape[0]):
        ref[r] = val[r * tile:(r + 1) * tile, :].T.astype(ref.dtype)


def _store_vt_ones(ref, val, n_heads, dv):
    dva = dv + ONES_ROWS
    ones = jnp.ones((ONES_ROWS, TK), ref.dtype)
    for r in range(ref.shape[0]):
        vt = val[r * TK:(r + 1) * TK, :].T.astype(ref.dtype)
        for h in range(n_heads):
            ref[r, h * dva:h * dva + dv, :] = vt[h * dv:(h + 1) * dv, :]
            ref[r, h * dva + dv:(h + 1) * dva, :] = ones


def _load_tiles_t(ref):
    return jnp.concatenate([ref[r].astype(F32).T for r in range(ref.shape[0])], axis=0).astype(BF16)


def _inproj_kernel(*refs, prompt):
    (x_ref, g_ref, w_ref, gq_ref, gkv_ref, wuq_ref, c64_ref, s64_ref, c32_ref, s32_ref) = refs[:10]
    x = x_ref[...]
    hb = _rms(x, g_ref[...]).astype(BF16)

    def proj(lo, hi):
        return jnp.dot(hb, w_ref[:, lo:hi], preferred_element_type=F32)

    c64, s64 = c64_ref[...], s64_ref[...]
    c32, s32 = c32_ref[...], s32_ref[...]
    o = 0
    qa = _rope(proj(o, o + COLS_A), c64, s64, DH_A) * SCALE_A
    o += COLS_A
    ka = _rope(proj(o, o + COLS_A), c64, s64, DH_A)
    o += COLS_A
    va = proj(o, o + COLS_A)
    o += COLS_A
    cq = _rms(proj(o, o + Q_LORA), gq_ref[...]).astype(BF16)
    o += Q_LORA
    qfull = jnp.dot(cq, wuq_ref[...], preferred_element_type=F32) * SCALE_B
    ckv = _rms(proj(o, o + KV_LORA), gkv_ref[...])
    ckvb = ckv.astype(BF16)
    o += KV_LORA
    kr_pad = _rope(proj(o, o + KR_PAD), c32, s32, QK_ROPE)
    o += KR_PAD
    gates = jax.nn.sigmoid(proj(o, o + 2 * D_MODEL)).astype(BF16)

    if prompt:
        cq_ref, sq_ref, wkn_ref, wkr_ref, wuv_ref = refs[10:15]
        (k_ref, v_ref, ckv_ref, kr_ref, qat_ref, kb_ref, vat_ref, qbt_ref, kq_ref, vbt_ref,
         gate_ref) = refs[len(refs) - 11:]
        k_ref[...] = ka.T
        for h in range(H_A):
            v_ref[pl.ds(h, va.shape[0], stride=H_A), :] = va[:, h * DV_A:(h + 1) * DV_A]
        ckv_ref[...] = ckv
        kr_ref[...] = kr_pad.T[:QK_ROPE, :]
        _store_tiles_t(qat_ref, qa)
        kb_ref[...] = ka.astype(BF16)
        _store_vt_ones(vat_ref, va, H_A, DV_A)
        _store_tiles_t(qbt_ref, _rope(qfull, cq_ref[...], sq_ref[...], QK_ROPE))
        kq = (jnp.dot(ckvb, wkn_ref[...], preferred_element_type=F32)
              + jnp.dot(kr_pad.astype(BF16), wkr_ref[...], preferred_element_type=F32))
        kq_ref[...] = kq.astype(BF16)
        _store_vt_ones(vbt_ref, jnp.dot(ckvb, wuv_ref[...], preferred_element_type=F32), H_B, V_B)
    else:
        (k_ref, v_ref, ckv_ref, kr_ref, qa_ref, kb_ref, vb_ref, ckvb_ref, krb_ref, qn_ref, qr_ref,
         gate_ref) = refs[10:]
        qa_ref[...] = qa.astype(BF16)
        kb_ref[...] = ka.astype(BF16)
        vb_ref[...] = va.astype(BF16)
        ckvb_ref[...] = ckvb
        krb_ref[...] = kr_pad[:, :QK_ROPE].astype(BF16)
        qn_ref[...] = qfull[:, :COLS_QN].astype(BF16)
        qr_ref[...] = _rope(qfull[:, COLS_QN:], c32, s32, QK_ROPE).astype(BF16)
        k_ref[...] = ka
        v_ref[...] = va
        ckv_ref[...] = ckv
        kr_ref[...] = kr_pad[:, :QK_ROPE]
    gate_ref[...] = gates


def _inproj(x, w, tabs, *, tm, n_pos_tiles, prompt, layer=0, depth=1, batch=1, prev=None):
    t = x.shape[0]
    nt = t // tm
    tok = lambda i: (i, 0)
    full = lambda i: (0, 0)
    pos = lambda i: (i % n_pos_tiles, 0)
    tile3 = lambda i: (i, 0, 0)
    wuq = w['w_uq_g'] if prompt else w['w_uq_p']
    ins = [x, w['g_attn'], w['w_pack'], w['g_q'], w['g_kv'], wuq, tabs['c64'], tabs['s64'], tabs['c32'], tabs['s32']]
    in_specs = [pl.BlockSpec((tm, D_MODEL), tok), pl.BlockSpec((1, D_MODEL), full),
                pl.BlockSpec((D_MODEL, PACK_COLS), full), pl.BlockSpec((1, Q_LORA), full),
                pl.BlockSpec((1, KV_LORA), full), pl.BlockSpec(wuq.shape, full)] + [pl.BlockSpec((tm, LANES), pos)] * 4
    if prompt:
        seq = t // batch
        per_b = seq // tm
        seq_minor = lambda i: (layer * batch + i // per_b, i % per_b)
        tok_l = lambda i: (layer * nt + i, 0)
        out_specs = [pl.BlockSpec((COLS_A, tm), seq_minor), pl.BlockSpec((tm * H_A, DV_A), tok_l),
                     pl.BlockSpec((tm, KV_LORA), tok_l), pl.BlockSpec((QK_ROPE, tm), seq_minor)]
        out_shape = [jax.ShapeDtypeStruct((depth * batch * COLS_A, seq), F32),
                     jax.ShapeDtypeStruct((depth * t * H_A, DV_A), F32),
                     jax.ShapeDtypeStruct((depth * t, KV_LORA), F32),
                     jax.ShapeDtypeStruct((depth * batch * QK_ROPE, seq), F32)]
    else:
        leaves = [(COLS_A, F32), (COLS_A, F32), (KV_LORA, F32), (QK_ROPE, F32)]
        out_specs = [pl.BlockSpec((tm, c), tok) for c, _ in leaves]
        out_shape = [jax.ShapeDtypeStruct((t, c), d) for c, d in leaves]

    def add2d(c):
        out_specs.append(pl.BlockSpec((tm, c), tok))
        out_shape.append(jax.ShapeDtypeStruct((t, c), BF16))

    def add3d(c, tile):
        out_specs.append(pl.BlockSpec((tm // tile, c, tile), tile3))
        out_shape.append(jax.ShapeDtypeStruct((t // tile, c, tile), BF16))

    if prompt:
        ins += [tabs['cq'], tabs['sq'], w['w_kn'], w['w_krp'], w['w_uv_all']]
        in_specs += [pl.BlockSpec((tm, LANES), pos)] * 2 + [pl.BlockSpec(w[n].shape, full)
                                                             for n in ('w_kn', 'w_krp', 'w_uv_all')]
        add3d(COLS_A, TQ), add2d(COLS_A), add3d(H_A * (DV_A + ONES_ROWS), TK)
        add3d(COLS_QB, TQ), add2d(COLS_QB), add3d(H_B * (V_B + ONES_ROWS), TK)
    else:
        for c in (COLS_A, COLS_A, COLS_A, KV_LORA, QK_ROPE, COLS_QN, COLS_QR):
            add2d(c)
    add2d(2 * D_MODEL)
    aliases = {}
    if prev is not None:
        aliases = {len(ins) + n: n for n in range(len(prev))}
        ins += list(prev)
        in_specs += [pl.BlockSpec(memory_space=pl.ANY)] * len(prev)
    return pl.pallas_call(
        functools.partial(_inproj_kernel, prompt=prompt),
        grid=(nt,),
        in_specs=in_specs,
        out_specs=out_specs,
        out_shape=out_shape,
        input_output_aliases=aliases,
        compiler_params=_cparams(("parallel",)),
        name="inproj_prompt" if prompt else "inproj_sample",
    )(*ins)


def _attn_t_kernel(*refs, n_heads, dv, n_maps, out_scale):
    if n_maps == 2:
        lam_ref, qt_ref, k_ref, vt_ref, g_ref, o_ref, q_sc, m_sc, acc_sc, sta_sc, stb_sc, mxa_sc, mxb_sc = refs
    else:
        qt_ref, k_ref, vt_ref, o_ref, m_sc, acc_sc, sta_sc, stb_sc, mxa_sc, mxb_sc = refs
    i = pl.program_id(1)
    dva = dv + ONES_ROWS
    width = n_maps * TQ
    heads = range(n_heads)
    if n_maps == 2:
        for h in heads:
            qt = qt_ref[0, h * LANES:(h + 1) * LANES, :]
            row = lax.broadcasted_iota(jnp.int32, qt.shape, 0)
            zero = jnp.zeros_like(qt)
            q_sc[h, :, :TQ] = jnp.where(row < DH_A, qt, zero)
            q_sc[h, :, TQ:] = jnp.where(row >= DH_A, qt, zero)
    m_sc[...] = jnp.full(m_sc.shape, NEG_INF, F32)
    acc_sc[...] = jnp.zeros(acc_sc.shape, F32)
    rel = ((lax.broadcasted_iota(jnp.int32, (TK, width), 1) & (TQ - 1)) // CHUNK
           - lax.broadcasted_iota(jnp.int32, (TK, width), 0) // CHUNK)

    def scores(j, st_ref, mx_ref):
        rows = pl.ds(pl.multiple_of(j * TK, TK), TK)
        for h in heads:
            st = jnp.dot(k_ref[rows, h * LANES:(h + 1) * LANES],
                         q_sc[h] if n_maps == 2 else qt_ref[0, h * LANES:(h + 1) * LANES, :],
                         preferred_element_type=F32)
            st_ref[h] = st
            mx_ref[h] = jnp.max(st, axis=0, keepdims=True)

    def update(j, st_ref, mx_ref, masked):
        pts, alphas = [], []
        for h in heads:
            st = st_ref[h]
            if masked:
                st = jnp.where(rel >= j * (TK // CHUNK) - i * (TQ // CHUNK), st, NEG_INF)
                tile_max = jnp.max(st, axis=0, keepdims=True)
            else:
                tile_max = mx_ref[h]
            m = m_sc[h]
            m_new = jnp.maximum(m, tile_max)
            alphas.append(jnp.exp2(m - m_new))
            pts.append(jnp.exp2(st - m_new).astype(BF16))
            m_sc[h] = m_new
        pvs = [jnp.dot(vt_ref[j, h * dva:(h + 1) * dva, :], pts[h], preferred_element_type=F32) for h in heads]
        for h in heads:
            acc_sc[h] = alphas[h] * acc_sc[h] + pvs[h]

    def full_pair(p, carry):
        j = 2 * p
        scores(j + 1, stb_sc, mxb_sc)
        update(j, sta_sc, mxa_sc, False)
        scores(j + 2, sta_sc, mxa_sc)
        update(j + 1, stb_sc, mxb_sc, False)
        return carry

    assert TQ == 2 * TK
    scores(0, sta_sc, mxa_sc)
    lax.fori_loop(0, i, full_pair, 0)
    scores(2 * i + 1, stb_sc, mxb_sc)
    update(2 * i, sta_sc, mxa_sc, True)
    update(2 * i + 1, stb_sc, mxb_sc, True)

    for h in heads:
        acc = acc_sc[h]
        inv = acc[:dv] / acc[dv:dv + 1]
        if n_maps == 2:
            o = inv[:, :TQ] - lam_ref[0] * inv[:, TQ:]
            o = o * lax.rsqrt(jnp.mean(o * o, axis=0, keepdims=True) + EPS) * g_ref[...] * out_scale
        else:
            o = inv
        o_ref[0, h * dv:(h + 1) * dv, :] = o.astype(o_ref.dtype)


def _attn_t(qt, k, vt, *, batch, seq, n_heads, dv, n_maps, lam=None, g=None, out_scale=1.0, name):
    nq = seq // TQ
    dva = dv + ONES_ROWS
    width = n_maps * TQ
    once = pl.Buffered(1)
    ins, in_specs, scratch = [], [], []
    if n_maps == 2:
        ins.append(lam)
        in_specs.append(pl.BlockSpec(memory_space=pltpu.SMEM))
        scratch.append(pltpu.VMEM((n_heads, LANES, width), BF16))
    ins += [qt, k, vt]
    in_specs += [pl.BlockSpec((1, n_heads * LANES, TQ), lambda b, i: (b * nq + i, 0, 0)),
                 pl.BlockSpec((seq, n_heads * LANES), lambda b, i: (b, 0), pipeline_mode=once),
                 pl.BlockSpec((seq // TK, n_heads * dva, TK), lambda b, i: (b, 0, 0), pipeline_mode=once)]
    if n_maps == 2:
        ins.append(g)
        in_specs.append(pl.BlockSpec((dv, 1), lambda b, i: (0, 0)))
    return pl.pallas_call(
        functools.partial(_attn_t_kernel, n_heads=n_heads, dv=dv, n_maps=n_maps, out_scale=out_scale),
        grid=(batch, nq),
        in_specs=in_specs,
        out_specs=pl.BlockSpec((1, n_heads * dv, TQ), lambda b, i: (b * nq + i, 0, 0)),
        out_shape=jax.ShapeDtypeStruct((batch * nq, n_heads * dv, TQ), BF16),
        scratch_shapes=scratch + [pltpu.VMEM((n_heads, 1, width), F32), pltpu.VMEM((n_heads, dva, width), F32),
                                  pltpu.VMEM((n_heads, TK, width), F32), pltpu.VMEM((n_heads, TK, width), F32),
                                  pltpu.VMEM((n_heads, 1, width), F32), pltpu.VMEM((n_heads, 1, width), F32)],
        compiler_params=_cparams(("parallel", "arbitrary")),
        name=name,
    )(*ins)


def _flash_steps(ss, vs, m_sc, l_sc, acc_sc):
    ps, alphas = [], []
    for c, s in enumerate(ss):
        m_prev = m_sc[c]
        m_new = jnp.maximum(m_prev, jnp.max(s, axis=-1, keepdims=True))
        alpha = jnp.exp2(m_prev - m_new)
        cols = s.shape[-1]
        p = jnp.exp2(s - (_widen(m_new, cols) if cols % LANES == 0 else m_new[:, :cols]))
        l_sc[c] = alpha * l_sc[c] + jnp.sum(p, axis=-1, keepdims=True)
        m_sc[c] = m_new
        ps.append(p.astype(BF16))
        alphas.append(alpha)
    pvs = [jnp.dot(p, v, preferred_element_type=F32) for p, v in zip(ps, vs)]
    for c, pv in enumerate(pvs):
        acc_sc[c] = _widen(alphas[c], acc_sc.shape[-1]) * acc_sc[c] + pv


def _init_flash(m_sc, l_sc, acc_sc):
    m_sc[...] = jnp.full(m_sc.shape, NEG_INF, F32)
    l_sc[...] = jnp.zeros(l_sc.shape, F32)
    acc_sc[...] = jnp.zeros(acc_sc.shape, F32)


def _stack_maps(q):
    lane = lax.broadcasted_iota(jnp.int32, q.shape, 1)
    zero = jnp.zeros_like(q)
    return jnp.concatenate([jnp.where(lane < DH_A, q, zero), jnp.where(lane >= DH_A, q, zero)], axis=0)


def _diff_sample_kernel(lam_ref, q_ref, kc_ref, vc_ref, kn_ref, vn_ref, g_ref, o_ref, m_sc, l_sc, acc_sc,
                        *, tq, tk, n_cache_tiles, out_scale):
    heads = range(H_A)
    lanes = [slice(h * DV_A, (h + 1) * DV_A) for h in heads]
    qss = [_stack_maps(q_ref[:, lanes[h]]) for h in heads]
    _init_flash(m_sc, l_sc, acc_sc)

    for j in range(n_cache_tiles):
        ss = [jnp.dot(qss[h], kc_ref[lanes[h], j * tk:(j + 1) * tk].astype(BF16), preferred_element_type=F32)
              for h in heads]
        vs = [vc_ref[0, pl.ds(j * tk * H_A + h, tk, stride=H_A), :].astype(BF16) for h in heads]
        _flash_steps(ss, vs, m_sc, l_sc, acc_sc)
    _flash_steps([_dot_nt(qss[h], kn_ref[:, lanes[h]]) for h in heads], [vn_ref[:, lanes[h]] for h in heads],
                 m_sc, l_sc, acc_sc)
    for h in heads:
        inv = acc_sc[h] / l_sc[h]
        o = inv[:tq] - lam_ref[0] * inv[tq:]
        o_ref[:, lanes[h]] = (_rms(o, g_ref[...]) * out_scale).astype(o_ref.dtype)


def _diff_sample(lam, qa, kcache, vcache, kb, vb, g_subln, *, layer, batch, seq, past, out_scale):
    tk = min(512, past)
    smem = pl.BlockSpec(memory_space=pltpu.SMEM)
    new = pl.BlockSpec((seq, COLS_A), lambda b: (b, 0))
    kcache_spec = pl.BlockSpec((COLS_A, past), lambda b: (layer * batch + b, 0))
    vcache_spec = pl.BlockSpec((1, past * H_A, DV_A), lambda b: (layer * batch + b, 0, 0))
    return pl.pallas_call(
        functools.partial(_diff_sample_kernel, tq=seq, tk=tk, n_cache_tiles=past // tk, out_scale=out_scale),
        grid=(batch,),
        in_specs=[smem, new, kcache_spec, vcache_spec, new, new, pl.BlockSpec((1, DV_A), lambda b: (0, 0))],
        out_specs=new,
        out_shape=jax.ShapeDtypeStruct((batch * seq, COLS_A), BF16),
        scratch_shapes=[pltpu.VMEM((H_A, 2 * seq, LANES), F32), pltpu.VMEM((H_A, 2 * seq, LANES), F32),
                        pltpu.VMEM((H_A, 2 * seq, DV_A), F32)],
        compiler_params=_cparams(("parallel",)),
        name="diff_attn_sample",
    )(lam, qa, kcache, vcache, kb, vb, g_subln)


def _mla_sample_kernel(qn_ref, qr_ref, cc_ref, rc_ref, cn_ref, rn_ref, wuk_ref, wz_ref, o_ref,
                       ql_sc, qr_sc, m_sc, l_sc, acc_sc, *, tq, tk, n_cache_tiles):
    qn = qn_ref[...]
    qr = qr_ref[...].astype(F32)
    for h in range(H_B):
        ql = jnp.dot(qn[:, h * QK_NOPE:(h + 1) * QK_NOPE], wuk_ref[h], preferred_element_type=F32)
        ql_sc[h * tq:(h + 1) * tq, :] = ql.astype(BF16)
        qr_sc[h * tq:(h + 1) * tq, :] = qr[:, h * QK_ROPE:(h + 1) * QK_ROPE].astype(BF16)
    _init_flash(m_sc, l_sc, acc_sc)
    n_chains = m_sc.shape[0]
    half = H_B * tq // n_chains
    parts = [slice(c * half, (c + 1) * half) for c in range(n_chains)]

    for j in range(n_cache_tiles):
        c = cc_ref[0, j * tk:(j + 1) * tk, :].astype(BF16)
        rt = rc_ref[:, j * tk:(j + 1) * tk].astype(BF16)
        ss = [_dot_nt(ql_sc[p, :], c) + jnp.dot(qr_sc[p, :], rt, preferred_element_type=F32) for p in parts]
        _flash_steps(ss, [c] * n_chains, m_sc, l_sc, acc_sc)
    c = cn_ref[...]
    ss = [_dot_nt(ql_sc[p, :], c) + _dot_nt(qr_sc[p, :], rn_ref[...]) for p in parts]
    _flash_steps(ss, [c] * n_chains, m_sc, l_sc, acc_sc)
    ob = jnp.zeros(o_ref.shape, F32)
    heads_per_chain = H_B // n_chains
    for h in range(H_B):
        c, r = divmod(h, heads_per_chain)
        o_lat = (acc_sc[c, r * tq:(r + 1) * tq, :] / _widen(l_sc[c, r * tq:(r + 1) * tq, :], KV_LORA)).astype(BF16)
        ob += jnp.dot(o_lat, wz_ref[h], preferred_element_type=F32)
    o_ref[...] = ob.astype(o_ref.dtype)


def _mla_sample(qn, qr, ccache, rcache, ckvb, krb, wuk_t, wz, *, layer, batch, seq, past):
    tk = min(512, past)
    rows = H_B * seq
    return pl.pallas_call(
        functools.partial(_mla_sample_kernel, tq=seq, tk=tk, n_cache_tiles=past // tk),
        grid=(batch,),
        in_specs=[pl.BlockSpec((seq, COLS_QN), lambda b: (b, 0)),
                  pl.BlockSpec((seq, COLS_QR), lambda b: (b, 0)),
                  pl.BlockSpec((1, past, KV_LORA), lambda b: (layer * batch + b, 0, 0)),
                  pl.BlockSpec((QK_ROPE, past), lambda b: (layer * batch + b, 0)),
                  pl.BlockSpec((seq, KV_LORA), lambda b: (b, 0)),
                  pl.BlockSpec((seq, QK_ROPE), lambda b: (b, 0)),
                  pl.BlockSpec((H_B, QK_NOPE, KV_LORA), lambda b: (0, 0, 0)),
                  pl.BlockSpec((H_B, KV_LORA, COLS_VB), lambda b: (0, 0, 0))],
        out_specs=pl.BlockSpec((seq, COLS_VB), lambda b: (b, 0)),
        out_shape=jax.ShapeDtypeStruct((batch * seq, COLS_VB), BF16),
        scratch_shapes=[pltpu.VMEM((rows, KV_LORA), BF16), pltpu.VMEM((rows, QK_ROPE), BF16),
                        pltpu.VMEM((2, rows // 2, LANES), F32), pltpu.VMEM((2, rows // 2, LANES), F32),
                        pltpu.VMEM((2, rows // 2, KV_LORA), F32)],
        compiler_params=_cparams(("parallel",)),
        name="mla_attn_sample",
    )(qn, qr, ccache, rcache, ckvb, krb, wuk_t, wz)


def _merge_kernel(oa_ref, ob_ref, gate_ref, x_ref, wa_ref, wb_ref, wo_ref, gf_ref, wr_ref, br_ref, tri_ref,
                  x1_ref, h_ref, route_ref, count_ref, run_sc, *, tiled_t):
    oa = _load_tiles_t(oa_ref) if tiled_t else oa_ref[...]
    ob = _load_tiles_t(ob_ref) if tiled_t else ob_ref[...]
    tm = x_ref.shape[0]
    subs = [slice(r * MERGE_SUB, (r + 1) * MERGE_SUB) for r in range(tm // MERGE_SUB)]
    yas = [jnp.dot(oa[s], wa_ref[...], preferred_element_type=F32) for s in subs]
    ybs = [jnp.dot(ob[s], wb_ref[...], preferred_element_type=F32) for s in subs]
    merged = []
    for s, ya, yb in zip(subs, yas, ybs):
        gates = gate_ref[s, :].astype(F32)
        merged.append((gates[:, :D_MODEL] * ya + gates[:, D_MODEL:] * yb).astype(BF16))
    x1s = [x_ref[s, :] + jnp.dot(mg, wo_ref[...], preferred_element_type=F32) for s, mg in zip(subs, merged)]
    logit_parts = []
    for s, x1 in zip(subs, x1s):
        x1_ref[s, :] = x1
        h = _rms(x1, gf_ref[...])
        h_hi = h.astype(BF16)
        h_ref[s, :] = h_hi
        h_lo = (h - h_hi.astype(F32)).astype(BF16)
        logit_parts.append(jnp.dot(h_hi, wr_ref[0], preferred_element_type=F32)
                           + jnp.dot(h_lo, wr_ref[0], preferred_element_type=F32)
                           + jnp.dot(h_hi, wr_ref[1], preferred_element_type=F32))

    logits = jnp.concatenate(logit_parts, axis=0) + br_ref[...]
    lane = lax.broadcasted_iota(jnp.int32, logits.shape, 1)
    big = jnp.int32(ROUTE_COLS)

    def top1(mask):
        v = jnp.max(jnp.where(mask, logits, NEG_INF), axis=-1, keepdims=True)
        idx = jnp.min(jnp.where(mask & (logits == v), lane, big), axis=-1, keepdims=True)
        return v, idx

    gmask = lane < N_GROUPS
    gmax, gidx = top1(gmask)
    g_w = 1.0 / jnp.sum(jnp.where(gmask, jnp.exp(logits - gmax), 0.0), axis=-1, keepdims=True)
    lo = N_GROUPS + gidx * EXPERTS_PER_GROUP
    emask = (lane >= lo) & (lane < lo + EXPERTS_PER_GROUP)
    v1, i1 = top1(emask)
    v2, i2 = top1(emask & (lane != i1))
    e2 = jnp.exp(v2 - v1)
    w1 = g_w / (1.0 + e2)
    w2 = g_w * e2 / (1.0 + e2)
    @pl.when(pl.program_id(0) == 0)
    def _():
        run_sc[...] = jnp.zeros(run_sc.shape, F32)

    e1 = i1 - N_GROUPS
    e2i = i2 - N_GROUPS
    picks = jnp.where((lane == e1) | (lane == e2i), 1.0, 0.0)
    before = jnp.dot(tri_ref[...], picks.astype(BF16), preferred_element_type=F32) + run_sc[...]
    rank1 = jnp.sum(jnp.where(lane == e1, before, 0.0), axis=-1, keepdims=True)
    rank2 = jnp.sum(jnp.where(lane == e2i, before, 0.0), axis=-1, keepdims=True)
    run_sc[...] = run_sc[...] + jnp.sum(picks, axis=0, keepdims=True)
    count_ref[...] = run_sc[...]

    vals = [e1.astype(F32), e2i.astype(F32), w1, w2, rank1, rank2]
    route = jnp.zeros(logits.shape, F32)
    for n, v in enumerate(vals):
        route = jnp.where(lane == n, v, route)
    route_ref[...] = route


def _merge(oa, ob, gates, x, w, *, tm, tiled_t):
    t = x.shape[0]
    tok = lambda i: (i, 0)
    full = lambda i: (0, 0)
    if tiled_t:
        o_specs = [pl.BlockSpec((tm // TQ, COLS_A, TQ), lambda i: (i, 0, 0)),
                   pl.BlockSpec((tm // TQ, COLS_VB, TQ), lambda i: (i, 0, 0))]
    else:
        o_specs = [pl.BlockSpec((tm, COLS_A), tok), pl.BlockSpec((tm, COLS_VB), tok)]
    return pl.pallas_call(
        functools.partial(_merge_kernel, tiled_t=tiled_t),
        grid=(t // tm,),
        in_specs=o_specs + [
            pl.BlockSpec((tm, 2 * D_MODEL), tok), pl.BlockSpec((tm, D_MODEL), tok),
            pl.BlockSpec((COLS_A, D_MODEL), full), pl.BlockSpec((COLS_VB, D_MODEL), full),
            pl.BlockSpec((D_MODEL, D_MODEL), full), pl.BlockSpec((1, D_MODEL), full),
            pl.BlockSpec((2, D_MODEL, ROUTE_COLS), lambda i: (0, 0, 0)), pl.BlockSpec((1, ROUTE_COLS), full),
            pl.BlockSpec((tm, tm), full)],
        out_specs=[pl.BlockSpec((tm, D_MODEL), tok), pl.BlockSpec((tm, D_MODEL), tok),
                   pl.BlockSpec((tm, ROUTE_COLS), tok), pl.BlockSpec((1, ROUTE_COLS), full)],
        out_shape=[jax.ShapeDtypeStruct((t, D_MODEL), F32), jax.ShapeDtypeStruct((t, D_MODEL), BF16),
                   jax.ShapeDtypeStruct((t, ROUTE_COLS), F32), jax.ShapeDtypeStruct((1, ROUTE_COLS), F32)],
        scratch_shapes=[pltpu.VMEM((1, ROUTE_COLS), F32)],
        compiler_params=_cparams(("arbitrary",)),
        name="merge_router",
    )(oa, ob, gates, x, w['wa'], w['wb'], w['wo'], w['g_ffn'], w['w_route'], w['b_route'],
      jnp.tril(jnp.ones((tm, tm), BF16), -1))


def _expert_kernel(blk_e_ref, n_used_ref, x_ref, wgu_ref, wdn_ref, y_ref):
    i = pl.program_id(0)

    @pl.when(i < n_used_ref[0])
    def _():
        gu = jnp.dot(x_ref[...], wgu_ref[0].astype(BF16), preferred_element_type=F32)
        gate, up = gu[:, :D_EXPERT], gu[:, D_EXPERT:]
        a = (gate * jax.nn.sigmoid(gate) * up).astype(BF16)
        y_ref[...] = jnp.dot(a, wdn_ref[0].astype(BF16), preferred_element_type=F32).astype(y_ref.dtype)

    @pl.when(i >= n_used_ref[0])
    def _():
        y_ref[...] = jnp.zeros(y_ref.shape, y_ref.dtype)


def _experts(blk_e, n_used, xs, wgu, wdn, *, layer, bm):
    n_slots = xs.shape[0]
    grid_spec = pltpu.PrefetchScalarGridSpec(
        num_scalar_prefetch=2,
        grid=(n_slots // bm,),
        in_specs=[pl.BlockSpec((bm, D_MODEL), lambda i, be, nu: (i, 0)),
                  pl.BlockSpec((1, D_MODEL, 2 * D_EXPERT), lambda i, be, nu: (layer * N_EXPERTS + be[i], 0, 0)),
                  pl.BlockSpec((1, D_EXPERT, D_MODEL), lambda i, be, nu: (layer * N_EXPERTS + be[i], 0, 0))],
        out_specs=pl.BlockSpec((bm, D_MODEL), lambda i, be, nu: (i, 0)),
    )
    return pl.pallas_call(
        _expert_kernel,
        grid_spec=grid_spec,
        out_shape=jax.ShapeDtypeStruct((n_slots, D_MODEL), BF16),
        compiler_params=_cparams(("arbitrary",)),
        name="experts",
    )(blk_e, n_used, xs, wgu, wdn)


def _combine_kernel(x_ref, y0_ref, y1_ref, route_ref, g_ref, o_ref, *, final_norm):
    r = route_ref[...]
    y = x_ref[...] + r[:, 2:3] * y0_ref[...].astype(F32) + r[:, 3:4] * y1_ref[...].astype(F32)
    if final_norm:
        y = _rms(y, g_ref[...])
    o_ref[...] = y


def _combine(x1, y0, y1, route, g_final, *, tm, final_norm):
    t = x1.shape[0]
    tok = lambda i: (i, 0)
    return pl.pallas_call(
        functools.partial(_combine_kernel, final_norm=final_norm),
        grid=(t // tm,),
        in_specs=[pl.BlockSpec((tm, D_MODEL), tok), pl.BlockSpec((tm, D_MODEL), tok),
                  pl.BlockSpec((tm, D_MODEL), tok), pl.BlockSpec((tm, ROUTE_COLS), tok),
                  pl.BlockSpec((1, D_MODEL), lambda i: (0, 0))],
        out_specs=pl.BlockSpec((tm, D_MODEL), tok),
        out_shape=jax.ShapeDtypeStruct((t, D_MODEL), F32),
        compiler_params=_cparams(("parallel",)),
        name="combine",
    )(x1, y0, y1, route, g_final)


def _rope_tables(pos):
    lane = jnp.arange(LANES, dtype=jnp.int32)

    def tab(chunk):
        half = chunk // 2
        inv = ROPE_THETA ** (-(lane % half).astype(F32) / half)
        ang = pos.astype(F32)[:, None] * inv[None, :]
        first = (lane % chunk) < half
        return jnp.cos(ang), jnp.where(first[None, :], -jnp.sin(ang), jnp.sin(ang))

    c64, s64 = tab(DH_A)
    c32, s32 = tab(QK_ROPE)
    rope_lane = ((lane >= QK_NOPE) & (lane < QK_NOPE + QK_ROPE))[None, :]
    return dict(c64=c64, s64=s64, c32=c32, s32=s32,
                cq=jnp.where(rope_lane, c32, 1.0), sq=jnp.where(rope_lane, s32, 0.0))


def _moe_plan(route, counts, bm):
    n_tok = route.shape[0]
    n_asg = n_tok * TOP_K
    flat_e = route[:, :TOP_K].astype(jnp.int32).reshape(n_asg)
    rank = route[:, 4:4 + TOP_K].astype(jnp.int32).reshape(n_asg)
    counts = counts[0, :N_EXPERTS].astype(jnp.int32)
    padded = (counts + bm - 1) // bm * bm
    pad_end = jnp.cumsum(padded)
    pad_start = pad_end - padded
    dest = (pad_start[flat_e] + rank).astype(jnp.int32)
    n_slots = n_asg + N_EXPERTS * bm
    n_blk = n_slots // bm
    slot_tok = jnp.zeros((n_slots,), jnp.int32).at[dest].set(jnp.arange(n_asg, dtype=jnp.int32) // TOP_K)
    blk_start = jnp.arange(n_blk, dtype=jnp.int32) * bm
    blk_e = jnp.minimum(jnp.sum((pad_end[None, :] <= blk_start[:, None]).astype(jnp.int32), axis=1),
                        N_EXPERTS - 1)
    n_used = (pad_end[-1:] // bm).astype(jnp.int32)
    return slot_tok, blk_e, n_used, dest.reshape(n_tok, TOP_K)


def _head_groups(parts):
    rows = parts[0].shape[0]
    used = sum(p.shape[-1] for p in parts)
    pad = jnp.zeros((rows, H_B, LANES - used), parts[0].dtype)
    return jnp.concatenate(list(parts) + [pad], axis=-1).reshape(rows, H_B * LANES)


def _layer_weights(l, g_attn, w_in, g_subln, w_a, g_q_lat, w_uq, g_kv_lat, w_uk, w_uv, w_b, w_o, g_ffn,
                   w_group, b_group, w_router, b_router):
    w = w_in[l]
    o_kr = 3 * COLS_A + Q_LORA + KV_LORA
    w_pack = jnp.concatenate(
        [w[:, :o_kr], jnp.pad(w[:, o_kr:o_kr + QK_ROPE], ((0, 0), (0, KR_PAD - QK_ROPE))), w[:, o_kr + QK_ROPE:]],
        axis=1).astype(BF16)
    wq = w_uq[l].reshape(Q_LORA, H_B, QK_NOPE + QK_ROPE)
    w_uq_p = jnp.concatenate([wq[:, :, :QK_NOPE].reshape(Q_LORA, COLS_QN),
                              wq[:, :, QK_NOPE:].reshape(Q_LORA, COLS_QR)], axis=1).astype(BF16)
    w_uq_g = _head_groups([wq]).astype(BF16)
    w_kn = _head_groups([w_uk[l]]).astype(BF16)
    place = jnp.pad(jnp.eye(QK_ROPE, dtype=F32), ((0, KR_PAD - QK_ROPE), (0, 0)))
    w_krp = _head_groups([jnp.zeros((KR_PAD, H_B, QK_NOPE), F32),
                          jnp.broadcast_to(place[:, None, :], (KR_PAD, H_B, QK_ROPE))]).astype(BF16)
    w_uv_all = w_uv[l].reshape(KV_LORA, COLS_VB).astype(BF16)
    wuk_t = jnp.transpose(w_uk[l], (1, 2, 0)).astype(BF16)
    wuv = jnp.transpose(w_uv[l], (1, 0, 2))
    eye = jnp.eye(H_B, dtype=F32)
    wz = (wuv[:, :, None, :] * eye[:, None, :, None]).reshape(H_B, KV_LORA, COLS_VB).astype(BF16)
    w_route = jnp.concatenate(
        [w_group[l], jnp.transpose(w_router[l], (1, 0, 2)).reshape(D_MODEL, N_EXPERTS),
         jnp.zeros((D_MODEL, ROUTE_COLS - N_GROUPS - N_EXPERTS), F32)], axis=1)
    w_route_hi = w_route.astype(BF16)
    w_route = jnp.stack([w_route_hi, (w_route - w_route_hi.astype(F32)).astype(BF16)])
    b_route = jnp.concatenate([b_group[l], b_router[l].reshape(N_EXPERTS),
                               jnp.zeros((ROUTE_COLS - N_GROUPS - N_EXPERTS,), F32)])[None, :]
    return dict(g_attn=g_attn[l][None, :], w_pack=w_pack, g_q=g_q_lat[l][None, :], g_kv=g_kv_lat[l][None, :],
                w_uq_p=w_uq_p, w_uq_g=w_uq_g, w_kn=w_kn, w_krp=w_krp, w_uv_all=w_uv_all, wuk_t=wuk_t, wz=wz,
                g_subln=g_subln[l][None, :], g_subln_col=g_subln[l][:, None],
                wa=w_a[l].astype(BF16), wb=w_b[l].astype(BF16), wo=w_o[l].astype(BF16),
                g_ffn=g_ffn[l][None, :], w_route=w_route, b_route=b_route)


def _trunk(x, pos_tile, n_pos_tiles, tm, caches, lws, lams, wgu, wdn, g_final, *, batch, seq, bm):
    depth = len(lws)
    t = batch * seq
    x = x.reshape(t, D_MODEL)
    tabs = _rope_tables(pos_tile)
    prompt = caches is None
    ks, vs, cs, rs = [], [], [], []
    new_cache = None
    for l in range(depth):
        w = lws[l]
        out_scale = 1.0 - (0.8 - 0.6 * math.exp(-0.3 * l))
        outs = _inproj(x, w, tabs, tm=tm, n_pos_tiles=n_pos_tiles, prompt=prompt, layer=l, depth=depth,
                       batch=batch, prev=new_cache)
        k, v, ckv, kr = outs[:4]
        if prompt:
            new_cache = (k, v, ckv, kr)
            qat, kb, vat, qbt, kq, vbt, gates = outs[4:]
            oa = _attn_t(qat, kb, vat, batch=batch, seq=seq, n_heads=H_A, dv=DV_A, n_maps=2, lam=lams[l],
                         g=w['g_subln_col'], out_scale=out_scale, name="diff_attn_prompt")
            ob = _attn_t(qbt, kq, vbt, batch=batch, seq=seq, n_heads=H_B, dv=V_B, n_maps=1,
                         name="mla_attn_prompt")
        else:
            qa, kb, vb, ckvb, krb, qn, qr, gates = outs[4:]
            kc, vc, cc, rc, past = caches
            oa = _diff_sample(lams[l], qa, kc, vc, kb, vb, w['g_subln'], layer=l, batch=batch, seq=seq,
                              past=past, out_scale=out_scale)
            ob = _mla_sample(qn, qr, cc, rc, ckvb, krb, w['wuk_t'], w['wz'], layer=l, batch=batch, seq=seq,
                             past=past)
        x1, h, route, counts = _merge(oa, ob, gates, x, w, tm=tm, tiled_t=prompt)
        slot_tok, blk_e, n_used, dest = _moe_plan(route, counts, bm)
        xs = h.at[slot_tok].get(mode="promise_in_bounds")
        y_slots = _experts(blk_e, n_used, xs, wgu, wdn, layer=l, bm=bm)
        y0 = y_slots.at[dest[:, 0]].get(mode="promise_in_bounds")
        y1 = y_slots.at[dest[:, 1]].get(mode="promise_in_bounds")
        x = _combine(x1, y0, y1, route, g_final, tm=tm, final_norm=(l == depth - 1))
        ks.append(k)
        vs.append(v)
        cs.append(ckv)
        rs.append(kr)
    y = x.reshape(batch, seq, D_MODEL)
    if prompt:
        k, v, ckv, kr = new_cache
        return (y,
                jnp.transpose(k.reshape(depth, batch, H_A, 2, DH_A, seq), (0, 1, 5, 2, 3, 4)),
                v.reshape(depth, batch, seq, H_A, DV_A),
                ckv.reshape(depth, batch, seq, KV_LORA),
                jnp.transpose(kr.reshape(depth, batch, QK_ROPE, seq), (0, 1, 3, 2)))
    return (y,
            jnp.stack(ks).reshape(depth, batch, seq, H_A, 2, DH_A),
            jnp.stack(vs).reshape(depth, batch, seq, H_A, DV_A),
            jnp.stack(cs).reshape(depth, batch, seq, KV_LORA),
            jnp.stack(rs).reshape(depth, batch, seq, QK_ROPE))


def kernel(x_prompt, x_sample, cache_diff_k, cache_diff_v, cache_mla_ckv, cache_mla_krope, g_attn, w_in,
           lambda_q1, lambda_k1, lambda_q2, lambda_k2, g_subln, w_a, g_q_lat, w_uq, g_kv_lat, w_uk, w_uv, w_b,
           w_o, g_ffn, w_group, b_group, w_router, b_router, w_gate_up, w_down, g_final):
    depth = w_in.shape[0]
    batch, seq, _ = x_prompt.shape
    dec_batch, dec_seq, _ = x_sample.shape
    past = cache_diff_k.shape[2]
    assert dec_seq == CHUNK and past % CHUNK == 0, "sample frames must form exactly one new chunk"
    assert seq % TQ == 0

    lws = [_layer_weights(l, g_attn, w_in, g_subln, w_a, g_q_lat, w_uq, g_kv_lat, w_uk, w_uv, w_b, w_o, g_ffn,
                          w_group, b_group, w_router, b_router) for l in range(depth)]
    lams = []
    for l in range(depth):
        lam_init = 0.8 - 0.6 * math.exp(-0.3 * l)
        lam = (jnp.exp(jnp.sum(lambda_q1[l] * lambda_k1[l])) - jnp.exp(jnp.sum(lambda_q2[l] * lambda_k2[l]))
               + lam_init)
        lams.append(lam.reshape(1).astype(F32))
    wgu = w_gate_up.reshape(depth * N_EXPERTS, D_MODEL, 2 * D_EXPERT)
    wdn = w_down.reshape(depth * N_EXPERTS, D_EXPERT, D_MODEL)
    g_fin = g_final[None, :]

    tm_p = min(512, seq)
    outs_p = _trunk(x_prompt, jnp.arange(seq, dtype=jnp.int32), seq // tm_p, tm_p, None, lws, lams, wgu, wdn,
                    g_fin, batch=batch, seq=seq, bm=min(512, batch * seq))

    tm_s = min(512, dec_batch * dec_seq)
    pos_tile = jnp.tile(past + jnp.arange(dec_seq, dtype=jnp.int32), tm_s // dec_seq)
    caches = (jnp.transpose(cache_diff_k, (0, 1, 3, 4, 5, 2)).reshape(depth * dec_batch * COLS_A, past),
              cache_diff_v.reshape(depth * dec_batch, past * H_A, DV_A),
              cache_mla_ckv.reshape(depth * dec_batch, past, KV_LORA),
              jnp.transpose(cache_mla_krope, (0, 1, 3, 2)).reshape(depth * dec_batch * QK_ROPE, past), past)
    outs_s = _trunk(x_sample, pos_tile, 1, tm_s, caches, lws, lams, wgu, wdn, g_fin,
                    batch=dec_batch, seq=dec_seq, bm=min(128, dec_batch * dec_seq))

    return (outs_p[0], outs_s[0]) + outs_p[1:] + outs_s[1:]
```

```python
import functools
import math

import jax
import jax.numpy as jnp
from jax import lax
from jax.experimental import pallas as pl
from jax.experimental.pallas import tpu as pltpu

D_MODEL = 1024
CHUNK = 64
ROPE_THETA = 10000.0
EPS = 1e-6
H_A = 4
DH_A = 64
DV_A = 2 * DH_A
H_B = 8
Q_LORA = 384
KV_LORA = 256
QK_NOPE = 64
QK_ROPE = 32
V_B = 64
N_GROUPS = 4
EXPERTS_PER_GROUP = 8
N_EXPERTS = N_GROUPS * EXPERTS_PER_GROUP
TOP_K = 2
D_EXPERT = 512

COLS_A = H_A * 2 * DH_A
COLS_QR = H_B * QK_ROPE
COLS_QN = H_B * QK_NOPE
COLS_VB = H_B * V_B
LANES = 128
COLS_QB = H_B * LANES
KR_PAD = LANES
PACK_COLS = 3 * COLS_A + Q_LORA + KV_LORA + KR_PAD + 2 * D_MODEL
ROUTE_COLS = LANES
TQ = 512
TK = 256
ONES_ROWS = 16
MERGE_SUB = 256
VMEM_LIMIT = 56 * 1024 * 1024
LOG2E = 1.4426950408889634
SCALE_A = DH_A ** -0.5 * LOG2E
SCALE_B = (QK_NOPE + QK_ROPE) ** -0.5 * LOG2E

F32 = jnp.float32
BF16 = jnp.bfloat16
NEG_INF = float("-inf")


def _cparams(sem):
    return pltpu.CompilerParams(dimension_semantics=sem, vmem_limit_bytes=VMEM_LIMIT)


def _rms(x, g):
    return x * lax.rsqrt(jnp.mean(x * x, axis=-1, keepdims=True) + EPS) * g


def _widen(tab, cols):
    reps = cols // LANES
    return tab if reps == 1 else jnp.concatenate([tab] * reps, axis=-1)


def _rope(x, cos, sin_signed, chunk):
    n = x.shape[-1]
    half = chunk // 2
    lane = lax.broadcasted_iota(jnp.int32, x.shape, 1)
    fwd = pltpu.roll(x, n - half, 1)
    bwd = pltpu.roll(x, half, 1)
    swapped = jnp.where((lane & (chunk - 1)) < half, fwd, bwd)
    return x * _widen(cos, n) + swapped * _widen(sin_signed, n)


def _dot_nt(a, b):
    return lax.dot_general(a, b, (((1,), (1,)), ((), ())), preferred_element_type=F32)


def _store_tiles_t(ref, val):
    tile = ref.shape[2]
    for r in range(ref.shape[0]):
        ref[r] = val[r * tile:(r + 1) * tile, :].T.astype(ref.dtype)


def _store_vt_ones(ref, val, n_heads, dv):
    dva = dv + ONES_ROWS
    ones = jnp.ones((ONES_ROWS, TK), ref.dtype)
    for r in range(ref.shape[0]):
        vt = val[r * TK:(r + 1) * TK, :].T.astype(ref.dtype)
        for h in range(n_heads):
            ref[r, h * dva:h * dva + dv, :] = vt[h * dv:(h + 1) * dv, :]
            ref[r, h * dva + dv:(h + 1) * dva, :] = ones


def _load_tiles_t(ref):
    return jnp.concatenate([ref[r].astype(F32).T for r in range(ref.shape[0])], axis=0).astype(BF16)


def _inproj_kernel(*refs, prompt):
    (x_ref, g_ref, w_ref, gq_ref, gkv_ref, wuq_ref, c64_ref, s64_ref, c32_ref, s32_ref) = refs[:10]
    x = x_ref[...]
    hb = _rms(x, g_ref[...]).astype(BF16)

    def proj(lo, hi):
        return jnp.dot(hb, w_ref[:, lo:hi], preferred_element_type=F32)

    c64, s64 = c64_ref[...], s64_ref[...]
    c32, s32 = c32_ref[...], s32_ref[...]
    o = 0
    qa = _rope(proj(o, o + COLS_A), c64, s64, DH_A) * SCALE_A
    o += COLS_A
    ka = _rope(proj(o, o + COLS_A), c64, s64, DH_A)
    o += COLS_A
    va = proj(o, o + COLS_A)
    o += COLS_A
    cq = _rms(proj(o, o + Q_LORA), gq_ref[...]).astype(BF16)
    o += Q_LORA
    qfull = jnp.dot(cq, wuq_ref[...], preferred_element_type=F32) * SCALE_B
    ckv = _rms(proj(o, o + KV_LORA), gkv_ref[...])
    ckvb = ckv.astype(BF16)
    o += KV_LORA
    kr_pad = _rope(proj(o, o + KR_PAD), c32, s32, QK_ROPE)
    o += KR_PAD
    gates = jax.nn.sigmoid(proj(o, o + 2 * D_MODEL)).astype(BF16)

    if prompt:
        cq_ref, sq_ref, wkn_ref, wkr_ref, wuv_ref = refs[10:15]
        (k_ref, v_ref, ckv_ref, kr_ref, qat_ref, kb_ref, vat_ref, qbt_ref, kq_ref, vbt_ref,
         gate_ref) = refs[len(refs) - 11:]
        k_ref[...] = ka.T
        for h in range(H_A):
            v_ref[pl.ds(h, va.shape[0], stride=H_A), :] = va[:, h * DV_A:(h + 1) * DV_A]
        ckv_ref[...] = ckv
        kr_ref[...] = kr_pad.T[:QK_ROPE, :]
        _store_tiles_t(qat_ref, qa)
        kb_ref[...] = ka.astype(BF16)
        _store_vt_ones(vat_ref, va, H_A, DV_A)
        _store_tiles_t(qbt_ref, _rope(qfull, cq_ref[...], sq_ref[...], QK_ROPE))
        kq = (jnp.dot(ckvb, wkn_ref[...], preferred_element_type=F32)
              + jnp.dot(kr_pad.astype(BF16), wkr_ref[...], preferred_element_type=F32))
        kq_ref[...] = kq.astype(BF16)
        _store_vt_ones(vbt_ref, jnp.dot(ckvb, wuv_ref[...], preferred_element_type=F32), H_B, V_B)
    else:
        (k_ref, v_ref, ckv_ref, kr_ref, qa_ref, kb_ref, vb_ref, ckvb_ref, krb_ref, qn_ref, qr_ref,
         gate_ref) = refs[10:]
        qa_ref[...] = qa.astype(BF16)
        kb_ref[...] = ka.astype(BF16)
        vb_ref[...] = va.astype(BF16)
        ckvb_ref[...] = ckvb
        krb_ref[...] = kr_pad[:, :QK_ROPE].astype(BF16)
        qn_ref[...] = qfull[:, :COLS_QN].astype(BF16)
        qr_ref[...] = _rope(qfull[:, COLS_QN:], c32, s32, QK_ROPE).astype(BF16)
        k_ref[...] = ka
        v_ref[...] = va
        ckv_ref[...] = ckv
        kr_ref[...] = kr_pad[:, :QK_ROPE]
    gate_ref[...] = gates


def _inproj(x, w, tabs, *, tm, n_pos_tiles, prompt, layer=0, depth=1, batch=1, prev=None):
    t = x.shape[0]
    nt = t // tm
    tok = lambda i: (i, 0)
    full = lambda i: (0, 0)
    pos = lambda i: (i % n_pos_tiles, 0)
    tile3 = lambda i: (i, 0, 0)
    wuq = w['w_uq_g'] if prompt else w['w_uq_p']
    ins = [x, w['g_attn'], w['w_pack'], w['g_q'], w['g_kv'], wuq, tabs['c64'], tabs['s64'], tabs['c32'], tabs['s32']]
    in_specs = [pl.BlockSpec((tm, D_MODEL), tok), pl.BlockSpec((1, D_MODEL), full),
                pl.BlockSpec((D_MODEL, PACK_COLS), full), pl.BlockSpec((1, Q_LORA), full),
                pl.BlockSpec((1, KV_LORA), full), pl.BlockSpec(wuq.shape, full)] + [pl.BlockSpec((tm, LANES), pos)] * 4
    if prompt:
        seq = t // batch
        per_b = seq // tm
        seq_minor = lambda i: (layer * batch + i // per_b, i % per_b)
        tok_l = lambda i: (layer * nt + i, 0)
        out_specs = [pl.BlockSpec((COLS_A, tm), seq_minor), pl.BlockSpec((tm * H_A, DV_A), tok_l),
                     pl.BlockSpec((tm, KV_LORA), tok_l), pl.BlockSpec((QK_ROPE, tm), seq_minor)]
        out_shape = [jax.ShapeDtypeStruct((depth * batch * COLS_A, seq), F32),
                     jax.ShapeDtypeStruct((depth * t * H_A, DV_A), F32),
                     jax.ShapeDtypeStruct((depth * t, KV_LORA), F32),
                     jax.ShapeDtypeStruct((depth * batch * QK_ROPE, seq), F32)]
    else:
        leaves = [(COLS_A, F32), (COLS_A, F32), (KV_LORA, F32), (QK_ROPE, F32)]
        out_specs = [pl.BlockSpec((tm, c), tok) for c, _ in leaves]
        out_shape = [jax.ShapeDtypeStruct((t, c), d) for c, d in leaves]

    def add2d(c):
        out_specs.append(pl.BlockSpec((tm, c), tok))
        out_shape.append(jax.ShapeDtypeStruct((t, c), BF16))

    def add3d(c, tile):
        out_specs.append(pl.BlockSpec((tm // tile, c, tile), tile3))
        out_shape.append(jax.ShapeDtypeStruct((t // tile, c, tile), BF16))

    if prompt:
        ins += [tabs['cq'], tabs['sq'], w['w_kn'], w['w_krp'], w['w_uv_all']]
        in_specs += [pl.BlockSpec((tm, LANES), pos)] * 2 + [pl.BlockSpec(w[n].shape, full)
                                                             for n in ('w_kn', 'w_krp', 'w_uv_all')]
        add3d(COLS_A, TQ), add2d(COLS_A), add3d(H_A * (DV_A + ONES_ROWS), TK)
        add3d(COLS_QB, TQ), add2d(COLS_QB), add3d(H_B * (V_B + ONES_ROWS), TK)
    else:
        for c in (COLS_A, COLS_A, COLS_A, KV_LORA, QK_ROPE, COLS_QN, COLS_QR):
            add2d(c)
    add2d(2 * D_MODEL)
    aliases = {}
    if prev is not None:
        aliases = {len(ins) + n: n for n in range(len(prev))}
        ins += list(prev)
        in_specs += [pl.BlockSpec(memory_space=pl.ANY)] * len(prev)
    return pl.pallas_call(
        functools.partial(_inproj_kernel, prompt=prompt),
        grid=(nt,),
        in_specs=in_specs,
        out_specs=out_specs,
        out_shape=out_shape,
        input_output_aliases=aliases,
        compiler_params=_cparams(("parallel",)),
        name="inproj_prompt" if prompt else "inproj_sample",
    )(*ins)


def _attn_t_kernel(*refs, n_heads, dv, n_maps, out_scale):
    if n_maps == 2:
        lam_ref, qt_ref, k_ref, vt_ref, g_ref, o_ref, q_sc, m_sc, acc_sc, sta_sc, stb_sc, mxa_sc, mxb_sc = refs
    else:
        qt_ref, k_ref, vt_ref, o_ref, m_sc, acc_sc, sta_sc, stb_sc, mxa_sc, mxb_sc = refs
    i = pl.program_id(1)
    dva = dv + ONES_ROWS
    width = n_maps * TQ
    heads = range(n_heads)
    if n_maps == 2:
        for h in heads:
            qt = qt_ref[0, h * LANES:(h + 1) * LANES, :]
            row = lax.broadcasted_iota(jnp.int32, qt.shape, 0)
            zero = jnp.zeros_like(qt)
            q_sc[h, :, :TQ] = jnp.where(row < DH_A, qt, zero)
            q_sc[h, :, TQ:] = jnp.where(row >= DH_A, qt, zero)
    m_sc[...] = jnp.full(m_sc.shape, NEG_INF, F32)
    acc_sc[...] = jnp.zeros(acc_sc.shape, F32)
    rel = ((lax.broadcasted_iota(jnp.int32, (TK, width), 1) & (TQ - 1)) // CHUNK
           - lax.broadcasted_iota(jnp.int32, (TK, width), 0) // CHUNK)

    def scores(j, st_ref, mx_ref):
        rows = pl.ds(pl.multiple_of(j * TK, TK), TK)
        for h in heads:
            st = jnp.dot(k_ref[rows, h * LANES:(h + 1) * LANES],
                         q_sc[h] if n_maps == 2 else qt_ref[0, h * LANES:(h + 1) * LANES, :],
                         preferred_element_type=F32)
            st_ref[h] = st
            mx_ref[h] = jnp.max(st, axis=0, keepdims=True)

    def update(j, st_ref, mx_ref, masked):
        pts, alphas = [], []
        for h in heads:
            st = st_ref[h]
            if masked:
                st = jnp.where(rel >= j * (TK // CHUNK) - i * (TQ // CHUNK), st, NEG_INF)
                tile_max = jnp.max(st, axis=0, keepdims=True)
            else:
                tile_max = mx_ref[h]
            m = m_sc[h]
            m_new = jnp.maximum(m, tile_max)
            alphas.append(jnp.exp2(m - m_new))
            pts.append(jnp.exp2(st - m_new).astype(BF16))
            m_sc[h] = m_new
        pvs = [jnp.dot(vt_ref[j, h * dva:(h + 1) * dva, :], pts[h], preferred_element_type=F32) for h in heads]
        for h in heads:
            acc_sc[h] = alphas[h] * acc_sc[h] + pvs[h]

    def full_pair(p, carry):
        j = 2 * p
        scores(j + 1, stb_sc, mxb_sc)
        update(j, sta_sc, mxa_sc, False)
        scores(j + 2, sta_sc, mxa_sc)
        update(j + 1, stb_sc, mxb_sc, False)
        return carry

    assert TQ == 2 * TK
    scores(0, sta_sc, mxa_sc)
    lax.fori_loop(0, i, full_pair, 0)
    scores(2 * i + 1, stb_sc, mxb_sc)
    update(2 * i, sta_sc, mxa_sc, True)
    update(2 * i + 1, stb_sc, mxb_sc, True)

    for h in heads:
        acc = acc_sc[h]
        inv = acc[:dv] / acc[dv:dv + 1]
        if n_maps == 2:
            o = inv[:, :TQ] - lam_ref[0] * inv[:, TQ:]
            o = o * lax.rsqrt(jnp.mean(o * o, axis=0, keepdims=True) + EPS) * g_ref[...] * out_scale
        else:
            o = inv
        o_ref[0, h * dv:(h + 1) * dv, :] = o.astype(o_ref.dtype)


def _attn_t(qt, k, vt, *, batch, seq, n_heads, dv, n_maps, lam=None, g=None, out_scale=1.0, name):
    nq = seq // TQ
    dva = dv + ONES_ROWS
    width = n_maps * TQ
    once = pl.Buffered(1)
    ins, in_specs, scratch = [], [], []
    if n_maps == 2:
        ins.append(lam)
        in_specs.append(pl.BlockSpec(memory_space=pltpu.SMEM))
        scratch.append(pltpu.VMEM((n_heads, LANES, width), BF16))
    ins += [qt, k, vt]
    in_specs += [pl.BlockSpec((1, n_heads * LANES, TQ), lambda b, i: (b * nq + i, 0, 0)),
                 pl.BlockSpec((seq, n_heads * LANES), lambda b, i: (b, 0), pipeline_mode=once),
                 pl.BlockSpec((seq // TK, n_heads * dva, TK), lambda b, i: (b, 0, 0), pipeline_mode=once)]
    if n_maps == 2:
        ins.append(g)
        in_specs.append(pl.BlockSpec((dv, 1), lambda b, i: (0, 0)))
    return pl.pallas_call(
        functools.partial(_attn_t_kernel, n_heads=n_heads, dv=dv, n_maps=n_maps, out_scale=out_scale),
        grid=(batch, nq),
        in_specs=in_specs,
        out_specs=pl.BlockSpec((1, n_heads * dv, TQ), lambda b, i: (b * nq + i, 0, 0)),
        out_shape=jax.ShapeDtypeStruct((batch * nq, n_heads * dv, TQ), BF16),
        scratch_shapes=scratch + [pltpu.VMEM((n_heads, 1, width), F32), pltpu.VMEM((n_heads, dva, width), F32),
                                  pltpu.VMEM((n_heads, TK, width), F32), pltpu.VMEM((n_heads, TK, width), F32),
                                  pltpu.VMEM((n_heads, 1, width), F32), pltpu.VMEM((n_heads, 1, width), F32)],
        compiler_params=_cparams(("parallel", "arbitrary")),
        name=name,
    )(*ins)


def _flash_steps(ss, vs, m_sc, l_sc, acc_sc):
    ps, alphas = [], []
    for c, s in enumerate(ss):
        m_prev = m_sc[c]
        m_new = jnp.maximum(m_prev, jnp.max(s, axis=-1, keepdims=True))
        alpha = jnp.exp2(m_prev - m_new)
        cols = s.shape[-1]
        p = jnp.exp2(s - (_widen(m_new, cols) if cols % LANES == 0 else m_new[:, :cols]))
        l_sc[c] = alpha * l_sc[c] + jnp.sum(p, axis=-1, keepdims=True)
        m_sc[c] = m_new
        ps.append(p.astype(BF16))
        alphas.append(alpha)
    pvs = [jnp.dot(p, v, preferred_element_type=F32) for p, v in zip(ps, vs)]
    for c, pv in enumerate(pvs):
        acc_sc[c] = _widen(alphas[c], acc_sc.shape[-1]) * acc_sc[c] + pv


def _init_flash(m_sc, l_sc, acc_sc):
    m_sc[...] = jnp.full(m_sc.shape, NEG_INF, F32)
    l_sc[...] = jnp.zeros(l_sc.shape, F32)
    acc_sc[...] = jnp.zeros(acc_sc.shape, F32)


def _stack_maps(q):
    lane = lax.broadcasted_iota(jnp.int32, q.shape, 1)
    zero = jnp.zeros_like(q)
    return jnp.concatenate([jnp.where(lane < DH_A, q, zero), jnp.where(lane >= DH_A, q, zero)], axis=0)


def _diff_sample_kernel(lam_ref, q_ref, kc_ref, vc_ref, kn_ref, vn_ref, g_ref, o_ref, m_sc, l_sc, acc_sc,
                        *, tq, tk, n_cache_tiles, out_scale):
    heads = range(H_A)
    lanes = [slice(h * DV_A, (h + 1) * DV_A) for h in heads]
    qss = [_stack_maps(q_ref[:, lanes[h]]) for h in heads]
    _init_flash(m_sc, l_sc, acc_sc)

    for j in range(n_cache_tiles):
        ss = [jnp.dot(qss[h], kc_ref[lanes[h], j * tk:(j + 1) * tk].astype(BF16), preferred_element_type=F32)
              for h in heads]
        vs = [vc_ref[0, pl.ds(j * tk * H_A + h, tk, stride=H_A), :].astype(BF16) for h in heads]
        _flash_steps(ss, vs, m_sc, l_sc, acc_sc)
    _flash_steps([_dot_nt(qss[h], kn_ref[:, lanes[h]]) for h in heads], [vn_ref[:, lanes[h]] for h in heads],
                 m_sc, l_sc, acc_sc)
    for h in heads:
        inv = acc_sc[h] / l_sc[h]
        o = inv[:tq] - lam_ref[0] * inv[tq:]
        o_ref[:, lanes[h]] = (_rms(o, g_ref[...]) * out_scale).astype(o_ref.dtype)


def _diff_sample(lam, qa, kcache, vcache, kb, vb, g_subln, *, layer, batch, seq, past, out_scale):
    tk = min(512, past)
    smem = pl.BlockSpec(memory_space=pltpu.SMEM)
    new = pl.BlockSpec((seq, COLS_A), lambda b: (b, 0))
    kcache_spec = pl.BlockSpec((COLS_A, past), lambda b: (layer * batch + b, 0))
    vcache_spec = pl.BlockSpec((1, past * H_A, DV_A), lambda b: (layer * batch + b, 0, 0))
    return pl.pallas_call(
        functools.partial(_diff_sample_kernel, tq=seq, tk=tk, n_cache_tiles=past // tk, out_scale=out_scale),
        grid=(batch,),
        in_specs=[smem, new, kcache_spec, vcache_spec, new, new, pl.BlockSpec((1, DV_A), lambda b: (0, 0))],
        out_specs=new,
        out_shape=jax.ShapeDtypeStruct((batch * seq, COLS_A), BF16),
        scratch_shapes=[pltpu.VMEM((H_A, 2 * seq, LANES), F32), pltpu.VMEM((H_A, 2 * seq, LANES), F32),
                        pltpu.VMEM((H_A, 2 * seq, DV_A), F32)],
        compiler_params=_cparams(("parallel",)),
        name="diff_attn_sample",
    )(lam, qa, kcache, vcache, kb, vb, g_subln)


def _mla_sample_kernel(qn_ref, qr_ref, cc_ref, rc_ref, cn_ref, rn_ref, wuk_ref, wz_ref, o_ref,
                       ql_sc, qr_sc, m_sc, l_sc, acc_sc, *, tq, tk, n_cache_tiles):
    qn = qn_ref[...]
    qr = qr_ref[...].astype(F32)
    for h in range(H_B):
        ql = jnp.dot(qn[:, h * QK_NOPE:(h + 1) * QK_NOPE], wuk_ref[h], preferred_element_type=F32)
        ql_sc[h * tq:(h + 1) * tq, :] = ql.astype(BF16)
        qr_sc[h * tq:(h + 1) * tq, :] = qr[:, h * QK_ROPE:(h + 1) * QK_ROPE].astype(BF16)
    _init_flash(m_sc, l_sc, acc_sc)
    n_chains = m_sc.shape[0]
    half = H_B * tq // n_chains
    parts = [slice(c * half, (c + 1) * half) for c in range(n_chains)]

    for j in range(n_cache_tiles):
        c = cc_ref[0, j * tk:(j + 1) * tk, :].astype(BF16)
        rt = rc_ref[:, j * tk:(j + 1) * tk].astype(BF16)
        ss = [_dot_nt(ql_sc[p, :], c) + jnp.dot(qr_sc[p, :], rt, preferred_element_type=F32) for p in parts]
        _flash_steps(ss, [c] * n_chains, m_sc, l_sc, acc_sc)
    c = cn_ref[...]
    ss = [_dot_nt(ql_sc[p, :], c) + _dot_nt(qr_sc[p, :], rn_ref[...]) for p in parts]
    _flash_steps(ss, [c] * n_chains, m_sc, l_sc, acc_sc)
    ob = jnp.zeros(o_ref.shape, F32)
    heads_per_chain = H_B // n_chains
    for h in range(H_B):
        c, r = divmod(h, heads_per_chain)
        o_lat = (acc_sc[c, r * tq:(r + 1) * tq, :] / _widen(l_sc[c, r * tq:(r + 1) * tq, :], KV_LORA)).astype(BF16)
        ob += jnp.dot(o_lat, wz_ref[h], preferred_element_type=F32)
    o_ref[...] = ob.astype(o_ref.dtype)


def _mla_sample(qn, qr, ccache, rcache, ckvb, krb, wuk_t, wz, *, layer, batch, seq, past):
    tk = min(512, past)
    rows = H_B * seq
    return pl.pallas_call(
        functools.partial(_mla_sample_kernel, tq=seq, tk=tk, n_cache_tiles=past // tk),
        grid=(batch,),
        in_specs=[pl.BlockSpec((seq, COLS_QN), lambda b: (b, 0)),
                  pl.BlockSpec((seq, COLS_QR), lambda b: (b, 0)),
                  pl.BlockSpec((1, past, KV_LORA), lambda b: (layer * batch + b, 0, 0)),
                  pl.BlockSpec((QK_ROPE, past), lambda b: (layer * batch + b, 0)),
                  pl.BlockSpec((seq, KV_LORA), lambda b: (b, 0)),
                  pl.BlockSpec((seq, QK_ROPE), lambda b: (b, 0)),
                  pl.BlockSpec((H_B, QK_NOPE, KV_LORA), lambda b: (0, 0, 0)),
                  pl.BlockSpec((H_B, KV_LORA, COLS_VB), lambda b: (0, 0, 0))],
        out_specs=pl.BlockSpec((seq, COLS_VB), lambda b: (b, 0)),
        out_shape=jax.ShapeDtypeStruct((batch * seq, COLS_VB), BF16),
        scratch_shapes=[pltpu.VMEM((rows, KV_LORA), BF16), pltpu.VMEM((rows, QK_ROPE), BF16),
                        pltpu.VMEM((2, rows // 2, LANES), F32), pltpu.VMEM((2, rows // 2, LANES), F32),
                        pltpu.VMEM((2, rows // 2, KV_LORA), F32)],
        compiler_params=_cparams(("parallel",)),
        name="mla_attn_sample",
    )(qn, qr, ccache, rcache, ckvb, krb, wuk_t, wz)


def _merge_kernel(oa_ref, ob_ref, gate_ref, x_ref, wa_ref, wb_ref, wo_ref, gf_ref, wr_ref, br_ref, tri_ref,
                  x1_ref, h_ref, route_ref, count_ref, run_sc, *, tiled_t):
    oa = _load_tiles_t(oa_ref) if tiled_t else oa_ref[...]
    ob = _load_tiles_t(ob_ref) if tiled_t else ob_ref[...]
    tm = x_ref.shape[0]
    subs = [slice(r * MERGE_SUB, (r + 1) * MERGE_SUB) for r in range(tm // MERGE_SUB)]
    yas = [jnp.dot(oa[s], wa_ref[...], preferred_element_type=F32) for s in subs]
    ybs = [jnp.dot(ob[s], wb_ref[...], preferred_element_type=F32) for s in subs]
    merged = []
    for s, ya, yb in zip(subs, yas, ybs):
        gates = gate_ref[s, :].astype(F32)
        merged.append((gates[:, :D_MODEL] * ya + gates[:, D_MODEL:] * yb).astype(BF16))
    x1s = [x_ref[s, :] + jnp.dot(mg, wo_ref[...], preferred_element_type=F32) for s, mg in zip(subs, merged)]
    logit_parts = []
    for s, x1 in zip(subs, x1s):
        x1_ref[s, :] = x1
        h = _rms(x1, gf_ref[...])
        h_hi = h.astype(BF16)
        h_ref[s, :] = h_hi
        h_lo = (h - h_hi.astype(F32)).astype(BF16)
        logit_parts.append(jnp.dot(h_hi, wr_ref[0], preferred_element_type=F32)
                           + jnp.dot(h_lo, wr_ref[0], preferred_element_type=F32)
                           + jnp.dot(h_hi, wr_ref[1], preferred_element_type=F32))

    logits = jnp.concatenate(logit_parts, axis=0) + br_ref[...]
    lane = lax.broadcasted_iota(jnp.int32, logits.shape, 1)
    big = jnp.int32(ROUTE_COLS)

    def top1(mask):
        v = jnp.max(jnp.where(mask, logits, NEG_INF), axis=-1, keepdims=True)
        idx = jnp.min(jnp.where(mask & (logits == v), lane, big), axis=-1, keepdims=True)
        return v, idx

    gmask = lane < N_GROUPS
    gmax, gidx = top1(gmask)
    g_w = 1.0 / jnp.sum(jnp.where(gmask, jnp.exp(logits - gmax), 0.0), axis=-1, keepdims=True)
    lo = N_GROUPS + gidx * EXPERTS_PER_GROUP
    emask = (lane >= lo) & (lane < lo + EXPERTS_PER_GROUP)
    v1, i1 = top1(emask)
    v2, i2 = top1(emask & (lane != i1))
    e2 = jnp.exp(v2 - v1)
    w1 = g_w / (1.0 + e2)
    w2 = g_w * e2 / (1.0 + e2)
    @pl.when(pl.program_id(0) == 0)
    def _():
        run_sc[...] = jnp.zeros(run_sc.shape, F32)

    e1 = i1 - N_GROUPS
    e2i = i2 - N_GROUPS
    picks = jnp.where((lane == e1) | (lane == e2i), 1.0, 0.0)
    before = jnp.dot(tri_ref[...], picks.astype(BF16), preferred_element_type=F32) + run_sc[...]
    rank1 = jnp.sum(jnp.where(lane == e1, before, 0.0), axis=-1, keepdims=True)
    rank2 = jnp.sum(jnp.where(lane == e2i, before, 0.0), axis=-1, keepdims=True)
    run_sc[...] = run_sc[...] + jnp.sum(picks, axis=0, keepdims=True)
    count_ref[...] = run_sc[...]

    vals = [e1.astype(F32), e2i.astype(F32), w1, w2, rank1, rank2]
    route = jnp.zeros(logits.shape, F32)
    for n, v in enumerate(vals):
        route = jnp.where(lane == n, v, route)
    route_ref[...] = route


def _merge(oa, ob, gates, x, w, *, tm, tiled_t):
    t = x.shape[0]
    tok = lambda i: (i, 0)
    full = lambda i: (0, 0)
    if tiled_t:
        o_specs = [pl.BlockSpec((tm // TQ, COLS_A, TQ), lambda i: (i, 0, 0)),
                   pl.BlockSpec((tm // TQ, COLS_VB, TQ), lambda i: (i, 0, 0))]
    else:
        o_specs = [pl.BlockSpec((tm, COLS_A), tok), pl.BlockSpec((tm, COLS_VB), tok)]
    return pl.pallas_call(
        functools.partial(_merge_kernel, tiled_t=tiled_t),
        grid=(t // tm,),
        in_specs=o_specs + [
            pl.BlockSpec((tm, 2 * D_MODEL), tok), pl.BlockSpec((tm, D_MODEL), tok),
            pl.BlockSpec((COLS_A, D_MODEL), full), pl.BlockSpec((COLS_VB, D_MODEL), full),
            pl.BlockSpec((D_MODEL, D_MODEL), full), pl.BlockSpec((1, D_MODEL), full),
            pl.BlockSpec((2, D_MODEL, ROUTE_COLS), lambda i: (0, 0, 0)), pl.BlockSpec((1, ROUTE_COLS), full),
            pl.BlockSpec((tm, tm), full)],
        out_specs=[pl.BlockSpec((tm, D_MODEL), tok), pl.BlockSpec((tm, D_MODEL), tok),
                   pl.BlockSpec((tm, ROUTE_COLS), tok), pl.BlockSpec((1, ROUTE_COLS), full)],
        out_shape=[jax.ShapeDtypeStruct((t, D_MODEL), F32), jax.ShapeDtypeStruct((t, D_MODEL), BF16),
                   jax.ShapeDtypeStruct((t, ROUTE_COLS), F32), jax.ShapeDtypeStruct((1, ROUTE_COLS), F32)],
        scratch_shapes=[pltpu.VMEM((1, ROUTE_COLS), F32)],
        compiler_params=_cparams(("arbitrary",)),
        name="merge_router",
    )(oa, ob, gates, x, w['wa'], w['wb'], w['wo'], w['g_ffn'], w['w_route'], w['b_route'],
      jnp.tril(jnp.ones((tm, tm), BF16), -1))


def _expert_kernel(blk_e_ref, n_used_ref, x_ref, wgu_ref, wdn_ref, y_ref):
    i = pl.program_id(0)

    @pl.when(i < n_used_ref[0])
    def _():
        gu = jnp.dot(x_ref[...], wgu_ref[0].astype(BF16), preferred_element_type=F32)
        gate, up = gu[:, :D_EXPERT], gu[:, D_EXPERT:]
        a = (gate * jax.nn.sigmoid(gate) * up).astype(BF16)
        y_ref[...] = jnp.dot(a, wdn_ref[0].astype(BF16), preferred_element_type=F32).astype(y_ref.dtype)

    @pl.when(i >= n_used_ref[0])
    def _():
        y_ref[...] = jnp.zeros(y_ref.shape, y_ref.dtype)


def _experts(blk_e, n_used, xs, wgu, wdn, *, layer, bm):
    n_slots = xs.shape[0]
    grid_spec = pltpu.PrefetchScalarGridSpec(
        num_scalar_prefetch=2,
        grid=(n_slots // bm,),
        in_specs=[pl.BlockSpec((bm, D_MODEL), lambda i, be, nu: (i, 0)),
                  pl.BlockSpec((1, D_MODEL, 2 * D_EXPERT), lambda i, be, nu: (layer * N_EXPERTS + be[i], 0, 0)),
                  pl.BlockSpec((1, D_EXPERT, D_MODEL), lambda i, be, nu: (layer * N_EXPERTS + be[i], 0, 0))],
        out_specs=pl.BlockSpec((bm, D_MODEL), lambda i, be, nu: (i, 0)),
    )
    return pl.pallas_call(
        _expert_kernel,
        grid_spec=grid_spec,
        out_shape=jax.ShapeDtypeStruct((n_slots, D_MODEL), BF16),
        compiler_params=_cparams(("arbitrary",)),
        name="experts",
    )(blk_e, n_used, xs, wgu, wdn)


def _combine_kernel(x_ref, y0_ref, y1_ref, route_ref, g_ref, o_ref, *, final_norm):
    r = route_ref[...]
    y = x_ref[...] + r[:, 2:3] * y0_ref[...].astype(F32) + r[:, 3:4] * y1_ref[...].astype(F32)
    if final_norm:
        y = _rms(y, g_ref[...])
    o_ref[...] = y


def _combine(x1, y0, y1, route, g_final, *, tm, final_norm):
    t = x1.shape[0]
    tok = lambda i: (i, 0)
    return pl.pallas_call(
        functools.partial(_combine_kernel, final_norm=final_norm),
        grid=(t // tm,),
        in_specs=[pl.BlockSpec((tm, D_MODEL), tok), pl.BlockSpec((tm, D_MODEL), tok),
                  pl.BlockSpec((tm, D_MODEL), tok), pl.BlockSpec((tm, ROUTE_COLS), tok),
                  pl.BlockSpec((1, D_MODEL), lambda i: (0, 0))],
        out_specs=pl.BlockSpec((tm, D_MODEL), tok),
        out_shape=jax.ShapeDtypeStruct((t, D_MODEL), F32),
        compiler_params=_cparams(("parallel",)),
        name="combine",
    )(x1, y0, y1, route, g_final)


def _transpose_cast_kernel(x_ref, o_ref):
    o_ref[...] = x_ref[...].T.astype(o_ref.dtype)


def _transpose_cast(xt, block=256):
    c, d = xt.shape
    return pl.pallas_call(
        _transpose_cast_kernel,
        grid=(c // block,),
        in_specs=[pl.BlockSpec((block, d), lambda i: (i, 0))],
        out_specs=pl.BlockSpec((d, block), lambda i: (0, i)),
        out_shape=jax.ShapeDtypeStruct((d, c), BF16),
        compiler_params=_cparams(("parallel",)),
        name="weight_transpose",
    )(xt)


def _rope_tables(pos):
    lane = jnp.arange(LANES, dtype=jnp.int32)

    def tab(chunk):
        half = chunk // 2
        inv = ROPE_THETA ** (-(lane % half).astype(F32) / half)
        ang = pos.astype(F32)[:, None] * inv[None, :]
        first = (lane % chunk) < half
        return jnp.cos(ang), jnp.where(first[None, :], -jnp.sin(ang), jnp.sin(ang))

    c64, s64 = tab(DH_A)
    c32, s32 = tab(QK_ROPE)
    rope_lane = ((lane >= QK_NOPE) & (lane < QK_NOPE + QK_ROPE))[None, :]
    return dict(c64=c64, s64=s64, c32=c32, s32=s32,
                cq=jnp.where(rope_lane, c32, 1.0), sq=jnp.where(rope_lane, s32, 0.0))


def _moe_plan(route, counts, bm):
    n_tok = route.shape[0]
    n_asg = n_tok * TOP_K
    flat_e = route[:, :TOP_K].astype(jnp.int32).reshape(n_asg)
    rank = route[:, 4:4 + TOP_K].astype(jnp.int32).reshape(n_asg)
    counts = counts[0, :N_EXPERTS].astype(jnp.int32)
    padded = (counts + bm - 1) // bm * bm
    pad_end = jnp.cumsum(padded)
    pad_start = pad_end - padded
    dest = (pad_start[flat_e] + rank).astype(jnp.int32)
    n_slots = n_asg + N_EXPERTS * bm
    n_blk = n_slots // bm
    slot_tok = jnp.zeros((n_slots,), jnp.int32).at[dest].set(
        jnp.arange(n_asg, dtype=jnp.int32) // TOP_K, unique_indices=True, mode="promise_in_bounds")
    blk_start = jnp.arange(n_blk, dtype=jnp.int32) * bm
    blk_e = jnp.minimum(jnp.sum((pad_end[None, :] <= blk_start[:, None]).astype(jnp.int32), axis=1),
                        N_EXPERTS - 1)
    n_used = (pad_end[-1:] // bm).astype(jnp.int32)
    return slot_tok, blk_e, n_used, dest.reshape(n_tok, TOP_K)


def _head_groups(parts):
    rows = parts[0].shape[0]
    used = sum(p.shape[-1] for p in parts)
    pad = jnp.zeros((rows, H_B, LANES - used), parts[0].dtype)
    return jnp.concatenate(list(parts) + [pad], axis=-1).reshape(rows, H_B * LANES)


def _layer_weights(l, g_attn, w_in, g_subln, w_a, g_q_lat, w_uq, g_kv_lat, w_uk, w_uv, w_b, w_o, g_ffn,
                   w_group, b_group, w_router, b_router):
    wt = jnp.swapaxes(w_in, 1, 2)[l]
    o_kr = 3 * COLS_A + Q_LORA + KV_LORA
    w_pack = _transpose_cast(jnp.concatenate(
        [wt[:o_kr], jnp.pad(wt[o_kr:o_kr + QK_ROPE], ((0, KR_PAD - QK_ROPE), (0, 0))), wt[o_kr + QK_ROPE:]],
        axis=0))
    wq = w_uq[l].reshape(Q_LORA, H_B, QK_NOPE + QK_ROPE)
    w_uq_p = jnp.concatenate([wq[:, :, :QK_NOPE].reshape(Q_LORA, COLS_QN),
                              wq[:, :, QK_NOPE:].reshape(Q_LORA, COLS_QR)], axis=1).astype(BF16)
    w_uq_g = _head_groups([wq]).astype(BF16)
    w_kn = _head_groups([w_uk[l]]).astype(BF16)
    place = jnp.pad(jnp.eye(QK_ROPE, dtype=F32), ((0, KR_PAD - QK_ROPE), (0, 0)))
    w_krp = _head_groups([jnp.zeros((KR_PAD, H_B, QK_NOPE), F32),
                          jnp.broadcast_to(place[:, None, :], (KR_PAD, H_B, QK_ROPE))]).astype(BF16)
    w_uv_all = w_uv[l].reshape(KV_LORA, COLS_VB).astype(BF16)
    wuk_t = jnp.transpose(w_uk[l], (1, 2, 0)).astype(BF16)
    wuv = jnp.transpose(w_uv[l], (1, 0, 2))
    eye = jnp.eye(H_B, dtype=F32)
    wz = (wuv[:, :, None, :] * eye[:, None, :, None]).reshape(H_B, KV_LORA, COLS_VB).astype(BF16)
    w_route = jnp.concatenate(
        [w_group[l], jnp.transpose(w_router[l], (1, 0, 2)).reshape(D_MODEL, N_EXPERTS),
         jnp.zeros((D_MODEL, ROUTE_COLS - N_GROUPS - N_EXPERTS), F32)], axis=1)
    w_route_hi = w_route.astype(BF16)
    w_route = jnp.stack([w_route_hi, (w_route - w_route_hi.astype(F32)).astype(BF16)])
    b_route = jnp.concatenate([b_group[l], b_router[l].reshape(N_EXPERTS),
                               jnp.zeros((ROUTE_COLS - N_GROUPS - N_EXPERTS,), F32)])[None, :]
    return dict(g_attn=g_attn[l][None, :], w_pack=w_pack, g_q=g_q_lat[l][None, :], g_kv=g_kv_lat[l][None, :],
                w_uq_p=w_uq_p, w_uq_g=w_uq_g, w_kn=w_kn, w_krp=w_krp, w_uv_all=w_uv_all, wuk_t=wuk_t, wz=wz,
                g_subln=g_subln[l][None, :], g_subln_col=g_subln[l][:, None],
                wa=w_a[l].astype(BF16), wb=w_b[l].astype(BF16), wo=w_o[l].astype(BF16),
                g_ffn=g_ffn[l][None, :], w_route=w_route, b_route=b_route)


def _trunk(x, pos_tile, n_pos_tiles, tm, caches, lws, lams, wgu, wdn, g_final, *, batch, seq, bm):
    depth = len(lws)
    t = batch * seq
    x = x.reshape(t, D_MODEL)
    tabs = _rope_tables(pos_tile)
    prompt = caches is None
    ks, vs, cs, rs = [], [], [], []
    new_cache = None
    for l in range(depth):
        w = lws[l]
        out_scale = 1.0 - (0.8 - 0.6 * math.exp(-0.3 * l))
        outs = _inproj(x, w, tabs, tm=tm, n_pos_tiles=n_pos_tiles, prompt=prompt, layer=l, depth=depth,
                       batch=batch, prev=new_cache)
        k, v, ckv, kr = outs[:4]
        if prompt:
            new_cache = (k, v, ckv, kr)
            qat, kb, vat, qbt, kq, vbt, gates = outs[4:]
            oa = _attn_t(qat, kb, vat, batch=batch, seq=seq, n_heads=H_A, dv=DV_A, n_maps=2, lam=lams[l],
                         g=w['g_subln_col'], out_scale=out_scale, name="diff_attn_prompt")
            ob = _attn_t(qbt, kq, vbt, batch=batch, seq=seq, n_heads=H_B, dv=V_B, n_maps=1,
                         name="mla_attn_prompt")
        else:
            qa, kb, vb, ckvb, krb, qn, qr, gates = outs[4:]
            kc, vc, cc, rc, past = caches
            oa = _diff_sample(lams[l], qa, kc, vc, kb, vb, w['g_subln'], layer=l, batch=batch, seq=seq,
                              past=past, out_scale=out_scale)
            ob = _mla_sample(qn, qr, cc, rc, ckvb, krb, w['wuk_t'], w['wz'], layer=l, batch=batch, seq=seq,
                             past=past)
        x1, h, route, counts = _merge(oa, ob, gates, x, w, tm=tm, tiled_t=prompt)
        slot_tok, blk_e, n_used, dest = _moe_plan(route, counts, bm)
        xs = h.at[slot_tok].get(mode="promise_in_bounds")
        y_slots = _experts(blk_e, n_used, xs, wgu, wdn, layer=l, bm=bm)
        y0 = y_slots.at[dest[:, 0]].get(mode="promise_in_bounds")
        y1 = y_slots.at[dest[:, 1]].get(mode="promise_in_bounds")
        x = _combine(x1, y0, y1, route, g_final, tm=tm, final_norm=(l == depth - 1))
        ks.append(k)
        vs.append(v)
        cs.append(ckv)
        rs.append(kr)
    y = x.reshape(batch, seq, D_MODEL)
    if prompt:
        k, v, ckv, kr = new_cache
        return (y,
                jnp.transpose(k.reshape(depth, batch, H_A, 2, DH_A, seq), (0, 1, 5, 2, 3, 4)),
                v.reshape(depth, batch, seq, H_A, DV_A),
                ckv.reshape(depth, batch, seq, KV_LORA),
                jnp.transpose(kr.reshape(depth, batch, QK_ROPE, seq), (0, 1, 3, 2)))
    return (y,
            jnp.stack(ks).reshape(depth, batch, seq, H_A, 2, DH_A),
            jnp.stack(vs).reshape(depth, batch, seq, H_A, DV_A),
            jnp.stack(cs).reshape(depth, batch, seq, KV_LORA),
            jnp.stack(rs).reshape(depth, batch, seq, QK_ROPE))


def kernel(x_prompt, x_sample, cache_diff_k, cache_diff_v, cache_mla_ckv, cache_mla_krope, g_attn, w_in,
           lambda_q1, lambda_k1, lambda_q2, lambda_k2, g_subln, w_a, g_q_lat, w_uq, g_kv_lat, w_uk, w_uv, w_b,
           w_o, g_ffn, w_group, b_group, w_router, b_router, w_gate_up, w_down, g_final):
    depth = w_in.shape[0]
    batch, seq, _ = x_prompt.shape
    dec_batch, dec_seq, _ = x_sample.shape
    past = cache_diff_k.shape[2]
    assert dec_seq == CHUNK and past % CHUNK == 0, "sample frames must form exactly one new chunk"
    assert seq % TQ == 0

    lws = [_layer_weights(l, g_attn, w_in, g_subln, w_a, g_q_lat, w_uq, g_kv_lat, w_uk, w_uv, w_b, w_o, g_ffn,
                          w_group, b_group, w_router, b_router) for l in range(depth)]
    lams = []
    for l in range(depth):
        lam_init = 0.8 - 0.6 * math.exp(-0.3 * l)
        lam = (jnp.exp(jnp.sum(lambda_q1[l] * lambda_k1[l])) - jnp.exp(jnp.sum(lambda_q2[l] * lambda_k2[l]))
               + lam_init)
        lams.append(lam.reshape(1).astype(F32))
    wgu = w_gate_up.reshape(depth * N_EXPERTS, D_MODEL, 2 * D_EXPERT)
    wdn = w_down.reshape(depth * N_EXPERTS, D_EXPERT, D_MODEL)
    g_fin = g_final[None, :]

    tm_p = min(512, seq)
    outs_p = _trunk(x_prompt, jnp.arange(seq, dtype=jnp.int32), seq // tm_p, tm_p, None, lws, lams, wgu, wdn,
                    g_fin, batch=batch, seq=seq, bm=min(512, batch * seq))

    tm_s = min(512, dec_batch * dec_seq)
    pos_tile = jnp.tile(past + jnp.arange(dec_seq, dtype=jnp.int32), tm_s // dec_seq)
    caches = (jnp.transpose(cache_diff_k, (0, 1, 3, 4, 5, 2)).reshape(depth * dec_batch * COLS_A, past),
              cache_diff_v.reshape(depth * dec_batch, past * H_A, DV_A),
              cache_mla_ckv.reshape(depth * dec_batch, past, KV_LORA),
              jnp.transpose(cache_mla_krope, (0, 1, 3, 2)).reshape(depth * dec_batch * QK_ROPE, past), past)
    outs_s = _trunk(x_sample, pos_tile, 1, tm_s, caches, lws, lams, wgu, wdn, g_fin,
                    batch=dec_batch, seq=dec_seq, bm=min(128, dec_batch * dec_seq))

    return (outs_p[0], outs_s[0]) + outs_p[1:] + outs_s[1:]
```

```python
import functools
import math

import jax
import jax.numpy as jnp
from jax import lax
from jax.experimental import pallas as pl
from jax.experimental.pallas import tpu as pltpu

D_MODEL = 1024
CHUNK = 64
ROPE_THETA = 10000.0
EPS = 1e-6
H_A = 4
DH_A = 64
DV_A = 2 * DH_A
H_B = 8
Q_LORA = 384
KV_LORA = 256
QK_NOPE = 64
QK_ROPE = 32
V_B = 64
N_GROUPS = 4
EXPERTS_PER_GROUP = 8
N_EXPERTS = N_GROUPS * EXPERTS_PER_GROUP
TOP_K = 2
D_EXPERT = 512

COLS_A = H_A * 2 * DH_A
COLS_QR = H_B * QK_ROPE
COLS_QN = H_B * QK_NOPE
COLS_VB = H_B * V_B
LANES = 128
COLS_QB = H_B * LANES
KR_PAD = LANES
PACK_COLS = 3 * COLS_A + Q_LORA + KV_LORA + KR_PAD + 2 * D_MODEL
ROUTE_COLS = LANES
TQ = 512
TK = 256
ONES_ROWS = 16
MERGE_SUB = 256
VMEM_LIMIT = 56 * 1024 * 1024
LOG2E = 1.4426950408889634
SCALE_A = DH_A ** -0.5 * LOG2E
SCALE_B = (QK_NOPE + QK_ROPE) ** -0.5 * LOG2E

F32 = jnp.float32
BF16 = jnp.bfloat16
NEG_INF = float("-inf")


def _cparams(sem):
    return pltpu.CompilerParams(dimension_semantics=sem, vmem_limit_bytes=VMEM_LIMIT)


def _rms(x, g):
    return x * lax.rsqrt(jnp.mean(x * x, axis=-1, keepdims=True) + EPS) * g


def _widen(tab, cols):
    reps = cols // LANES
    return tab if reps == 1 else jnp.concatenate([tab] * reps, axis=-1)


def _rope(x, cos, sin_signed, chunk):
    n = x.shape[-1]
    half = chunk // 2
    lane = lax.broadcasted_iota(jnp.int32, x.shape, 1)
    fwd = pltpu.roll(x, n - half, 1)
    bwd = pltpu.roll(x, half, 1)
    swapped = jnp.where((lane & (chunk - 1)) < half, fwd, bwd)
    return x * _widen(cos, n) + swapped * _widen(sin_signed, n)


def _dot_nt(a, b):
    return lax.dot_general(a, b, (((1,), (1,)), ((), ())), preferred_element_type=F32)


def _store_tiles_t(ref, val):
    tile = ref.shape[2]
    for r in range(ref.shape[0]):
        ref[r] = val[r * tile:(r + 1) * tile, :].T.astype(ref.dtype)


def _store_vt_ones(ref, val, n_heads, dv):
    dva = dv + ONES_ROWS
    ones = jnp.ones((ONES_ROWS, TK), ref.dtype)
    for r in range(ref.shape[0]):
        vt = val[r * TK:(r + 1) * TK, :].T.astype(ref.dtype)
        for h in range(n_heads):
            ref[r, h * dva:h * dva + dv, :] = vt[h * dv:(h + 1) * dv, :]
            ref[r, h * dva + dv:(h + 1) * dva, :] = ones


def _load_tiles_t(ref):
    return jnp.concatenate([ref[r].astype(F32).T for r in range(ref.shape[0])], axis=0).astype(BF16)


def _inproj_kernel(*refs, prompt):
    (x_ref, g_ref, w_ref, gq_ref, gkv_ref, wuq_ref, c64_ref, s64_ref, c32_ref, s32_ref) = refs[:10]
    x = x_ref[...]
    hb = _rms(x, g_ref[...]).astype(BF16)

    def proj(lo, hi):
        return jnp.dot(hb, w_ref[:, lo:hi], preferred_element_type=F32)

    c64, s64 = c64_ref[...], s64_ref[...]
    c32, s32 = c32_ref[...], s32_ref[...]
    o = 0
    qa = _rope(proj(o, o + COLS_A), c64, s64, DH_A) * SCALE_A
    o += COLS_A
    ka = _rope(proj(o, o + COLS_A), c64, s64, DH_A)
    o += COLS_A
    va = proj(o, o + COLS_A)
    o += COLS_A
    cq = _rms(proj(o, o + Q_LORA), gq_ref[...]).astype(BF16)
    o += Q_LORA
    qfull = jnp.dot(cq, wuq_ref[...], preferred_element_type=F32) * SCALE_B
    ckv = _rms(proj(o, o + KV_LORA), gkv_ref[...])
    ckvb = ckv.astype(BF16)
    o += KV_LORA
    kr_pad = _rope(proj(o, o + KR_PAD), c32, s32, QK_ROPE)
    o += KR_PAD
    gates = jax.nn.sigmoid(proj(o, o + 2 * D_MODEL)).astype(BF16)

    if prompt:
        cq_ref, sq_ref, wkn_ref, wkr_ref, wuv_ref = refs[10:15]
        (k_ref, v_ref, ckv_ref, kr_ref, qat_ref, kb_ref, vat_ref, qbt_ref, kq_ref, vbt_ref,
         gate_ref) = refs[len(refs) - 11:]
        k_ref[...] = ka.T
        for h in range(H_A):
            v_ref[pl.ds(h, va.shape[0], stride=H_A), :] = va[:, h * DV_A:(h + 1) * DV_A]
        ckv_ref[...] = ckv
        kr_ref[...] = kr_pad.T[:QK_ROPE, :]
        _store_tiles_t(qat_ref, qa)
        kb_ref[...] = ka.astype(BF16)
        _store_vt_ones(vat_ref, va, H_A, DV_A)
        _store_tiles_t(qbt_ref, _rope(qfull, cq_ref[...], sq_ref[...], QK_ROPE))
        kq = (jnp.dot(ckvb, wkn_ref[...], preferred_element_type=F32)
              + jnp.dot(kr_pad.astype(BF16), wkr_ref[...], preferred_element_type=F32))
        kq_ref[...] = kq.astype(BF16)
        _store_vt_ones(vbt_ref, jnp.dot(ckvb, wuv_ref[...], preferred_element_type=F32), H_B, V_B)
    else:
        (k_ref, v_ref, ckv_ref, kr_ref, qa_ref, kb_ref, vb_ref, ckvb_ref, krb_ref, qn_ref, qr_ref,
         gate_ref) = refs[10:]
        qa_ref[...] = qa.astype(BF16)
        kb_ref[...] = ka.astype(BF16)
        vb_ref[...] = va.astype(BF16)
        ckvb_ref[...] = ckvb
        krb_ref[...] = kr_pad[:, :QK_ROPE].astype(BF16)
        qn_ref[...] = qfull[:, :COLS_QN].astype(BF16)
        qr_ref[...] = _rope(qfull[:, COLS_QN:], c32, s32, QK_ROPE).astype(BF16)
        k_ref[...] = ka
        v_ref[...] = va
        ckv_ref[...] = ckv
        kr_ref[...] = kr_pad[:, :QK_ROPE]
    gate_ref[...] = gates


def _inproj(x, w, tabs, *, tm, n_pos_tiles, prompt, layer=0, depth=1, batch=1, prev=None):
    t = x.shape[0]
    nt = t // tm
    tok = lambda i: (i, 0)
    full = lambda i: (0, 0)
    pos = lambda i: (i % n_pos_tiles, 0)
    tile3 = lambda i: (i, 0, 0)
    wuq = w['w_uq_g'] if prompt else w['w_uq_p']
    ins = [x, w['g_attn'], w['w_pack'], w['g_q'], w['g_kv'], wuq, tabs['c64'], tabs['s64'], tabs['c32'], tabs['s32']]
    in_specs = [pl.BlockSpec((tm, D_MODEL), tok), pl.BlockSpec((1, D_MODEL), full),
                pl.BlockSpec((D_MODEL, PACK_COLS), full), pl.BlockSpec((1, Q_LORA), full),
                pl.BlockSpec((1, KV_LORA), full), pl.BlockSpec(wuq.shape, full)] + [pl.BlockSpec((tm, LANES), pos)] * 4
    if prompt:
        seq = t // batch
        per_b = seq // tm
        seq_minor = lambda i: (layer * batch + i // per_b, i % per_b)
        tok_l = lambda i: (layer * nt + i, 0)
        out_specs = [pl.BlockSpec((COLS_A, tm), seq_minor), pl.BlockSpec((tm * H_A, DV_A), tok_l),
                     pl.BlockSpec((tm, KV_LORA), tok_l), pl.BlockSpec((QK_ROPE, tm), seq_minor)]
        out_shape = [jax.ShapeDtypeStruct((depth * batch * COLS_A, seq), F32),
                     jax.ShapeDtypeStruct((depth * t * H_A, DV_A), F32),
                     jax.ShapeDtypeStruct((depth * t, KV_LORA), F32),
                     jax.ShapeDtypeStruct((depth * batch * QK_ROPE, seq), F32)]
    else:
        leaves = [(COLS_A, F32), (COLS_A, F32), (KV_LORA, F32), (QK_ROPE, F32)]
        out_specs = [pl.BlockSpec((tm, c), tok) for c, _ in leaves]
        out_shape = [jax.ShapeDtypeStruct((t, c), d) for c, d in leaves]

    def add2d(c):
        out_specs.append(pl.BlockSpec((tm, c), tok))
        out_shape.append(jax.ShapeDtypeStruct((t, c), BF16))

    def add3d(c, tile):
        out_specs.append(pl.BlockSpec((tm // tile, c, tile), tile3))
        out_shape.append(jax.ShapeDtypeStruct((t // tile, c, tile), BF16))

    if prompt:
        ins += [tabs['cq'], tabs['sq'], w['w_kn'], w['w_krp'], w['w_uv_all']]
        in_specs += [pl.BlockSpec((tm, LANES), pos)] * 2 + [pl.BlockSpec(w[n].shape, full)
                                                             for n in ('w_kn', 'w_krp', 'w_uv_all')]
        add3d(COLS_A, TQ), add2d(COLS_A), add3d(H_A * (DV_A + ONES_ROWS), TK)
        add3d(COLS_QB, TQ), add2d(COLS_QB), add3d(H_B * (V_B + ONES_ROWS), TK)
    else:
        for c in (COLS_A, COLS_A, COLS_A, KV_LORA, QK_ROPE, COLS_QN, COLS_QR):
            add2d(c)
    add2d(2 * D_MODEL)
    aliases = {}
    if prev is not None:
        aliases = {len(ins) + n: n for n in range(len(prev))}
        ins += list(prev)
        in_specs += [pl.BlockSpec(memory_space=pl.ANY)] * len(prev)
    return pl.pallas_call(
        functools.partial(_inproj_kernel, prompt=prompt),
        grid=(nt,),
        in_specs=in_specs,
        out_specs=out_specs,
        out_shape=out_shape,
        input_output_aliases=aliases,
        compiler_params=_cparams(("parallel",)),
        name="inproj_prompt" if prompt else "inproj_sample",
    )(*ins)


def _attn_t_kernel(*refs, n_heads, dv, n_maps, out_scale):
    if n_maps == 2:
        lam_ref, qt_ref, k_ref, vt_ref, g_ref, o_ref, q_sc, m_sc, acc_sc, sta_sc, stb_sc, mxa_sc, mxb_sc = refs
    else:
        qt_ref, k_ref, vt_ref, o_ref, m_sc, acc_sc, sta_sc, stb_sc, mxa_sc, mxb_sc = refs
    i = pl.program_id(1)
    dva = dv + ONES_ROWS
    width = n_maps * TQ
    heads = range(n_heads)
    if n_maps == 2:
        for h in heads:
            qt = qt_ref[0, h * LANES:(h + 1) * LANES, :]
            row = lax.broadcasted_iota(jnp.int32, qt.shape, 0)
            zero = jnp.zeros_like(qt)
            q_sc[h, :, :TQ] = jnp.where(row < DH_A, qt, zero)
            q_sc[h, :, TQ:] = jnp.where(row >= DH_A, qt, zero)
    m_sc[...] = jnp.full(m_sc.shape, NEG_INF, F32)
    acc_sc[...] = jnp.zeros(acc_sc.shape, F32)
    rel = ((lax.broadcasted_iota(jnp.int32, (TK, width), 1) & (TQ - 1)) // CHUNK
           - lax.broadcasted_iota(jnp.int32, (TK, width), 0) // CHUNK)

    def scores(j, st_ref, mx_ref):
        rows = pl.ds(pl.multiple_of(j * TK, TK), TK)
        for h in heads:
            st = jnp.dot(k_ref[rows, h * LANES:(h + 1) * LANES],
                         q_sc[h] if n_maps == 2 else qt_ref[0, h * LANES:(h + 1) * LANES, :],
                         preferred_element_type=F32)
            st_ref[h] = st
            mx_ref[h] = jnp.max(st, axis=0, keepdims=True)

    def update(j, st_ref, mx_ref, masked):
        pts, alphas = [], []
        for h in heads:
            st = st_ref[h]
            if masked:
                st = jnp.where(rel >= j * (TK // CHUNK) - i * (TQ // CHUNK), st, NEG_INF)
                tile_max = jnp.max(st, axis=0, keepdims=True)
            else:
                tile_max = mx_ref[h]
            m = m_sc[h]
            m_new = jnp.maximum(m, tile_max)
            alphas.append(jnp.exp2(m - m_new))
            pts.append(jnp.exp2(st - m_new).astype(BF16))
            m_sc[h] = m_new
        pvs = [jnp.dot(vt_ref[j, h * dva:(h + 1) * dva, :], pts[h], preferred_element_type=F32) for h in heads]
        for h in heads:
            acc_sc[h] = alphas[h] * acc_sc[h] + pvs[h]

    def full_pair(p, carry):
        j = 2 * p
        scores(j + 1, stb_sc, mxb_sc)
        update(j, sta_sc, mxa_sc, False)
        scores(j + 2, sta_sc, mxa_sc)
        update(j + 1, stb_sc, mxb_sc, False)
        return carry

    assert TQ == 2 * TK
    scores(0, sta_sc, mxa_sc)
    lax.fori_loop(0, i, full_pair, 0)
    scores(2 * i + 1, stb_sc, mxb_sc)
    update(2 * i, sta_sc, mxa_sc, True)
    update(2 * i + 1, stb_sc, mxb_sc, True)

    for h in heads:
        acc = acc_sc[h]
        inv = acc[:dv] / acc[dv:dv + 1]
        if n_maps == 2:
            o = inv[:, :TQ] - lam_ref[0] * inv[:, TQ:]
            o = o * lax.rsqrt(jnp.mean(o * o, axis=0, keepdims=True) + EPS) * g_ref[...] * out_scale
        else:
            o = inv
        o_ref[0, h * dv:(h + 1) * dv, :] = o.astype(o_ref.dtype)


def _attn_t(qt, k, vt, *, batch, seq, n_heads, dv, n_maps, lam=None, g=None, out_scale=1.0, name):
    nq = seq // TQ
    dva = dv + ONES_ROWS
    width = n_maps * TQ
    once = pl.Buffered(1)
    ins, in_specs, scratch = [], [], []
    if n_maps == 2:
        ins.append(lam)
        in_specs.append(pl.BlockSpec(memory_space=pltpu.SMEM))
        scratch.append(pltpu.VMEM((n_heads, LANES, width), BF16))
    ins += [qt, k, vt]
    in_specs += [pl.BlockSpec((1, n_heads * LANES, TQ), lambda b, i: (b * nq + i, 0, 0)),
                 pl.BlockSpec((seq, n_heads * LANES), lambda b, i: (b, 0), pipeline_mode=once),
                 pl.BlockSpec((seq // TK, n_heads * dva, TK), lambda b, i: (b, 0, 0), pipeline_mode=once)]
    if n_maps == 2:
        ins.append(g)
        in_specs.append(pl.BlockSpec((dv, 1), lambda b, i: (0, 0)))
    return pl.pallas_call(
        functools.partial(_attn_t_kernel, n_heads=n_heads, dv=dv, n_maps=n_maps, out_scale=out_scale),
        grid=(batch, nq),
        in_specs=in_specs,
        out_specs=pl.BlockSpec((1, n_heads * dv, TQ), lambda b, i: (b * nq + i, 0, 0)),
        out_shape=jax.ShapeDtypeStruct((batch * nq, n_heads * dv, TQ), BF16),
        scratch_shapes=scratch + [pltpu.VMEM((n_heads, 1, width), F32), pltpu.VMEM((n_heads, dva, width), F32),
                                  pltpu.VMEM((n_heads, TK, width), F32), pltpu.VMEM((n_heads, TK, width), F32),
                                  pltpu.VMEM((n_heads, 1, width), F32), pltpu.VMEM((n_heads, 1, width), F32)],
        compiler_params=_cparams(("parallel", "arbitrary")),
        name=name,
    )(*ins)


def _flash_steps(ss, vs, m_sc, l_sc, acc_sc):
    ps, alphas = [], []
    for c, s in enumerate(ss):
        m_prev = m_sc[c]
        m_new = jnp.maximum(m_prev, jnp.max(s, axis=-1, keepdims=True))
        alpha = jnp.exp2(m_prev - m_new)
        cols = s.shape[-1]
        p = jnp.exp2(s - (_widen(m_new, cols) if cols % LANES == 0 else m_new[:, :cols]))
        l_sc[c] = alpha * l_sc[c] + jnp.sum(p, axis=-1, keepdims=True)
        m_sc[c] = m_new
        ps.append(p.astype(BF16))
        alphas.append(alpha)
    pvs = [jnp.dot(p, v, preferred_element_type=F32) for p, v in zip(ps, vs)]
    for c, pv in enumerate(pvs):
        acc_sc[c] = _widen(alphas[c], acc_sc.shape[-1]) * acc_sc[c] + pv


def _init_flash(m_sc, l_sc, acc_sc):
    m_sc[...] = jnp.full(m_sc.shape, NEG_INF, F32)
    l_sc[...] = jnp.zeros(l_sc.shape, F32)
    acc_sc[...] = jnp.zeros(acc_sc.shape, F32)


def _stack_maps(q):
    lane = lax.broadcasted_iota(jnp.int32, q.shape, 1)
    zero = jnp.zeros_like(q)
    return jnp.concatenate([jnp.where(lane < DH_A, q, zero), jnp.where(lane >= DH_A, q, zero)], axis=0)


def _diff_sample_kernel(lam_ref, q_ref, kc_ref, vc_ref, kn_ref, vn_ref, g_ref, o_ref, m_sc, l_sc, acc_sc,
                        *, tq, tk, n_cache_tiles, out_scale):
    heads = range(H_A)
    lanes = [slice(h * DV_A, (h + 1) * DV_A) for h in heads]
    qss = [_stack_maps(q_ref[:, lanes[h]]) for h in heads]
    _init_flash(m_sc, l_sc, acc_sc)

    for j in range(n_cache_tiles):
        ss = [jnp.dot(qss[h], kc_ref[lanes[h], j * tk:(j + 1) * tk].astype(BF16), preferred_element_type=F32)
              for h in heads]
        vs = [vc_ref[0, pl.ds(j * tk * H_A + h, tk, stride=H_A), :].astype(BF16) for h in heads]
        _flash_steps(ss, vs, m_sc, l_sc, acc_sc)
    _flash_steps([_dot_nt(qss[h], kn_ref[:, lanes[h]]) for h in heads], [vn_ref[:, lanes[h]] for h in heads],
                 m_sc, l_sc, acc_sc)
    for h in heads:
        inv = acc_sc[h] / l_sc[h]
        o = inv[:tq] - lam_ref[0] * inv[tq:]
        o_ref[:, lanes[h]] = (_rms(o, g_ref[...]) * out_scale).astype(o_ref.dtype)


def _diff_sample(lam, qa, kcache, vcache, kb, vb, g_subln, *, layer, batch, seq, past, out_scale):
    tk = min(512, past)
    smem = pl.BlockSpec(memory_space=pltpu.SMEM)
    new = pl.BlockSpec((seq, COLS_A), lambda b: (b, 0))
    kcache_spec = pl.BlockSpec((COLS_A, past), lambda b: (layer * batch + b, 0))
    vcache_spec = pl.BlockSpec((1, past * H_A, DV_A), lambda b: (layer * batch + b, 0, 0))
    return pl.pallas_call(
        functools.partial(_diff_sample_kernel, tq=seq, tk=tk, n_cache_tiles=past // tk, out_scale=out_scale),
        grid=(batch,),
        in_specs=[smem, new, kcache_spec, vcache_spec, new, new, pl.BlockSpec((1, DV_A), lambda b: (0, 0))],
        out_specs=new,
        out_shape=jax.ShapeDtypeStruct((batch * seq, COLS_A), BF16),
        scratch_shapes=[pltpu.VMEM((H_A, 2 * seq, LANES), F32), pltpu.VMEM((H_A, 2 * seq, LANES), F32),
                        pltpu.VMEM((H_A, 2 * seq, DV_A), F32)],
        compiler_params=_cparams(("parallel",)),
        name="diff_attn_sample",
    )(lam, qa, kcache, vcache, kb, vb, g_subln)


def _mla_sample_kernel(qn_ref, qr_ref, cc_ref, rc_ref, cn_ref, rn_ref, wuk_ref, wz_ref, o_ref,
                       ql_sc, qr_sc, m_sc, l_sc, acc_sc, *, tq, tk, n_cache_tiles):
    qn = qn_ref[...]
    qr = qr_ref[...].astype(F32)
    for h in range(H_B):
        ql = jnp.dot(qn[:, h * QK_NOPE:(h + 1) * QK_NOPE], wuk_ref[h], preferred_element_type=F32)
        ql_sc[h * tq:(h + 1) * tq, :] = ql.astype(BF16)
        qr_sc[h * tq:(h + 1) * tq, :] = qr[:, h * QK_ROPE:(h + 1) * QK_ROPE].astype(BF16)
    _init_flash(m_sc, l_sc, acc_sc)
    n_chains = m_sc.shape[0]
    half = H_B * tq // n_chains
    parts = [slice(c * half, (c + 1) * half) for c in range(n_chains)]

    for j in range(n_cache_tiles):
        c = cc_ref[0, j * tk:(j + 1) * tk, :].astype(BF16)
        rt = rc_ref[:, j * tk:(j + 1) * tk].astype(BF16)
        ss = [_dot_nt(ql_sc[p, :], c) + jnp.dot(qr_sc[p, :], rt, preferred_element_type=F32) for p in parts]
        _flash_steps(ss, [c] * n_chains, m_sc, l_sc, acc_sc)
    c = cn_ref[...]
    ss = [_dot_nt(ql_sc[p, :], c) + _dot_nt(qr_sc[p, :], rn_ref[...]) for p in parts]
    _flash_steps(ss, [c] * n_chains, m_sc, l_sc, acc_sc)
    ob = jnp.zeros(o_ref.shape, F32)
    heads_per_chain = H_B // n_chains
    for h in range(H_B):
        c, r = divmod(h, heads_per_chain)
        o_lat = (acc_sc[c, r * tq:(r + 1) * tq, :] / _widen(l_sc[c, r * tq:(r + 1) * tq, :], KV_LORA)).astype(BF16)
        ob += jnp.dot(o_lat, wz_ref[h], preferred_element_type=F32)
    o_ref[...] = ob.astype(o_ref.dtype)


def _mla_sample(qn, qr, ccache, rcache, ckvb, krb, wuk_t, wz, *, layer, batch, seq, past):
    tk = min(512, past)
    rows = H_B * seq
    return pl.pallas_call(
        functools.partial(_mla_sample_kernel, tq=seq, tk=tk, n_cache_tiles=past // tk),
        grid=(batch,),
        in_specs=[pl.BlockSpec((seq, COLS_QN), lambda b: (b, 0)),
                  pl.BlockSpec((seq, COLS_QR), lambda b: (b, 0)),
                  pl.BlockSpec((1, past, KV_LORA), lambda b: (layer * batch + b, 0, 0)),
                  pl.BlockSpec((QK_ROPE, past), lambda b: (layer * batch + b, 0)),
                  pl.BlockSpec((seq, KV_LORA), lambda b: (b, 0)),
                  pl.BlockSpec((seq, QK_ROPE), lambda b: (b, 0)),
                  pl.BlockSpec((H_B, QK_NOPE, KV_LORA), lambda b: (0, 0, 0)),
                  pl.BlockSpec((H_B, KV_LORA, COLS_VB), lambda b: (0, 0, 0))],
        out_specs=pl.BlockSpec((seq, COLS_VB), lambda b: (b, 0)),
        out_shape=jax.ShapeDtypeStruct((batch * seq, COLS_VB), BF16),
        scratch_shapes=[pltpu.VMEM((rows, KV_LORA), BF16), pltpu.VMEM((rows, QK_ROPE), BF16),
                        pltpu.VMEM((2, rows // 2, LANES), F32), pltpu.VMEM((2, rows // 2, LANES), F32),
                        pltpu.VMEM((2, rows // 2, KV_LORA), F32)],
        compiler_params=_cparams(("parallel",)),
        name="mla_attn_sample",
    )(qn, qr, ccache, rcache, ckvb, krb, wuk_t, wz)


def _merge_kernel(oa_ref, ob_ref, gate_ref, x_ref, wa_ref, wb_ref, wo_ref, gf_ref, wr_ref, br_ref, tri_ref,
                  x1_ref, h_ref, route_ref, count_ref, run_sc, *, tiled_t):
    oa = _load_tiles_t(oa_ref) if tiled_t else oa_ref[...]
    ob = _load_tiles_t(ob_ref) if tiled_t else ob_ref[...]
    tm = x_ref.shape[0]
    subs = [slice(r * MERGE_SUB, (r + 1) * MERGE_SUB) for r in range(tm // MERGE_SUB)]
    yas = [jnp.dot(oa[s], wa_ref[...], preferred_element_type=F32) for s in subs]
    ybs = [jnp.dot(ob[s], wb_ref[...], preferred_element_type=F32) for s in subs]
    merged = []
    for s, ya, yb in zip(subs, yas, ybs):
        gates = gate_ref[s, :].astype(F32)
        merged.append((gates[:, :D_MODEL] * ya + gates[:, D_MODEL:] * yb).astype(BF16))
    x1s = [x_ref[s, :] + jnp.dot(mg, wo_ref[...], preferred_element_type=F32) for s, mg in zip(subs, merged)]
    logit_parts = []
    for s, x1 in zip(subs, x1s):
        x1_ref[s, :] = x1
        h = _rms(x1, gf_ref[...])
        h_hi = h.astype(BF16)
        h_ref[s, :] = h_hi
        h_lo = (h - h_hi.astype(F32)).astype(BF16)
        logit_parts.append(jnp.dot(h_hi, wr_ref[0], preferred_element_type=F32)
                           + jnp.dot(h_lo, wr_ref[0], preferred_element_type=F32)
                           + jnp.dot(h_hi, wr_ref[1], preferred_element_type=F32))

    logits = jnp.concatenate(logit_parts, axis=0) + br_ref[...]
    lane = lax.broadcasted_iota(jnp.int32, logits.shape, 1)
    big = jnp.int32(ROUTE_COLS)

    def top1(mask):
        v = jnp.max(jnp.where(mask, logits, NEG_INF), axis=-1, keepdims=True)
        idx = jnp.min(jnp.where(mask & (logits == v), lane, big), axis=-1, keepdims=True)
        return v, idx

    gmask = lane < N_GROUPS
    gmax, gidx = top1(gmask)
    g_w = 1.0 / jnp.sum(jnp.where(gmask, jnp.exp(logits - gmax), 0.0), axis=-1, keepdims=True)
    lo = N_GROUPS + gidx * EXPERTS_PER_GROUP
    emask = (lane >= lo) & (lane < lo + EXPERTS_PER_GROUP)
    v1, i1 = top1(emask)
    v2, i2 = top1(emask & (lane != i1))
    e2 = jnp.exp(v2 - v1)
    w1 = g_w / (1.0 + e2)
    w2 = g_w * e2 / (1.0 + e2)
    @pl.when(pl.program_id(0) == 0)
    def _():
        run_sc[...] = jnp.zeros(run_sc.shape, F32)

    e1 = i1 - N_GROUPS
    e2i = i2 - N_GROUPS
    picks = jnp.where((lane == e1) | (lane == e2i), 1.0, 0.0)
    before = jnp.dot(tri_ref[...], picks.astype(BF16), preferred_element_type=F32) + run_sc[...]
    rank1 = jnp.sum(jnp.where(lane == e1, before, 0.0), axis=-1, keepdims=True)
    rank2 = jnp.sum(jnp.where(lane == e2i, before, 0.0), axis=-1, keepdims=True)
    run_sc[...] = run_sc[...] + jnp.sum(picks, axis=0, keepdims=True)
    count_ref[...] = run_sc[...]

    vals = [e1.astype(F32), e2i.astype(F32), w1, w2, rank1, rank2]
    route = jnp.zeros(logits.shape, F32)
    for n, v in enumerate(vals):
        route = jnp.where(lane == n, v, route)
    route_ref[...] = route


def _merge(oa, ob, gates, x, w, *, tm, tiled_t):
    t = x.shape[0]
    tok = lambda i: (i, 0)
    full = lambda i: (0, 0)
    if tiled_t:
        o_specs = [pl.BlockSpec((tm // TQ, COLS_A, TQ), lambda i: (i, 0, 0)),
                   pl.BlockSpec((tm // TQ, COLS_VB, TQ), lambda i: (i, 0, 0))]
    else:
        o_specs = [pl.BlockSpec((tm, COLS_A), tok), pl.BlockSpec((tm, COLS_VB), tok)]
    return pl.pallas_call(
        functools.partial(_merge_kernel, tiled_t=tiled_t),
        grid=(t // tm,),
        in_specs=o_specs + [
            pl.BlockSpec((tm, 2 * D_MODEL), tok), pl.BlockSpec((tm, D_MODEL), tok),
            pl.BlockSpec((COLS_A, D_MODEL), full), pl.BlockSpec((COLS_VB, D_MODEL), full),
            pl.BlockSpec((D_MODEL, D_MODEL), full), pl.BlockSpec((1, D_MODEL), full),
            pl.BlockSpec((2, D_MODEL, ROUTE_COLS), lambda i: (0, 0, 0)), pl.BlockSpec((1, ROUTE_COLS), full),
            pl.BlockSpec((tm, tm), full)],
        out_specs=[pl.BlockSpec((tm, D_MODEL), tok), pl.BlockSpec((tm, D_MODEL), tok),
                   pl.BlockSpec((tm, ROUTE_COLS), tok), pl.BlockSpec((1, ROUTE_COLS), full)],
        out_shape=[jax.ShapeDtypeStruct((t, D_MODEL), F32), jax.ShapeDtypeStruct((t, D_MODEL), BF16),
                   jax.ShapeDtypeStruct((t, ROUTE_COLS), F32), jax.ShapeDtypeStruct((1, ROUTE_COLS), F32)],
        scratch_shapes=[pltpu.VMEM((1, ROUTE_COLS), F32)],
        compiler_params=_cparams(("arbitrary",)),
        name="merge_router",
    )(oa, ob, gates, x, w['wa'], w['wb'], w['wo'], w['g_ffn'], w['w_route'], w['b_route'],
      jnp.tril(jnp.ones((tm, tm), BF16), -1))


def _expert_kernel(blk_e_ref, n_used_ref, x_ref, wgu_ref, wdn_ref, y_ref):
    i = pl.program_id(0)

    @pl.when(i < n_used_ref[0])
    def _():
        gu = jnp.dot(x_ref[...], wgu_ref[0].astype(BF16), preferred_element_type=F32)
        gate, up = gu[:, :D_EXPERT], gu[:, D_EXPERT:]
        a = (gate * jax.nn.sigmoid(gate) * up).astype(BF16)
        y_ref[...] = jnp.dot(a, wdn_ref[0].astype(BF16), preferred_element_type=F32).astype(y_ref.dtype)

    @pl.when(i >= n_used_ref[0])
    def _():
        y_ref[...] = jnp.zeros(y_ref.shape, y_ref.dtype)


def _experts(blk_e, n_used, xs, wgu, wdn, *, layer, bm):
    n_slots = xs.shape[0]
    grid_spec = pltpu.PrefetchScalarGridSpec(
        num_scalar_prefetch=2,
        grid=(n_slots // bm,),
        in_specs=[pl.BlockSpec((bm, D_MODEL), lambda i, be, nu: (i, 0)),
                  pl.BlockSpec((1, D_MODEL, 2 * D_EXPERT), lambda i, be, nu: (layer * N_EXPERTS + be[i], 0, 0)),
                  pl.BlockSpec((1, D_EXPERT, D_MODEL), lambda i, be, nu: (layer * N_EXPERTS + be[i], 0, 0))],
        out_specs=pl.BlockSpec((bm, D_MODEL), lambda i, be, nu: (i, 0)),
    )
    return pl.pallas_call(
        _expert_kernel,
        grid_spec=grid_spec,
        out_shape=jax.ShapeDtypeStruct((n_slots, D_MODEL), BF16),
        compiler_params=_cparams(("arbitrary",)),
        name="experts",
    )(blk_e, n_used, xs, wgu, wdn)


def _combine_kernel(x_ref, y0_ref, y1_ref, route_ref, g_ref, o_ref, *, final_norm):
    r = route_ref[...]
    y = x_ref[...] + r[:, 2:3] * y0_ref[...].astype(F32) + r[:, 3:4] * y1_ref[...].astype(F32)
    if final_norm:
        y = _rms(y, g_ref[...])
    o_ref[...] = y


def _combine(x1, y0, y1, route, g_final, *, tm, final_norm):
    t = x1.shape[0]
    tok = lambda i: (i, 0)
    return pl.pallas_call(
        functools.partial(_combine_kernel, final_norm=final_norm),
        grid=(t // tm,),
        in_specs=[pl.BlockSpec((tm, D_MODEL), tok), pl.BlockSpec((tm, D_MODEL), tok),
                  pl.BlockSpec((tm, D_MODEL), tok), pl.BlockSpec((tm, ROUTE_COLS), tok),
                  pl.BlockSpec((1, D_MODEL), lambda i: (0, 0))],
        out_specs=pl.BlockSpec((tm, D_MODEL), tok),
        out_shape=jax.ShapeDtypeStruct((t, D_MODEL), F32),
        compiler_params=_cparams(("parallel",)),
        name="combine",
    )(x1, y0, y1, route, g_final)


def _transpose_cast_kernel(x_ref, o_ref):
    o_ref[...] = x_ref[...].T.astype(o_ref.dtype)


def _transpose_cast(xt, block=256):
    c, d = xt.shape
    return pl.pallas_call(
        _transpose_cast_kernel,
        grid=(c // block,),
        in_specs=[pl.BlockSpec((block, d), lambda i: (i, 0))],
        out_specs=pl.BlockSpec((d, block), lambda i: (0, i)),
        out_shape=jax.ShapeDtypeStruct((d, c), BF16),
        compiler_params=_cparams(("parallel",)),
        name="weight_transpose",
    )(xt)


def _rope_tables(pos):
    lane = jnp.arange(LANES, dtype=jnp.int32)

    def tab(chunk):
        half = chunk // 2
        inv = ROPE_THETA ** (-(lane % half).astype(F32) / half)
        ang = pos.astype(F32)[:, None] * inv[None, :]
        first = (lane % chunk) < half
        return jnp.cos(ang), jnp.where(first[None, :], -jnp.sin(ang), jnp.sin(ang))

    c64, s64 = tab(DH_A)
    c32, s32 = tab(QK_ROPE)
    rope_lane = ((lane >= QK_NOPE) & (lane < QK_NOPE + QK_ROPE))[None, :]
    return dict(c64=c64, s64=s64, c32=c32, s32=s32,
                cq=jnp.where(rope_lane, c32, 1.0), sq=jnp.where(rope_lane, s32, 0.0))


def _moe_plan(route, counts, bm):
    n_tok = route.shape[0]
    n_asg = n_tok * TOP_K
    flat_e = route[:, :TOP_K].astype(jnp.int32).reshape(n_asg)
    rank = route[:, 4:4 + TOP_K].astype(jnp.int32).reshape(n_asg)
    counts = counts[0, :N_EXPERTS].astype(jnp.int32)
    padded = (counts + bm - 1) // bm * bm
    pad_end = jnp.cumsum(padded)
    pad_start = pad_end - padded
    dest = (pad_start[flat_e] + rank).astype(jnp.int32)
    n_slots = n_asg + N_EXPERTS * bm
    n_blk = n_slots // bm
    slot_tok = (jnp.arange(n_slots, dtype=jnp.int32) % n_tok).at[dest].set(
        jnp.arange(n_asg, dtype=jnp.int32) // TOP_K, unique_indices=True, mode="promise_in_bounds")
    blk_start = jnp.arange(n_blk, dtype=jnp.int32) * bm
    blk_e = jnp.minimum(jnp.sum((pad_end[None, :] <= blk_start[:, None]).astype(jnp.int32), axis=1),
                        N_EXPERTS - 1)
    n_used = (pad_end[-1:] // bm).astype(jnp.int32)
    return slot_tok, blk_e, n_used, dest.reshape(n_tok, TOP_K)


def _head_groups(parts):
    rows = parts[0].shape[0]
    used = sum(p.shape[-1] for p in parts)
    pad = jnp.zeros((rows, H_B, LANES - used), parts[0].dtype)
    return jnp.concatenate(list(parts) + [pad], axis=-1).reshape(rows, H_B * LANES)


def _layer_weights(l, g_attn, w_in, g_subln, w_a, g_q_lat, w_uq, g_kv_lat, w_uk, w_uv, w_b, w_o, g_ffn,
                   w_group, b_group, w_router, b_router):
    wt = jnp.swapaxes(w_in, 1, 2)[l]
    o_kr = 3 * COLS_A + Q_LORA + KV_LORA
    w_pack = _transpose_cast(jnp.concatenate(
        [wt[:o_kr], jnp.pad(wt[o_kr:o_kr + QK_ROPE], ((0, KR_PAD - QK_ROPE), (0, 0))), wt[o_kr + QK_ROPE:]],
        axis=0))
    wq = w_uq[l].reshape(Q_LORA, H_B, QK_NOPE + QK_ROPE)
    w_uq_p = jnp.concatenate([wq[:, :, :QK_NOPE].reshape(Q_LORA, COLS_QN),
                              wq[:, :, QK_NOPE:].reshape(Q_LORA, COLS_QR)], axis=1).astype(BF16)
    w_uq_g = _head_groups([wq]).astype(BF16)
    w_kn = _head_groups([w_uk[l]]).astype(BF16)
    place = jnp.pad(jnp.eye(QK_ROPE, dtype=F32), ((0, KR_PAD - QK_ROPE), (0, 0)))
    w_krp = _head_groups([jnp.zeros((KR_PAD, H_B, QK_NOPE), F32),
                          jnp.broadcast_to(place[:, None, :], (KR_PAD, H_B, QK_ROPE))]).astype(BF16)
    w_uv_all = w_uv[l].reshape(KV_LORA, COLS_VB).astype(BF16)
    wuk_t = jnp.transpose(w_uk[l], (1, 2, 0)).astype(BF16)
    wuv = jnp.transpose(w_uv[l], (1, 0, 2))
    eye = jnp.eye(H_B, dtype=F32)
    wz = (wuv[:, :, None, :] * eye[:, None, :, None]).reshape(H_B, KV_LORA, COLS_VB).astype(BF16)
    w_route = jnp.concatenate(
        [w_group[l], jnp.transpose(w_router[l], (1, 0, 2)).reshape(D_MODEL, N_EXPERTS),
         jnp.zeros((D_MODEL, ROUTE_COLS - N_GROUPS - N_EXPERTS), F32)], axis=1)
    w_route_hi = w_route.astype(BF16)
    w_route = jnp.stack([w_route_hi, (w_route - w_route_hi.astype(F32)).astype(BF16)])
    b_route = jnp.concatenate([b_group[l], b_router[l].reshape(N_EXPERTS),
                               jnp.zeros((ROUTE_COLS - N_GROUPS - N_EXPERTS,), F32)])[None, :]
    return dict(g_attn=g_attn[l][None, :], w_pack=w_pack, g_q=g_q_lat[l][None, :], g_kv=g_kv_lat[l][None, :],
                w_uq_p=w_uq_p, w_uq_g=w_uq_g, w_kn=w_kn, w_krp=w_krp, w_uv_all=w_uv_all, wuk_t=wuk_t, wz=wz,
                g_subln=g_subln[l][None, :], g_subln_col=g_subln[l][:, None],
                wa=w_a[l].astype(BF16), wb=w_b[l].astype(BF16), wo=w_o[l].astype(BF16),
                g_ffn=g_ffn[l][None, :], w_route=w_route, b_route=b_route)


def _trunk(x, pos_tile, n_pos_tiles, tm, caches, lws, lams, wgu, wdn, g_final, *, batch, seq, bm):
    depth = len(lws)
    t = batch * seq
    x = x.reshape(t, D_MODEL)
    tabs = _rope_tables(pos_tile)
    prompt = caches is None
    ks, vs, cs, rs = [], [], [], []
    new_cache = None
    for l in range(depth):
        w = lws[l]
        out_scale = 1.0 - (0.8 - 0.6 * math.exp(-0.3 * l))
        outs = _inproj(x, w, tabs, tm=tm, n_pos_tiles=n_pos_tiles, prompt=prompt, layer=l, depth=depth,
                       batch=batch, prev=new_cache)
        k, v, ckv, kr = outs[:4]
        if prompt:
            new_cache = (k, v, ckv, kr)
            qat, kb, vat, qbt, kq, vbt, gates = outs[4:]
            oa = _attn_t(qat, kb, vat, batch=batch, seq=seq, n_heads=H_A, dv=DV_A, n_maps=2, lam=lams[l],
                         g=w['g_subln_col'], out_scale=out_scale, name="diff_attn_prompt")
            ob = _attn_t(qbt, kq, vbt, batch=batch, seq=seq, n_heads=H_B, dv=V_B, n_maps=1,
                         name="mla_attn_prompt")
        else:
            qa, kb, vb, ckvb, krb, qn, qr, gates = outs[4:]
            kc, vc, cc, rc, past = caches
            oa = _diff_sample(lams[l], qa, kc, vc, kb, vb, w['g_subln'], layer=l, batch=batch, seq=seq,
                              past=past, out_scale=out_scale)
            ob = _mla_sample(qn, qr, cc, rc, ckvb, krb, w['wuk_t'], w['wz'], layer=l, batch=batch, seq=seq,
                             past=past)
        x1, h, route, counts = _merge(oa, ob, gates, x, w, tm=tm, tiled_t=prompt)
        slot_tok, blk_e, n_used, dest = _moe_plan(route, counts, bm)
        xs = h.at[slot_tok].get(mode="promise_in_bounds")
        y_slots = _experts(blk_e, n_used, xs, wgu, wdn, layer=l, bm=bm)
        y0 = y_slots.at[dest[:, 0]].get(mode="promise_in_bounds")
        y1 = y_slots.at[dest[:, 1]].get(mode="promise_in_bounds")
        x = _combine(x1, y0, y1, route, g_final, tm=tm, final_norm=(l == depth - 1))
        ks.append(k)
        vs.append(v)
        cs.append(ckv)
        rs.append(kr)
    y = x.reshape(batch, seq, D_MODEL)
    if prompt:
        k, v, ckv, kr = new_cache
        return (y,
                jnp.transpose(k.reshape(depth, batch, H_A, 2, DH_A, seq), (0, 1, 5, 2, 3, 4)),
                v.reshape(depth, batch, seq, H_A, DV_A),
                ckv.reshape(depth, batch, seq, KV_LORA),
                jnp.transpose(kr.reshape(depth, batch, QK_ROPE, seq), (0, 1, 3, 2)))
    return (y,
            jnp.stack(ks).reshape(depth, batch, seq, H_A, 2, DH_A),
            jnp.stack(vs).reshape(depth, batch, seq, H_A, DV_A),
            jnp.stack(cs).reshape(depth, batch, seq, KV_LORA),
            jnp.stack(rs).reshape(depth, batch, seq, QK_ROPE))


def kernel(x_prompt, x_sample, cache_diff_k, cache_diff_v, cache_mla_ckv, cache_mla_krope, g_attn, w_in,
           lambda_q1, lambda_k1, lambda_q2, lambda_k2, g_subln, w_a, g_q_lat, w_uq, g_kv_lat, w_uk, w_uv, w_b,
           w_o, g_ffn, w_group, b_group, w_router, b_router, w_gate_up, w_down, g_final):
    depth = w_in.shape[0]
    batch, seq, _ = x_prompt.shape
    dec_batch, dec_seq, _ = x_sample.shape
    past = cache_diff_k.shape[2]
    assert dec_seq == CHUNK and past % CHUNK == 0, "sample frames must form exactly one new chunk"
    assert seq % TQ == 0

    lws = [_layer_weights(l, g_attn, w_in, g_subln, w_a, g_q_lat, w_uq, g_kv_lat, w_uk, w_uv, w_b, w_o, g_ffn,
                          w_group, b_group, w_router, b_router) for l in range(depth)]
    lams = []
    for l in range(depth):
        lam_init = 0.8 - 0.6 * math.exp(-0.3 * l)
        lam = (jnp.exp(jnp.sum(lambda_q1[l] * lambda_k1[l])) - jnp.exp(jnp.sum(lambda_q2[l] * lambda_k2[l]))
               + lam_init)
        lams.append(lam.reshape(1).astype(F32))
    wgu = w_gate_up.reshape(depth * N_EXPERTS, D_MODEL, 2 * D_EXPERT)
    wdn = w_down.reshape(depth * N_EXPERTS, D_EXPERT, D_MODEL)
    g_fin = g_final[None, :]

    tm_p = min(512, seq)
    outs_p = _trunk(x_prompt, jnp.arange(seq, dtype=jnp.int32), seq // tm_p, tm_p, None, lws, lams, wgu, wdn,
                    g_fin, batch=batch, seq=seq, bm=min(512, batch * seq))

    tm_s = min(512, dec_batch * dec_seq)
    pos_tile = jnp.tile(past + jnp.arange(dec_seq, dtype=jnp.int32), tm_s // dec_seq)
    caches = (jnp.transpose(cache_diff_k, (0, 1, 3, 4, 5, 2)).reshape(depth * dec_batch * COLS_A, past),
              cache_diff_v.reshape(depth * dec_batch, past * H_A, DV_A),
              cache_mla_ckv.reshape(depth * dec_batch, past, KV_LORA),
              jnp.transpose(cache_mla_krope, (0, 1, 3, 2)).reshape(depth * dec_batch * QK_ROPE, past), past)
    outs_s = _trunk(x_sample, pos_tile, 1, tm_s, caches, lws, lams, wgu, wdn, g_fin,
                    batch=dec_batch, seq=dec_seq, bm=min(128, dec_batch * dec_seq))

    return (outs_p[0], outs_s[0]) + outs_p[1:] + outs_s[1:]
```

```python
import functools
import math

import jax
import jax.numpy as jnp
from jax import lax
from jax.experimental import pallas as pl
from jax.experimental.pallas import tpu as pltpu

D_MODEL = 1024
CHUNK = 64
ROPE_THETA = 10000.0
EPS = 1e-6
H_A = 4
DH_A = 64
DV_A = 2 * DH_A
H_B = 8
Q_LORA = 384
KV_LORA = 256
QK_NOPE = 64
QK_ROPE = 32
V_B = 64
N_GROUPS = 4
EXPERTS_PER_GROUP = 8
N_EXPERTS = N_GROUPS * EXPERTS_PER_GROUP
TOP_K = 2
D_EXPERT = 512

COLS_A = H_A * 2 * DH_A
COLS_QR = H_B * QK_ROPE
COLS_QN = H_B * QK_NOPE
COLS_VB = H_B * V_B
LANES = 128
COLS_QB = H_B * LANES
KR_PAD = LANES
PACK_COLS = 3 * COLS_A + Q_LORA + KV_LORA + KR_PAD + 2 * D_MODEL
ROUTE_COLS = LANES
ROUTE_ROWS = 8
TQ = 512
TK = 256
ONES_ROWS = 16
MERGE_SUB = 256
VMEM_LIMIT = 56 * 1024 * 1024
LOG2E = 1.4426950408889634
SCALE_A = DH_A ** -0.5 * LOG2E
SCALE_B = (QK_NOPE + QK_ROPE) ** -0.5 * LOG2E

F32 = jnp.float32
BF16 = jnp.bfloat16
NEG_INF = float("-inf")


def _cparams(sem):
    return pltpu.CompilerParams(dimension_semantics=sem, vmem_limit_bytes=VMEM_LIMIT)


def _rms(x, g):
    return x * lax.rsqrt(jnp.mean(x * x, axis=-1, keepdims=True) + EPS) * g


def _widen(tab, cols):
    reps = cols // LANES
    return tab if reps == 1 else jnp.concatenate([tab] * reps, axis=-1)


def _rope(x, cos, sin_signed, chunk):
    n = x.shape[-1]
    half = chunk // 2
    lane = lax.broadcasted_iota(jnp.int32, x.shape, 1)
    fwd = pltpu.roll(x, n - half, 1)
    bwd = pltpu.roll(x, half, 1)
    swapped = jnp.where((lane & (chunk - 1)) < half, fwd, bwd)
    return x * _widen(cos, n) + swapped * _widen(sin_signed, n)


def _dot_nt(a, b):
    return lax.dot_general(a, b, (((1,), (1,)), ((), ())), preferred_element_type=F32)


def _store_tiles_t(ref, val):
    tile = ref.shape[2]
    for r in range(ref.shape[0]):
        ref[r] = val[r * tile:(r + 1) * tile, :].T.astype(ref.dtype)


def _store_vt_ones(ref, val, n_heads, dv):
    dva = dv + ONES_ROWS
    ones = jnp.ones((ONES_ROWS, TK), ref.dtype)
    for r in range(ref.shape[0]):
        vt = val[r * TK:(r + 1) * TK, :].T.astype(ref.dtype)
        for h in range(n_heads):
            ref[r, h * dva:h * dva + dv, :] = vt[h * dv:(h + 1) * dv, :]
            ref[r, h * dva + dv:(h + 1) * dva, :] = ones


def _load_tiles_t(ref):
    return jnp.concatenate([ref[r].astype(F32).T for r in range(ref.shape[0])], axis=0).astype(BF16)


def _inproj_kernel(*refs, prompt):
    (x_ref, g_ref, w_ref, gq_ref, gkv_ref, wuq_ref, c64_ref, s64_ref, c32_ref, s32_ref) = refs[:10]
    x = x_ref[...]
    hb = _rms(x, g_ref[...]).astype(BF16)

    def proj(lo, hi):
        return jnp.dot(hb, w_ref[:, lo:hi], preferred_element_type=F32)

    c64, s64 = c64_ref[...], s64_ref[...]
    c32, s32 = c32_ref[...], s32_ref[...]
    o = 0
    qa = _rope(proj(o, o + COLS_A), c64, s64, DH_A) * SCALE_A
    o += COLS_A
    ka = _rope(proj(o, o + COLS_A), c64, s64, DH_A)
    o += COLS_A
    va = proj(o, o + COLS_A)
    o += COLS_A
    cq = _rms(proj(o, o + Q_LORA), gq_ref[...]).astype(BF16)
    o += Q_LORA
    qfull = jnp.dot(cq, wuq_ref[...], preferred_element_type=F32) * SCALE_B
    ckv = _rms(proj(o, o + KV_LORA), gkv_ref[...])
    ckvb = ckv.astype(BF16)
    o += KV_LORA
    kr_pad = _rope(proj(o, o + KR_PAD), c32, s32, QK_ROPE)
    o += KR_PAD
    gates = jax.nn.sigmoid(proj(o, o + 2 * D_MODEL)).astype(BF16)

    if prompt:
        cq_ref, sq_ref, wkn_ref, wkr_ref, wuv_ref = refs[10:15]
        (k_ref, v_ref, ckv_ref, kr_ref, qat_ref, kb_ref, vat_ref, qbt_ref, kq_ref, vbt_ref,
         gate_ref) = refs[len(refs) - 11:]
        k_ref[...] = ka.T
        for h in range(H_A):
            v_ref[pl.ds(h, va.shape[0], stride=H_A), :] = va[:, h * DV_A:(h + 1) * DV_A]
        ckv_ref[...] = ckv
        kr_ref[...] = kr_pad.T[:QK_ROPE, :]
        _store_tiles_t(qat_ref, qa)
        kb_ref[...] = ka.astype(BF16)
        _store_vt_ones(vat_ref, va, H_A, DV_A)
        _store_tiles_t(qbt_ref, _rope(qfull, cq_ref[...], sq_ref[...], QK_ROPE))
        kq = (jnp.dot(ckvb, wkn_ref[...], preferred_element_type=F32)
              + jnp.dot(kr_pad.astype(BF16), wkr_ref[...], preferred_element_type=F32))
        kq_ref[...] = kq.astype(BF16)
        _store_vt_ones(vbt_ref, jnp.dot(ckvb, wuv_ref[...], preferred_element_type=F32), H_B, V_B)
    else:
        (k_ref, v_ref, ckv_ref, kr_ref, qa_ref, kb_ref, vb_ref, ckvb_ref, krb_ref, qn_ref, qr_ref,
         gate_ref) = refs[10:]
        qa_ref[...] = qa.astype(BF16)
        kb_ref[...] = ka.astype(BF16)
        vb_ref[...] = va.astype(BF16)
        ckvb_ref[...] = ckvb
        krb_ref[...] = kr_pad[:, :QK_ROPE].astype(BF16)
        qn_ref[...] = qfull[:, :COLS_QN].astype(BF16)
        qr_ref[...] = _rope(qfull[:, COLS_QN:], c32, s32, QK_ROPE).astype(BF16)
        k_ref[...] = ka
        v_ref[...] = va
        ckv_ref[...] = ckv
        kr_ref[...] = kr_pad[:, :QK_ROPE]
    gate_ref[...] = gates


def _inproj(x, w, tabs, *, tm, n_pos_tiles, prompt, layer=0, depth=1, batch=1, prev=None):
    t = x.shape[0]
    nt = t // tm
    tok = lambda i: (i, 0)
    full = lambda i: (0, 0)
    pos = lambda i: (i % n_pos_tiles, 0)
    tile3 = lambda i: (i, 0, 0)
    wuq = w['w_uq_g'] if prompt else w['w_uq_p']
    ins = [x, w['g_attn'], w['w_pack'], w['g_q'], w['g_kv'], wuq, tabs['c64'], tabs['s64'], tabs['c32'], tabs['s32']]
    in_specs = [pl.BlockSpec((tm, D_MODEL), tok), pl.BlockSpec((1, D_MODEL), full),
                pl.BlockSpec((D_MODEL, PACK_COLS), full), pl.BlockSpec((1, Q_LORA), full),
                pl.BlockSpec((1, KV_LORA), full), pl.BlockSpec(wuq.shape, full)] + [pl.BlockSpec((tm, LANES), pos)] * 4
    if prompt:
        seq = t // batch
        per_b = seq // tm
        seq_minor = lambda i: (layer * batch + i // per_b, i % per_b)
        tok_l = lambda i: (layer * nt + i, 0)
        out_specs = [pl.BlockSpec((COLS_A, tm), seq_minor), pl.BlockSpec((tm * H_A, DV_A), tok_l),
                     pl.BlockSpec((tm, KV_LORA), tok_l), pl.BlockSpec((QK_ROPE, tm), seq_minor)]
        out_shape = [jax.ShapeDtypeStruct((depth * batch * COLS_A, seq), F32),
                     jax.ShapeDtypeStruct((depth * t * H_A, DV_A), F32),
                     jax.ShapeDtypeStruct((depth * t, KV_LORA), F32),
                     jax.ShapeDtypeStruct((depth * batch * QK_ROPE, seq), F32)]
    else:
        leaves = [(COLS_A, F32), (COLS_A, F32), (KV_LORA, F32), (QK_ROPE, F32)]
        out_specs = [pl.BlockSpec((tm, c), tok) for c, _ in leaves]
        out_shape = [jax.ShapeDtypeStruct((t, c), d) for c, d in leaves]

    def add2d(c):
        out_specs.append(pl.BlockSpec((tm, c), tok))
        out_shape.append(jax.ShapeDtypeStruct((t, c), BF16))

    def add3d(c, tile):
        out_specs.append(pl.BlockSpec((tm // tile, c, tile), tile3))
        out_shape.append(jax.ShapeDtypeStruct((t // tile, c, tile), BF16))

    if prompt:
        ins += [tabs['cq'], tabs['sq'], w['w_kn'], w['w_krp'], w['w_uv_all']]
        in_specs += [pl.BlockSpec((tm, LANES), pos)] * 2 + [pl.BlockSpec(w[n].shape, full)
                                                             for n in ('w_kn', 'w_krp', 'w_uv_all')]
        add3d(COLS_A, TQ), add2d(COLS_A), add3d(H_A * (DV_A + ONES_ROWS), TK)
        add3d(COLS_QB, TQ), add2d(COLS_QB), add3d(H_B * (V_B + ONES_ROWS), TK)
    else:
        for c in (COLS_A, COLS_A, COLS_A, KV_LORA, QK_ROPE, COLS_QN, COLS_QR):
            add2d(c)
    add2d(2 * D_MODEL)
    aliases = {}
    if prev is not None:
        aliases = {len(ins) + n: n for n in range(len(prev))}
        ins += list(prev)
        in_specs += [pl.BlockSpec(memory_space=pl.ANY)] * len(prev)
    return pl.pallas_call(
        functools.partial(_inproj_kernel, prompt=prompt),
        grid=(nt,),
        in_specs=in_specs,
        out_specs=out_specs,
        out_shape=out_shape,
        input_output_aliases=aliases,
        compiler_params=_cparams(("parallel",)),
        name="inproj_prompt" if prompt else "inproj_sample",
    )(*ins)


def _attn_t_kernel(*refs, n_heads, dv, n_maps, out_scale):
    if n_maps == 2:
        lam_ref, qt_ref, k_ref, vt_ref, g_ref, o_ref, q_sc, m_sc, acc_sc, sta_sc, stb_sc, mxa_sc, mxb_sc = refs
    else:
        qt_ref, k_ref, vt_ref, o_ref, m_sc, acc_sc, sta_sc, stb_sc, mxa_sc, mxb_sc = refs
    i = pl.program_id(1)
    dva = dv + ONES_ROWS
    width = n_maps * TQ
    heads = range(n_heads)
    if n_maps == 2:
        for h in heads:
            qt = qt_ref[0, h * LANES:(h + 1) * LANES, :]
            row = lax.broadcasted_iota(jnp.int32, qt.shape, 0)
            zero = jnp.zeros_like(qt)
            q_sc[h, :, :TQ] = jnp.where(row < DH_A, qt, zero)
            q_sc[h, :, TQ:] = jnp.where(row >= DH_A, qt, zero)
    m_sc[...] = jnp.full(m_sc.shape, NEG_INF, F32)
    acc_sc[...] = jnp.zeros(acc_sc.shape, F32)
    rel = ((lax.broadcasted_iota(jnp.int32, (TK, width), 1) & (TQ - 1)) // CHUNK
           - lax.broadcasted_iota(jnp.int32, (TK, width), 0) // CHUNK)

    def scores(j, st_ref, mx_ref):
        rows = pl.ds(pl.multiple_of(j * TK, TK), TK)
        for h in heads:
            st = jnp.dot(k_ref[rows, h * LANES:(h + 1) * LANES],
                         q_sc[h] if n_maps == 2 else qt_ref[0, h * LANES:(h + 1) * LANES, :],
                         preferred_element_type=F32)
            st_ref[h] = st
            mx_ref[h] = jnp.max(st, axis=0, keepdims=True)

    def update(j, st_ref, mx_ref, masked):
        pts, alphas = [], []
        for h in heads:
            st = st_ref[h]
            if masked:
                st = jnp.where(rel >= j * (TK // CHUNK) - i * (TQ // CHUNK), st, NEG_INF)
                tile_max = jnp.max(st, axis=0, keepdims=True)
            else:
                tile_max = mx_ref[h]
            m = m_sc[h]
            m_new = jnp.maximum(m, tile_max)
            alphas.append(jnp.exp2(m - m_new))
            pts.append(jnp.exp2(st - m_new).astype(BF16))
            m_sc[h] = m_new
        pvs = [jnp.dot(vt_ref[j, h * dva:(h + 1) * dva, :], pts[h], preferred_element_type=F32) for h in heads]
        for h in heads:
            acc_sc[h] = alphas[h] * acc_sc[h] + pvs[h]

    def full_pair(p, carry):
        j = 2 * p
        scores(j + 1, stb_sc, mxb_sc)
        update(j, sta_sc, mxa_sc, False)
        scores(j + 2, sta_sc, mxa_sc)
        update(j + 1, stb_sc, mxb_sc, False)
        return carry

    assert TQ == 2 * TK
    scores(0, sta_sc, mxa_sc)
    lax.fori_loop(0, i, full_pair, 0)
    scores(2 * i + 1, stb_sc, mxb_sc)
    update(2 * i, sta_sc, mxa_sc, True)
    update(2 * i + 1, stb_sc, mxb_sc, True)

    for h in heads:
        acc = acc_sc[h]
        inv = acc[:dv] / acc[dv:dv + 1]
        if n_maps == 2:
            o = inv[:, :TQ] - lam_ref[0] * inv[:, TQ:]
            o = o * lax.rsqrt(jnp.mean(o * o, axis=0, keepdims=True) + EPS) * g_ref[...] * out_scale
        else:
            o = inv
        o_ref[0, h * dv:(h + 1) * dv, :] = o.astype(o_ref.dtype)


def _attn_t(qt, k, vt, *, batch, seq, n_heads, dv, n_maps, lam=None, g=None, out_scale=1.0, name):
    nq = seq // TQ
    dva = dv + ONES_ROWS
    width = n_maps * TQ
    once = pl.Buffered(1)
    ins, in_specs, scratch = [], [], []
    if n_maps == 2:
        ins.append(lam)
        in_specs.append(pl.BlockSpec(memory_space=pltpu.SMEM))
        scratch.append(pltpu.VMEM((n_heads, LANES, width), BF16))
    ins += [qt, k, vt]
    in_specs += [pl.BlockSpec((1, n_heads * LANES, TQ), lambda b, i: (b * nq + i, 0, 0)),
                 pl.BlockSpec((seq, n_heads * LANES), lambda b, i: (b, 0), pipeline_mode=once),
                 pl.BlockSpec((seq // TK, n_heads * dva, TK), lambda b, i: (b, 0, 0), pipeline_mode=once)]
    if n_maps == 2:
        ins.append(g)
        in_specs.append(pl.BlockSpec((dv, 1), lambda b, i: (0, 0)))
    return pl.pallas_call(
        functools.partial(_attn_t_kernel, n_heads=n_heads, dv=dv, n_maps=n_maps, out_scale=out_scale),
        grid=(batch, nq),
        in_specs=in_specs,
        out_specs=pl.BlockSpec((1, n_heads * dv, TQ), lambda b, i: (b * nq + i, 0, 0)),
        out_shape=jax.ShapeDtypeStruct((batch * nq, n_heads * dv, TQ), BF16),
        scratch_shapes=scratch + [pltpu.VMEM((n_heads, 1, width), F32), pltpu.VMEM((n_heads, dva, width), F32),
                                  pltpu.VMEM((n_heads, TK, width), F32), pltpu.VMEM((n_heads, TK, width), F32),
                                  pltpu.VMEM((n_heads, 1, width), F32), pltpu.VMEM((n_heads, 1, width), F32)],
        compiler_params=_cparams(("parallel", "arbitrary")),
        name=name,
    )(*ins)


def _flash_steps(ss, vs, m_sc, l_sc, acc_sc):
    ps, alphas = [], []
    for c, s in enumerate(ss):
        m_prev = m_sc[c]
        m_new = jnp.maximum(m_prev, jnp.max(s, axis=-1, keepdims=True))
        alpha = jnp.exp2(m_prev - m_new)
        cols = s.shape[-1]
        p = jnp.exp2(s - (_widen(m_new, cols) if cols % LANES == 0 else m_new[:, :cols]))
        l_sc[c] = alpha * l_sc[c] + jnp.sum(p, axis=-1, keepdims=True)
        m_sc[c] = m_new
        ps.append(p.astype(BF16))
        alphas.append(alpha)
    pvs = [jnp.dot(p, v, preferred_element_type=F32) for p, v in zip(ps, vs)]
    for c, pv in enumerate(pvs):
        acc_sc[c] = _widen(alphas[c], acc_sc.shape[-1]) * acc_sc[c] + pv


def _init_flash(m_sc, l_sc, acc_sc):
    m_sc[...] = jnp.full(m_sc.shape, NEG_INF, F32)
    l_sc[...] = jnp.zeros(l_sc.shape, F32)
    acc_sc[...] = jnp.zeros(acc_sc.shape, F32)


def _stack_maps(q):
    lane = lax.broadcasted_iota(jnp.int32, q.shape, 1)
    zero = jnp.zeros_like(q)
    return jnp.concatenate([jnp.where(lane < DH_A, q, zero), jnp.where(lane >= DH_A, q, zero)], axis=0)


def _diff_sample_kernel(lam_ref, q_ref, kc_ref, vc_ref, kn_ref, vn_ref, g_ref, o_ref, m_sc, l_sc, acc_sc,
                        *, tq, tk, n_cache_tiles, out_scale):
    heads = range(H_A)
    lanes = [slice(h * DV_A, (h + 1) * DV_A) for h in heads]
    qss = [_stack_maps(q_ref[:, lanes[h]]) for h in heads]
    _init_flash(m_sc, l_sc, acc_sc)

    for j in range(n_cache_tiles):
        ss = [jnp.dot(qss[h], kc_ref[lanes[h], j * tk:(j + 1) * tk].astype(BF16), preferred_element_type=F32)
              for h in heads]
        vs = [vc_ref[0, pl.ds(j * tk * H_A + h, tk, stride=H_A), :].astype(BF16) for h in heads]
        _flash_steps(ss, vs, m_sc, l_sc, acc_sc)
    _flash_steps([_dot_nt(qss[h], kn_ref[:, lanes[h]]) for h in heads], [vn_ref[:, lanes[h]] for h in heads],
                 m_sc, l_sc, acc_sc)
    for h in heads:
        inv = acc_sc[h] / l_sc[h]
        o = inv[:tq] - lam_ref[0] * inv[tq:]
        o_ref[:, lanes[h]] = (_rms(o, g_ref[...]) * out_scale).astype(o_ref.dtype)


def _diff_sample(lam, qa, kcache, vcache, kb, vb, g_subln, *, layer, batch, seq, past, out_scale):
    tk = min(512, past)
    smem = pl.BlockSpec(memory_space=pltpu.SMEM)
    new = pl.BlockSpec((seq, COLS_A), lambda b: (b, 0))
    kcache_spec = pl.BlockSpec((COLS_A, past), lambda b: (layer * batch + b, 0))
    vcache_spec = pl.BlockSpec((1, past * H_A, DV_A), lambda b: (layer * batch + b, 0, 0))
    return pl.pallas_call(
        functools.partial(_diff_sample_kernel, tq=seq, tk=tk, n_cache_tiles=past // tk, out_scale=out_scale),
        grid=(batch,),
        in_specs=[smem, new, kcache_spec, vcache_spec, new, new, pl.BlockSpec((1, DV_A), lambda b: (0, 0))],
        out_specs=new,
        out_shape=jax.ShapeDtypeStruct((batch * seq, COLS_A), BF16),
        scratch_shapes=[pltpu.VMEM((H_A, 2 * seq, LANES), F32), pltpu.VMEM((H_A, 2 * seq, LANES), F32),
                        pltpu.VMEM((H_A, 2 * seq, DV_A), F32)],
        compiler_params=_cparams(("parallel",)),
        name="diff_attn_sample",
    )(lam, qa, kcache, vcache, kb, vb, g_subln)


def _mla_sample_kernel(qn_ref, qr_ref, cc_ref, rc_ref, cn_ref, rn_ref, wuk_ref, wz_ref, o_ref,
                       ql_sc, qr_sc, m_sc, l_sc, acc_sc, *, tq, tk, n_cache_tiles):
    qn = qn_ref[...]
    qr = qr_ref[...].astype(F32)
    for h in range(H_B):
        ql = jnp.dot(qn[:, h * QK_NOPE:(h + 1) * QK_NOPE], wuk_ref[h], preferred_element_type=F32)
        ql_sc[h * tq:(h + 1) * tq, :] = ql.astype(BF16)
        qr_sc[h * tq:(h + 1) * tq, :] = qr[:, h * QK_ROPE:(h + 1) * QK_ROPE].astype(BF16)
    _init_flash(m_sc, l_sc, acc_sc)
    n_chains = m_sc.shape[0]
    half = H_B * tq // n_chains
    parts = [slice(c * half, (c + 1) * half) for c in range(n_chains)]

    for j in range(n_cache_tiles):
        c = cc_ref[0, j * tk:(j + 1) * tk, :].astype(BF16)
        rt = rc_ref[:, j * tk:(j + 1) * tk].astype(BF16)
        ss = [_dot_nt(ql_sc[p, :], c) + jnp.dot(qr_sc[p, :], rt, preferred_element_type=F32) for p in parts]
        _flash_steps(ss, [c] * n_chains, m_sc, l_sc, acc_sc)
    c = cn_ref[...]
    ss = [_dot_nt(ql_sc[p, :], c) + _dot_nt(qr_sc[p, :], rn_ref[...]) for p in parts]
    _flash_steps(ss, [c] * n_chains, m_sc, l_sc, acc_sc)
    ob = jnp.zeros(o_ref.shape, F32)
    heads_per_chain = H_B // n_chains
    for h in range(H_B):
        c, r = divmod(h, heads_per_chain)
        o_lat = (acc_sc[c, r * tq:(r + 1) * tq, :] / _widen(l_sc[c, r * tq:(r + 1) * tq, :], KV_LORA)).astype(BF16)
        ob += jnp.dot(o_lat, wz_ref[h], preferred_element_type=F32)
    o_ref[...] = ob.astype(o_ref.dtype)


def _mla_sample(qn, qr, ccache, rcache, ckvb, krb, wuk_t, wz, *, layer, batch, seq, past):
    tk = min(512, past)
    rows = H_B * seq
    return pl.pallas_call(
        functools.partial(_mla_sample_kernel, tq=seq, tk=tk, n_cache_tiles=past // tk),
        grid=(batch,),
        in_specs=[pl.BlockSpec((seq, COLS_QN), lambda b: (b, 0)),
                  pl.BlockSpec((seq, COLS_QR), lambda b: (b, 0)),
                  pl.BlockSpec((1, past, KV_LORA), lambda b: (layer * batch + b, 0, 0)),
                  pl.BlockSpec((QK_ROPE, past), lambda b: (layer * batch + b, 0)),
                  pl.BlockSpec((seq, KV_LORA), lambda b: (b, 0)),
                  pl.BlockSpec((seq, QK_ROPE), lambda b: (b, 0)),
                  pl.BlockSpec((H_B, QK_NOPE, KV_LORA), lambda b: (0, 0, 0)),
                  pl.BlockSpec((H_B, KV_LORA, COLS_VB), lambda b: (0, 0, 0))],
        out_specs=pl.BlockSpec((seq, COLS_VB), lambda b: (b, 0)),
        out_shape=jax.ShapeDtypeStruct((batch * seq, COLS_VB), BF16),
        scratch_shapes=[pltpu.VMEM((rows, KV_LORA), BF16), pltpu.VMEM((rows, QK_ROPE), BF16),
                        pltpu.VMEM((2, rows // 2, LANES), F32), pltpu.VMEM((2, rows // 2, LANES), F32),
                        pltpu.VMEM((2, rows // 2, KV_LORA), F32)],
        compiler_params=_cparams(("parallel",)),
        name="mla_attn_sample",
    )(qn, qr, ccache, rcache, ckvb, krb, wuk_t, wz)


def _merge_kernel(oa_ref, ob_ref, gate_ref, x_ref, wa_ref, wb_ref, wo_ref, gf_ref, wr_ref, br_ref, tri_ref,
                  x1_ref, h_ref, route_ref, count_ref, routet_ref, run_sc, *, tiled_t):
    oa = _load_tiles_t(oa_ref) if tiled_t else oa_ref[...]
    ob = _load_tiles_t(ob_ref) if tiled_t else ob_ref[...]
    tm = x_ref.shape[0]
    subs = [slice(r * MERGE_SUB, (r + 1) * MERGE_SUB) for r in range(tm // MERGE_SUB)]
    yas = [jnp.dot(oa[s], wa_ref[...], preferred_element_type=F32) for s in subs]
    ybs = [jnp.dot(ob[s], wb_ref[...], preferred_element_type=F32) for s in subs]
    merged = []
    for s, ya, yb in zip(subs, yas, ybs):
        gates = gate_ref[s, :].astype(F32)
        merged.append((gates[:, :D_MODEL] * ya + gates[:, D_MODEL:] * yb).astype(BF16))
    x1s = [x_ref[s, :] + jnp.dot(mg, wo_ref[...], preferred_element_type=F32) for s, mg in zip(subs, merged)]
    logit_parts = []
    for s, x1 in zip(subs, x1s):
        x1_ref[s, :] = x1
        h = _rms(x1, gf_ref[...])
        h_hi = h.astype(BF16)
        h_ref[s, :] = h_hi
        h_lo = (h - h_hi.astype(F32)).astype(BF16)
        logit_parts.append(jnp.dot(h_hi, wr_ref[0], preferred_element_type=F32)
                           + jnp.dot(h_lo, wr_ref[0], preferred_element_type=F32)
                           + jnp.dot(h_hi, wr_ref[1], preferred_element_type=F32))

    logits = jnp.concatenate(logit_parts, axis=0) + br_ref[...]
    lane = lax.broadcasted_iota(jnp.int32, logits.shape, 1)
    big = jnp.int32(ROUTE_COLS)

    def top1(mask):
        v = jnp.max(jnp.where(mask, logits, NEG_INF), axis=-1, keepdims=True)
        idx = jnp.min(jnp.where(mask & (logits == v), lane, big), axis=-1, keepdims=True)
        return v, idx

    gmask = lane < N_GROUPS
    gmax, gidx = top1(gmask)
    g_w = 1.0 / jnp.sum(jnp.where(gmask, jnp.exp(logits - gmax), 0.0), axis=-1, keepdims=True)
    lo = N_GROUPS + gidx * EXPERTS_PER_GROUP
    emask = (lane >= lo) & (lane < lo + EXPERTS_PER_GROUP)
    v1, i1 = top1(emask)
    v2, i2 = top1(emask & (lane != i1))
    e2 = jnp.exp(v2 - v1)
    w1 = g_w / (1.0 + e2)
    w2 = g_w * e2 / (1.0 + e2)
    @pl.when(pl.program_id(0) == 0)
    def _():
        run_sc[...] = jnp.zeros(run_sc.shape, F32)

    e1 = i1 - N_GROUPS
    e2i = i2 - N_GROUPS
    picks = jnp.where((lane == e1) | (lane == e2i), 1.0, 0.0)
    before = jnp.dot(tri_ref[...], picks.astype(BF16), preferred_element_type=F32) + run_sc[...]
    rank1 = jnp.sum(jnp.where(lane == e1, before, 0.0), axis=-1, keepdims=True)
    rank2 = jnp.sum(jnp.where(lane == e2i, before, 0.0), axis=-1, keepdims=True)
    run_sc[...] = run_sc[...] + jnp.sum(picks, axis=0, keepdims=True)
    count_ref[...] = run_sc[...]

    vals = [e1.astype(F32), e2i.astype(F32), w1, w2, rank1, rank2]
    route = jnp.zeros(logits.shape, F32)
    for n, v in enumerate(vals):
        route = jnp.where(lane == n, v, route)
    route_ref[...] = route
    routet_ref[...] = route.T[:routet_ref.shape[0], :]


def _merge(oa, ob, gates, x, w, *, tm, tiled_t):
    t = x.shape[0]
    tok = lambda i: (i, 0)
    full = lambda i: (0, 0)
    if tiled_t:
        o_specs = [pl.BlockSpec((tm // TQ, COLS_A, TQ), lambda i: (i, 0, 0)),
                   pl.BlockSpec((tm // TQ, COLS_VB, TQ), lambda i: (i, 0, 0))]
    else:
        o_specs = [pl.BlockSpec((tm, COLS_A), tok), pl.BlockSpec((tm, COLS_VB), tok)]
    return pl.pallas_call(
        functools.partial(_merge_kernel, tiled_t=tiled_t),
        grid=(t // tm,),
        in_specs=o_specs + [
            pl.BlockSpec((tm, 2 * D_MODEL), tok), pl.BlockSpec((tm, D_MODEL), tok),
            pl.BlockSpec((COLS_A, D_MODEL), full), pl.BlockSpec((COLS_VB, D_MODEL), full),
            pl.BlockSpec((D_MODEL, D_MODEL), full), pl.BlockSpec((1, D_MODEL), full),
            pl.BlockSpec((2, D_MODEL, ROUTE_COLS), lambda i: (0, 0, 0)), pl.BlockSpec((1, ROUTE_COLS), full),
            pl.BlockSpec((tm, tm), full)],
        out_specs=[pl.BlockSpec((tm, D_MODEL), tok), pl.BlockSpec((tm, D_MODEL), tok),
                   pl.BlockSpec((tm, ROUTE_COLS), tok), pl.BlockSpec((1, ROUTE_COLS), full),
                   pl.BlockSpec((ROUTE_ROWS, tm), lambda i: (0, i))],
        out_shape=[jax.ShapeDtypeStruct((t, D_MODEL), F32), jax.ShapeDtypeStruct((t, D_MODEL), BF16),
                   jax.ShapeDtypeStruct((t, ROUTE_COLS), F32), jax.ShapeDtypeStruct((1, ROUTE_COLS), F32),
                   jax.ShapeDtypeStruct((ROUTE_ROWS, t), F32)],
        scratch_shapes=[pltpu.VMEM((1, ROUTE_COLS), F32)],
        compiler_params=_cparams(("arbitrary",)),
        name="merge_router",
    )(oa, ob, gates, x, w['wa'], w['wb'], w['wo'], w['g_ffn'], w['w_route'], w['b_route'],
      jnp.tril(jnp.ones((tm, tm), BF16), -1))


def _expert_kernel(blk_e_ref, n_used_ref, x_ref, wgu_ref, wdn_ref, y_ref):
    i = pl.program_id(0)

    @pl.when(i < n_used_ref[0])
    def _():
        gu = jnp.dot(x_ref[...], wgu_ref[0].astype(BF16), preferred_element_type=F32)
        gate, up = gu[:, :D_EXPERT], gu[:, D_EXPERT:]
        a = (gate * jax.nn.sigmoid(gate) * up).astype(BF16)
        y_ref[...] = jnp.dot(a, wdn_ref[0].astype(BF16), preferred_element_type=F32).astype(y_ref.dtype)

    @pl.when(i >= n_used_ref[0])
    def _():
        y_ref[...] = jnp.zeros(y_ref.shape, y_ref.dtype)


def _experts(blk_e, n_used, xs, wgu, wdn, *, layer, bm):
    n_slots = xs.shape[0]
    grid_spec = pltpu.PrefetchScalarGridSpec(
        num_scalar_prefetch=2,
        grid=(n_slots // bm,),
        in_specs=[pl.BlockSpec((bm, D_MODEL), lambda i, be, nu: (i, 0)),
                  pl.BlockSpec((1, D_MODEL, 2 * D_EXPERT), lambda i, be, nu: (layer * N_EXPERTS + be[i], 0, 0)),
                  pl.BlockSpec((1, D_EXPERT, D_MODEL), lambda i, be, nu: (layer * N_EXPERTS + be[i], 0, 0))],
        out_specs=pl.BlockSpec((bm, D_MODEL), lambda i, be, nu: (i, 0)),
    )
    return pl.pallas_call(
        _expert_kernel,
        grid_spec=grid_spec,
        out_shape=jax.ShapeDtypeStruct((n_slots, D_MODEL), BF16),
        compiler_params=_cparams(("arbitrary",)),
        name="experts",
    )(blk_e, n_used, xs, wgu, wdn)


def _combine_kernel(x_ref, y0_ref, y1_ref, route_ref, g_ref, o_ref, *, final_norm):
    r = route_ref[...]
    y = x_ref[...] + r[:, 2:3] * y0_ref[...].astype(F32) + r[:, 3:4] * y1_ref[...].astype(F32)
    if final_norm:
        y = _rms(y, g_ref[...])
    o_ref[...] = y


def _combine(x1, y01, route, g_final, *, tm, final_norm):
    t = x1.shape[0]
    nt = t // tm
    tok = lambda i: (i, 0)
    return pl.pallas_call(
        functools.partial(_combine_kernel, final_norm=final_norm),
        grid=(nt,),
        in_specs=[pl.BlockSpec((tm, D_MODEL), tok), pl.BlockSpec((tm, D_MODEL), tok),
                  pl.BlockSpec((tm, D_MODEL), lambda i: (nt + i, 0)), pl.BlockSpec((tm, ROUTE_COLS), tok),
                  pl.BlockSpec((1, D_MODEL), lambda i: (0, 0))],
        out_specs=pl.BlockSpec((tm, D_MODEL), tok),
        out_shape=jax.ShapeDtypeStruct((t, D_MODEL), F32),
        compiler_params=_cparams(("parallel",)),
        name="combine",
    )(x1, y01, y01, route, g_final)


def _transpose_cast_kernel(x_ref, o_ref):
    o_ref[...] = x_ref[...].T.astype(o_ref.dtype)


def _transpose_cast(xt, block=256):
    c, d = xt.shape
    return pl.pallas_call(
        _transpose_cast_kernel,
        grid=(c // block,),
        in_specs=[pl.BlockSpec((block, d), lambda i: (i, 0))],
        out_specs=pl.BlockSpec((d, block), lambda i: (0, i)),
        out_shape=jax.ShapeDtypeStruct((d, c), BF16),
        compiler_params=_cparams(("parallel",)),
        name="weight_transpose",
    )(xt)


def _rope_tables(pos):
    lane = jnp.arange(LANES, dtype=jnp.int32)

    def tab(chunk):
        half = chunk // 2
        inv = ROPE_THETA ** (-(lane % half).astype(F32) / half)
        ang = pos.astype(F32)[:, None] * inv[None, :]
        first = (lane % chunk) < half
        return jnp.cos(ang), jnp.where(first[None, :], -jnp.sin(ang), jnp.sin(ang))

    c64, s64 = tab(DH_A)
    c32, s32 = tab(QK_ROPE)
    rope_lane = ((lane >= QK_NOPE) & (lane < QK_NOPE + QK_ROPE))[None, :]
    return dict(c64=c64, s64=s64, c32=c32, s32=s32,
                cq=jnp.where(rope_lane, c32, 1.0), sq=jnp.where(rope_lane, s32, 0.0))


def _slot_tokens_kernel(dest_ref, o_ref, *, n_tok, n_asg, n_slots):
    for base in range(0, n_slots, n_tok):
        def fill(t, carry, base=base):
            o_ref[base + t] = t
            return carry

        lax.fori_loop(0, min(n_tok, n_slots - base), fill, 0, unroll=8)

    for base in range(0, n_asg, n_tok):
        def place(t, carry, base=base):
            o_ref[dest_ref[base + t]] = t
            return carry

        lax.fori_loop(0, n_tok, place, 0, unroll=8)


def _slot_tokens(dest_flat, n_tok, n_slots):
    n_asg = dest_flat.shape[0]
    smem = pl.BlockSpec(memory_space=pltpu.SMEM)
    return pl.pallas_call(
        functools.partial(_slot_tokens_kernel, n_tok=n_tok, n_asg=n_asg, n_slots=n_slots),
        in_specs=[smem],
        out_specs=smem,
        out_shape=jax.ShapeDtypeStruct((n_slots,), jnp.int32),
        name="slot_tokens",
    )(dest_flat)


def _moe_plan(route_t, counts, bm):
    n_tok = route_t.shape[1]
    n_asg = n_tok * TOP_K
    e = route_t[:TOP_K].astype(jnp.int32)
    rank = route_t[4:4 + TOP_K].astype(jnp.int32)
    counts = counts[0, :N_EXPERTS].astype(jnp.int32)
    padded = (counts + bm - 1) // bm * bm
    pad_end = jnp.cumsum(padded)
    pad_start = pad_end - padded
    start_of = jnp.zeros_like(e)
    for j in range(N_EXPERTS):
        start_of = jnp.where(e == j, pad_start[j], start_of)
    dest = start_of + rank
    n_slots = n_asg + N_EXPERTS * bm
    n_blk = n_slots // bm
    slot_tok = _slot_tokens(dest.reshape(n_asg), n_tok, n_slots)
    blk_start = jnp.arange(n_blk, dtype=jnp.int32) * bm
    blk_e = jnp.minimum(jnp.sum((pad_end[None, :] <= blk_start[:, None]).astype(jnp.int32), axis=1),
                        N_EXPERTS - 1)
    n_used = (pad_end[-1:] // bm).astype(jnp.int32)
    return slot_tok, blk_e, n_used, dest


def _head_groups(parts):
    rows = parts[0].shape[0]
    used = sum(p.shape[-1] for p in parts)
    pad = jnp.zeros((rows, H_B, LANES - used), parts[0].dtype)
    return jnp.concatenate(list(parts) + [pad], axis=-1).reshape(rows, H_B * LANES)


def _layer_weights(l, g_attn, w_in, g_subln, w_a, g_q_lat, w_uq, g_kv_lat, w_uk, w_uv, w_b, w_o, g_ffn,
                   w_group, b_group, w_router, b_router):
    wt = jnp.swapaxes(w_in, 1, 2)[l]
    o_kr = 3 * COLS_A + Q_LORA + KV_LORA
    w_pack = _transpose_cast(jnp.concatenate(
        [wt[:o_kr], jnp.pad(wt[o_kr:o_kr + QK_ROPE], ((0, KR_PAD - QK_ROPE), (0, 0))), wt[o_kr + QK_ROPE:]],
        axis=0))
    wq = w_uq[l].reshape(Q_LORA, H_B, QK_NOPE + QK_ROPE)
    w_uq_p = jnp.concatenate([wq[:, :, :QK_NOPE].reshape(Q_LORA, COLS_QN),
                              wq[:, :, QK_NOPE:].reshape(Q_LORA, COLS_QR)], axis=1).astype(BF16)
    w_uq_g = _head_groups([wq]).astype(BF16)
    w_kn = _head_groups([w_uk[l]]).astype(BF16)
    place = jnp.pad(jnp.eye(QK_ROPE, dtype=F32), ((0, KR_PAD - QK_ROPE), (0, 0)))
    w_krp = _head_groups([jnp.zeros((KR_PAD, H_B, QK_NOPE), F32),
                          jnp.broadcast_to(place[:, None, :], (KR_PAD, H_B, QK_ROPE))]).astype(BF16)
    w_uv_all = w_uv[l].reshape(KV_LORA, COLS_VB).astype(BF16)
    wuk_t = jnp.transpose(w_uk[l], (1, 2, 0)).astype(BF16)
    wuv = jnp.transpose(w_uv[l], (1, 0, 2))
    eye = jnp.eye(H_B, dtype=F32)
    wz = (wuv[:, :, None, :] * eye[:, None, :, None]).reshape(H_B, KV_LORA, COLS_VB).astype(BF16)
    w_route = jnp.concatenate(
        [w_group[l], jnp.transpose(w_router[l], (1, 0, 2)).reshape(D_MODEL, N_EXPERTS),
         jnp.zeros((D_MODEL, ROUTE_COLS - N_GROUPS - N_EXPERTS), F32)], axis=1)
    w_route_hi = w_route.astype(BF16)
    w_route = jnp.stack([w_route_hi, (w_route - w_route_hi.astype(F32)).astype(BF16)])
    b_route = jnp.concatenate([b_group[l], b_router[l].reshape(N_EXPERTS),
                               jnp.zeros((ROUTE_COLS - N_GROUPS - N_EXPERTS,), F32)])[None, :]
    return dict(g_attn=g_attn[l][None, :], w_pack=w_pack, g_q=g_q_lat[l][None, :], g_kv=g_kv_lat[l][None, :],
                w_uq_p=w_uq_p, w_uq_g=w_uq_g, w_kn=w_kn, w_krp=w_krp, w_uv_all=w_uv_all, wuk_t=wuk_t, wz=wz,
                g_subln=g_subln[l][None, :], g_subln_col=g_subln[l][:, None],
                wa=w_a[l].astype(BF16), wb=w_b[l].astype(BF16), wo=w_o[l].astype(BF16),
                g_ffn=g_ffn[l][None, :], w_route=w_route, b_route=b_route)


def _trunk(x, pos_tile, n_pos_tiles, tm, caches, lws, lams, wgu, wdn, g_final, *, batch, seq, bm):
    depth = len(lws)
    t = batch * seq
    x = x.reshape(t, D_MODEL)
    tabs = _rope_tables(pos_tile)
    prompt = caches is None
    ks, vs, cs, rs = [], [], [], []
    new_cache = None
    for l in range(depth):
        w = lws[l]
        out_scale = 1.0 - (0.8 - 0.6 * math.exp(-0.3 * l))
        outs = _inproj(x, w, tabs, tm=tm, n_pos_tiles=n_pos_tiles, prompt=prompt, layer=l, depth=depth,
                       batch=batch, prev=new_cache)
        k, v, ckv, kr = outs[:4]
        if prompt:
            new_cache = (k, v, ckv, kr)
            qat, kb, vat, qbt, kq, vbt, gates = outs[4:]
            oa = _attn_t(qat, kb, vat, batch=batch, seq=seq, n_heads=H_A, dv=DV_A, n_maps=2, lam=lams[l],
                         g=w['g_subln_col'], out_scale=out_scale, name="diff_attn_prompt")
            ob = _attn_t(qbt, kq, vbt, batch=batch, seq=seq, n_heads=H_B, dv=V_B, n_maps=1,
                         name="mla_attn_prompt")
        else:
            qa, kb, vb, ckvb, krb, qn, qr, gates = outs[4:]
            kc, vc, cc, rc, past = caches
            oa = _diff_sample(lams[l], qa, kc, vc, kb, vb, w['g_subln'], layer=l, batch=batch, seq=seq,
                              past=past, out_scale=out_scale)
            ob = _mla_sample(qn, qr, cc, rc, ckvb, krb, w['wuk_t'], w['wz'], layer=l, batch=batch, seq=seq,
                             past=past)
        x1, h, route, counts, route_t = _merge(oa, ob, gates, x, w, tm=tm, tiled_t=prompt)
        slot_tok, blk_e, n_used, dest = _moe_plan(route_t, counts, bm)
        xs = h.at[slot_tok].get(mode="promise_in_bounds")
        y_slots = _experts(blk_e, n_used, xs, wgu, wdn, layer=l, bm=bm)
        y01 = y_slots.at[dest.reshape(TOP_K * t)].get(mode="promise_in_bounds")
        x = _combine(x1, y01, route, g_final, tm=tm, final_norm=(l == depth - 1))
        ks.append(k)
        vs.append(v)
        cs.append(ckv)
        rs.append(kr)
    y = x.reshape(batch, seq, D_MODEL)
    if prompt:
        k, v, ckv, kr = new_cache
        return (y,
                jnp.transpose(k.reshape(depth, batch, H_A, 2, DH_A, seq), (0, 1, 5, 2, 3, 4)),
                v.reshape(depth, batch, seq, H_A, DV_A),
                ckv.reshape(depth, batch, seq, KV_LORA),
                jnp.transpose(kr.reshape(depth, batch, QK_ROPE, seq), (0, 1, 3, 2)))
    return (y,
            jnp.stack(ks).reshape(depth, batch, seq, H_A, 2, DH_A),
            jnp.stack(vs).reshape(depth, batch, seq, H_A, DV_A),
            jnp.stack(cs).reshape(depth, batch, seq, KV_LORA),
            jnp.stack(rs).reshape(depth, batch, seq, QK_ROPE))


def kernel(x_prompt, x_sample, cache_diff_k, cache_diff_v, cache_mla_ckv, cache_mla_krope, g_attn, w_in,
           lambda_q1, lambda_k1, lambda_q2, lambda_k2, g_subln, w_a, g_q_lat, w_uq, g_kv_lat, w_uk, w_uv, w_b,
           w_o, g_ffn, w_group, b_group, w_router, b_router, w_gate_up, w_down, g_final):
    depth = w_in.shape[0]
    batch, seq, _ = x_prompt.shape
    dec_batch, dec_seq, _ = x_sample.shape
    past = cache_diff_k.shape[2]
    assert dec_seq == CHUNK and past % CHUNK == 0, "sample frames must form exactly one new chunk"
    assert seq % TQ == 0

    lws = [_layer_weights(l, g_attn, w_in, g_subln, w_a, g_q_lat, w_uq, g_kv_lat, w_uk, w_uv, w_b, w_o, g_ffn,
                          w_group, b_group, w_router, b_router) for l in range(depth)]
    lams = []
    for l in range(depth):
        lam_init = 0.8 - 0.6 * math.exp(-0.3 * l)
        lam = (jnp.exp(jnp.sum(lambda_q1[l] * lambda_k1[l])) - jnp.exp(jnp.sum(lambda_q2[l] * lambda_k2[l]))
               + lam_init)
        lams.append(lam.reshape(1).astype(F32))
    wgu = w_gate_up.reshape(depth * N_EXPERTS, D_MODEL, 2 * D_EXPERT)
    wdn = w_down.reshape(depth * N_EXPERTS, D_EXPERT, D_MODEL)
    g_fin = g_final[None, :]

    tm_p = min(512, seq)
    outs_p = _trunk(x_prompt, jnp.arange(seq, dtype=jnp.int32), seq // tm_p, tm_p, None, lws, lams, wgu, wdn,
                    g_fin, batch=batch, seq=seq, bm=min(512, batch * seq))

    tm_s = min(512, dec_batch * dec_seq)
    pos_tile = jnp.tile(past + jnp.arange(dec_seq, dtype=jnp.int32), tm_s // dec_seq)
    caches = (jnp.transpose(cache_diff_k, (0, 1, 3, 4, 5, 2)).reshape(depth * dec_batch * COLS_A, past),
              cache_diff_v.reshape(depth * dec_batch, past * H_A, DV_A),
              cache_mla_ckv.reshape(depth * dec_batch, past, KV_LORA),
              jnp.transpose(cache_mla_krope, (0, 1, 3, 2)).reshape(depth * dec_batch * QK_ROPE, past), past)
    outs_s = _trunk(x_sample, pos_tile, 1, tm_s, caches, lws, lams, wgu, wdn, g_fin,
                    batch=dec_batch, seq=dec_seq, bm=min(128, dec_batch * dec_seq))

    return (outs_p[0], outs_s[0]) + outs_p[1:] + outs_s[1:]
```

```python
import functools
import math

import jax
import jax.numpy as jnp
from jax import lax
from jax.experimental import pallas as pl
from jax.experimental.pallas import tpu as pltpu

D_MODEL = 1024
CHUNK = 64
ROPE_THETA = 10000.0
EPS = 1e-6
H_A = 4
DH_A = 64
DV_A = 2 * DH_A
H_B = 8
Q_LORA = 384
KV_LORA = 256
QK_NOPE = 64
QK_ROPE = 32
V_B = 64
N_GROUPS = 4
EXPERTS_PER_GROUP = 8
N_EXPERTS = N_GROUPS * EXPERTS_PER_GROUP
TOP_K = 2
D_EXPERT = 512

COLS_A = H_A * 2 * DH_A
COLS_QR = H_B * QK_ROPE
COLS_QN = H_B * QK_NOPE
COLS_VB = H_B * V_B
LANES = 128
COLS_QB = H_B * LANES
KR_PAD = LANES
PACK_COLS = 3 * COLS_A + Q_LORA + KV_LORA + KR_PAD + 2 * D_MODEL
ROUTE_COLS = LANES
ROUTE_ROWS = 8
TQ = 512
TK = 256
ONES_ROWS = 16
MERGE_SUB = 256
VMEM_LIMIT = 56 * 1024 * 1024
LOG2E = 1.4426950408889634
SCALE_A = DH_A ** -0.5 * LOG2E
SCALE_B = (QK_NOPE + QK_ROPE) ** -0.5 * LOG2E

F32 = jnp.float32
BF16 = jnp.bfloat16
NEG_INF = float("-inf")


def _cparams(sem):
    return pltpu.CompilerParams(dimension_semantics=sem, vmem_limit_bytes=VMEM_LIMIT)


def _rms(x, g):
    return x * lax.rsqrt(jnp.mean(x * x, axis=-1, keepdims=True) + EPS) * g


def _widen(tab, cols):
    reps = cols // LANES
    return tab if reps == 1 else jnp.concatenate([tab] * reps, axis=-1)


def _rope(x, cos, sin_signed, chunk):
    n = x.shape[-1]
    half = chunk // 2
    lane = lax.broadcasted_iota(jnp.int32, x.shape, 1)
    fwd = pltpu.roll(x, n - half, 1)
    bwd = pltpu.roll(x, half, 1)
    swapped = jnp.where((lane & (chunk - 1)) < half, fwd, bwd)
    return x * _widen(cos, n) + swapped * _widen(sin_signed, n)


def _dot_nt(a, b):
    return lax.dot_general(a, b, (((1,), (1,)), ((), ())), preferred_element_type=F32)


def _store_tiles_t(ref, val):
    tile = ref.shape[2]
    for r in range(ref.shape[0]):
        ref[r] = val[r * tile:(r + 1) * tile, :].T.astype(ref.dtype)


def _store_vt_ones(ref, val, n_heads, dv):
    dva = dv + ONES_ROWS
    ones = jnp.ones((ONES_ROWS, TK), ref.dtype)
    for r in range(ref.shape[0]):
        vt = val[r * TK:(r + 1) * TK, :].T.astype(ref.dtype)
        for h in range(n_heads):
            ref[r, h * dva:h * dva + dv, :] = vt[h * dv:(h + 1) * dv, :]
            ref[r, h * dva + dv:(h + 1) * dva, :] = ones


def _load_tiles_t(ref):
    return jnp.concatenate([ref[r].astype(F32).T for r in range(ref.shape[0])], axis=0).astype(BF16)


def _inproj_kernel(*refs, prompt):
    (x_ref, g_ref, w_ref, gq_ref, gkv_ref, wuq_ref, c64_ref, s64_ref, c32_ref, s32_ref) = refs[:10]
    x = x_ref[...]
    hb = _rms(x, g_ref[...]).astype(BF16)

    def proj(lo, hi):
        return jnp.dot(hb, w_ref[:, lo:hi], preferred_element_type=F32)

    c64, s64 = c64_ref[...], s64_ref[...]
    c32, s32 = c32_ref[...], s32_ref[...]
    o = 0
    qa = _rope(proj(o, o + COLS_A), c64, s64, DH_A) * SCALE_A
    o += COLS_A
    ka = _rope(proj(o, o + COLS_A), c64, s64, DH_A)
    o += COLS_A
    va = proj(o, o + COLS_A)
    o += COLS_A
    cq = _rms(proj(o, o + Q_LORA), gq_ref[...]).astype(BF16)
    o += Q_LORA
    qfull = jnp.dot(cq, wuq_ref[...], preferred_element_type=F32) * SCALE_B
    ckv = _rms(proj(o, o + KV_LORA), gkv_ref[...])
    ckvb = ckv.astype(BF16)
    o += KV_LORA
    kr_pad = _rope(proj(o, o + KR_PAD), c32, s32, QK_ROPE)
    o += KR_PAD
    gates = jax.nn.sigmoid(proj(o, o + 2 * D_MODEL)).astype(BF16)

    if prompt:
        cq_ref, sq_ref, wkn_ref, wkr_ref, wuv_ref = refs[10:15]
        (k_ref, v_ref, ckv_ref, kr_ref, qat_ref, kb_ref, vat_ref, qbt_ref, kq_ref, vbt_ref,
         gate_ref) = refs[len(refs) - 11:]
        k_ref[...] = ka.T
        for h in range(H_A):
            v_ref[pl.ds(h, va.shape[0], stride=H_A), :] = va[:, h * DV_A:(h + 1) * DV_A]
        ckv_ref[...] = ckv
        kr_ref[...] = kr_pad.T[:QK_ROPE, :]
        _store_tiles_t(qat_ref, qa)
        kb_ref[...] = ka.astype(BF16)
        _store_vt_ones(vat_ref, va, H_A, DV_A)
        _store_tiles_t(qbt_ref, _rope(qfull, cq_ref[...], sq_ref[...], QK_ROPE))
        kq = (jnp.dot(ckvb, wkn_ref[...], preferred_element_type=F32)
              + jnp.dot(kr_pad.astype(BF16), wkr_ref[...], preferred_element_type=F32))
        kq_ref[...] = kq.astype(BF16)
        _store_vt_ones(vbt_ref, jnp.dot(ckvb, wuv_ref[...], preferred_element_type=F32), H_B, V_B)
    else:
        (k_ref, v_ref, ckv_ref, kr_ref, qa_ref, kb_ref, vb_ref, ckvb_ref, krb_ref, qn_ref, qr_ref,
         gate_ref) = refs[10:]
        qa_ref[...] = qa.astype(BF16)
        kb_ref[...] = ka.astype(BF16)
        vb_ref[...] = va.astype(BF16)
        ckvb_ref[...] = ckvb
        krb_ref[...] = kr_pad[:, :QK_ROPE].astype(BF16)
        qn_ref[...] = qfull[:, :COLS_QN].astype(BF16)
        qr_ref[...] = _rope(qfull[:, COLS_QN:], c32, s32, QK_ROPE).astype(BF16)
        k_ref[...] = ka
        v_ref[...] = va
        ckv_ref[...] = ckv
        kr_ref[...] = kr_pad[:, :QK_ROPE]
    gate_ref[...] = gates


def _inproj(x, w, tabs, *, tm, n_pos_tiles, prompt, layer=0, depth=1, batch=1, prev=None):
    t = x.shape[0]
    nt = t // tm
    tok = lambda i: (i, 0)
    full = lambda i: (0, 0)
    pos = lambda i: (i % n_pos_tiles, 0)
    tile3 = lambda i: (i, 0, 0)
    wuq = w['w_uq_g'] if prompt else w['w_uq_p']
    ins = [x, w['g_attn'], w['w_pack'], w['g_q'], w['g_kv'], wuq, tabs['c64'], tabs['s64'], tabs['c32'], tabs['s32']]
    in_specs = [pl.BlockSpec((tm, D_MODEL), tok), pl.BlockSpec((1, D_MODEL), full),
                pl.BlockSpec((D_MODEL, PACK_COLS), full), pl.BlockSpec((1, Q_LORA), full),
                pl.BlockSpec((1, KV_LORA), full), pl.BlockSpec(wuq.shape, full)] + [pl.BlockSpec((tm, LANES), pos)] * 4
    if prompt:
        seq = t // batch
        per_b = seq // tm
        seq_minor = lambda i: (layer * batch + i // per_b, i % per_b)
        tok_l = lambda i: (layer * nt + i, 0)
        out_specs = [pl.BlockSpec((COLS_A, tm), seq_minor), pl.BlockSpec((tm * H_A, DV_A), tok_l),
                     pl.BlockSpec((tm, KV_LORA), tok_l), pl.BlockSpec((QK_ROPE, tm), seq_minor)]
        out_shape = [jax.ShapeDtypeStruct((depth * batch * COLS_A, seq), F32),
                     jax.ShapeDtypeStruct((depth * t * H_A, DV_A), F32),
                     jax.ShapeDtypeStruct((depth * t, KV_LORA), F32),
                     jax.ShapeDtypeStruct((depth * batch * QK_ROPE, seq), F32)]
    else:
        leaves = [(COLS_A, F32), (COLS_A, F32), (KV_LORA, F32), (QK_ROPE, F32)]
        out_specs = [pl.BlockSpec((tm, c), tok) for c, _ in leaves]
        out_shape = [jax.ShapeDtypeStruct((t, c), d) for c, d in leaves]

    def add2d(c):
        out_specs.append(pl.BlockSpec((tm, c), tok))
        out_shape.append(jax.ShapeDtypeStruct((t, c), BF16))

    def add3d(c, tile):
        out_specs.append(pl.BlockSpec((tm // tile, c, tile), tile3))
        out_shape.append(jax.ShapeDtypeStruct((t // tile, c, tile), BF16))

    if prompt:
        ins += [tabs['cq'], tabs['sq'], w['w_kn'], w['w_krp'], w['w_uv_all']]
        in_specs += [pl.BlockSpec((tm, LANES), pos)] * 2 + [pl.BlockSpec(w[n].shape, full)
                                                             for n in ('w_kn', 'w_krp', 'w_uv_all')]
        add3d(COLS_A, TQ), add2d(COLS_A), add3d(H_A * (DV_A + ONES_ROWS), TK)
        add3d(COLS_QB, TQ), add2d(COLS_QB), add3d(H_B * (V_B + ONES_ROWS), TK)
    else:
        for c in (COLS_A, COLS_A, COLS_A, KV_LORA, QK_ROPE, COLS_QN, COLS_QR):
            add2d(c)
    add2d(2 * D_MODEL)
    aliases = {}
    if prev is not None:
        aliases = {len(ins) + n: n for n in range(len(prev))}
        ins += list(prev)
        in_specs += [pl.BlockSpec(memory_space=pl.ANY)] * len(prev)
    return pl.pallas_call(
        functools.partial(_inproj_kernel, prompt=prompt),
        grid=(nt,),
        in_specs=in_specs,
        out_specs=out_specs,
        out_shape=out_shape,
        input_output_aliases=aliases,
        compiler_params=_cparams(("parallel",)),
        name="inproj_prompt" if prompt else "inproj_sample",
    )(*ins)


def _attn_t_kernel(*refs, n_heads, dv, n_maps, out_scale):
    if n_maps == 2:
        lam_ref, qt_ref, k_ref, vt_ref, g_ref, o_ref, q_sc, m_sc, acc_sc, sta_sc, stb_sc, mxa_sc, mxb_sc = refs
    else:
        qt_ref, k_ref, vt_ref, o_ref, m_sc, acc_sc, sta_sc, stb_sc, mxa_sc, mxb_sc = refs
    i = pl.program_id(1)
    dva = dv + ONES_ROWS
    width = n_maps * TQ
    heads = range(n_heads)
    if n_maps == 2:
        for h in heads:
            qt = qt_ref[0, h * LANES:(h + 1) * LANES, :]
            row = lax.broadcasted_iota(jnp.int32, qt.shape, 0)
            zero = jnp.zeros_like(qt)
            q_sc[h, :, :TQ] = jnp.where(row < DH_A, qt, zero)
            q_sc[h, :, TQ:] = jnp.where(row >= DH_A, qt, zero)
    m_sc[...] = jnp.full(m_sc.shape, NEG_INF, F32)
    acc_sc[...] = jnp.zeros(acc_sc.shape, F32)
    rel = ((lax.broadcasted_iota(jnp.int32, (TK, width), 1) & (TQ - 1)) // CHUNK
           - lax.broadcasted_iota(jnp.int32, (TK, width), 0) // CHUNK)

    def score_head(j, h, st_ref, mx_ref):
        rows = pl.ds(pl.multiple_of(j * TK, TK), TK)
        st = jnp.dot(k_ref[rows, h * LANES:(h + 1) * LANES],
                     q_sc[h] if n_maps == 2 else qt_ref[0, h * LANES:(h + 1) * LANES, :],
                     preferred_element_type=F32)
        st_ref[h] = st
        mx_ref[h] = jnp.max(st, axis=0, keepdims=True)

    def softmax_head(j, h, st_ref, mx_ref, masked):
        st = st_ref[h]
        if masked:
            st = jnp.where(rel >= j * (TK // CHUNK) - i * (TQ // CHUNK), st, NEG_INF)
            tile_max = jnp.max(st, axis=0, keepdims=True)
        else:
            tile_max = mx_ref[h]
        m = m_sc[h]
        m_new = jnp.maximum(m, tile_max)
        m_sc[h] = m_new
        return jnp.exp2(st - m_new).astype(BF16), jnp.exp2(m - m_new)

    def value_head(j, h, pt, alpha):
        pv = jnp.dot(vt_ref[j, h * dva:(h + 1) * dva, :], pt, preferred_element_type=F32)
        acc_sc[h] = alpha * acc_sc[h] + pv

    def stage(j, cur, masked, nxt=None):
        pending = None
        for h in heads:
            if nxt is not None:
                score_head(j + 1, h, *nxt)
            pt, alpha = softmax_head(j, h, *cur, masked)
            if pending is not None:
                value_head(j, *pending)
            pending = (h, pt, alpha)
        value_head(j, *pending)

    buf_a, buf_b = (sta_sc, mxa_sc), (stb_sc, mxb_sc)

    def full_pair(p, carry):
        j = 2 * p
        stage(j, buf_a, False, buf_b)
        stage(j + 1, buf_b, False, buf_a)
        return carry

    assert TQ == 2 * TK
    for h in heads:
        score_head(0, h, *buf_a)
    lax.fori_loop(0, i, full_pair, 0)
    stage(2 * i, buf_a, True, buf_b)
    stage(2 * i + 1, buf_b, True)


    for h in heads:
        acc = acc_sc[h]
        inv = acc[:dv] / acc[dv:dv + 1]
        if n_maps == 2:
            o = inv[:, :TQ] - lam_ref[0] * inv[:, TQ:]
            o = o * lax.rsqrt(jnp.mean(o * o, axis=0, keepdims=True) + EPS) * g_ref[...] * out_scale
        else:
            o = inv
        o_ref[0, h * dv:(h + 1) * dv, :] = o.astype(o_ref.dtype)


def _attn_t(qt, k, vt, *, batch, seq, n_heads, dv, n_maps, lam=None, g=None, out_scale=1.0, name):
    nq = seq // TQ
    dva = dv + ONES_ROWS
    width = n_maps * TQ
    once = pl.Buffered(1)
    ins, in_specs, scratch = [], [], []
    if n_maps == 2:
        ins.append(lam)
        in_specs.append(pl.BlockSpec(memory_space=pltpu.SMEM))
        scratch.append(pltpu.VMEM((n_heads, LANES, width), BF16))
    ins += [qt, k, vt]
    in_specs += [pl.BlockSpec((1, n_heads * LANES, TQ), lambda b, i: (b * nq + i, 0, 0)),
                 pl.BlockSpec((seq, n_heads * LANES), lambda b, i: (b, 0), pipeline_mode=once),
                 pl.BlockSpec((seq // TK, n_heads * dva, TK), lambda b, i: (b, 0, 0), pipeline_mode=once)]
    if n_maps == 2:
        ins.append(g)
        in_specs.append(pl.BlockSpec((dv, 1), lambda b, i: (0, 0)))
    return pl.pallas_call(
        functools.partial(_attn_t_kernel, n_heads=n_heads, dv=dv, n_maps=n_maps, out_scale=out_scale),
        grid=(batch, nq),
        in_specs=in_specs,
        out_specs=pl.BlockSpec((1, n_heads * dv, TQ), lambda b, i: (b * nq + i, 0, 0)),
        out_shape=jax.ShapeDtypeStruct((batch * nq, n_heads * dv, TQ), BF16),
        scratch_shapes=scratch + [pltpu.VMEM((n_heads, 1, width), F32), pltpu.VMEM((n_heads, dva, width), F32),
                                  pltpu.VMEM((n_heads, TK, width), F32), pltpu.VMEM((n_heads, TK, width), F32),
                                  pltpu.VMEM((n_heads, 1, width), F32), pltpu.VMEM((n_heads, 1, width), F32)],
        compiler_params=_cparams(("parallel", "arbitrary")),
        name=name,
    )(*ins)


def _flash_steps(ss, vs, m_sc, l_sc, acc_sc):
    ps, alphas = [], []
    for c, s in enumerate(ss):
        m_prev = m_sc[c]
        m_new = jnp.maximum(m_prev, jnp.max(s, axis=-1, keepdims=True))
        alpha = jnp.exp2(m_prev - m_new)
        cols = s.shape[-1]
        p = jnp.exp2(s - (_widen(m_new, cols) if cols % LANES == 0 else m_new[:, :cols]))
        l_sc[c] = alpha * l_sc[c] + jnp.sum(p, axis=-1, keepdims=True)
        m_sc[c] = m_new
        ps.append(p.astype(BF16))
        alphas.append(alpha)
    pvs = [jnp.dot(p, v, preferred_element_type=F32) for p, v in zip(ps, vs)]
    for c, pv in enumerate(pvs):
        acc_sc[c] = _widen(alphas[c], acc_sc.shape[-1]) * acc_sc[c] + pv


def _init_flash(m_sc, l_sc, acc_sc):
    m_sc[...] = jnp.full(m_sc.shape, NEG_INF, F32)
    l_sc[...] = jnp.zeros(l_sc.shape, F32)
    acc_sc[...] = jnp.zeros(acc_sc.shape, F32)


def _stack_maps(q):
    lane = lax.broadcasted_iota(jnp.int32, q.shape, 1)
    zero = jnp.zeros_like(q)
    return jnp.concatenate([jnp.where(lane < DH_A, q, zero), jnp.where(lane >= DH_A, q, zero)], axis=0)


def _diff_sample_kernel(lam_ref, q_ref, kc_ref, vc_ref, kn_ref, vn_ref, g_ref, o_ref, m_sc, l_sc, acc_sc,
                        *, tq, tk, n_cache_tiles, out_scale):
    heads = range(H_A)
    lanes = [slice(h * DV_A, (h + 1) * DV_A) for h in heads]
    qss = [_stack_maps(q_ref[:, lanes[h]]) for h in heads]
    _init_flash(m_sc, l_sc, acc_sc)

    def cache_scores(j):
        return [jnp.dot(qss[h], kc_ref[lanes[h], j * tk:(j + 1) * tk].astype(BF16), preferred_element_type=F32)
                for h in heads]

    ss = cache_scores(0)
    for j in range(n_cache_tiles):
        ss_next = (cache_scores(j + 1) if j + 1 < n_cache_tiles
                   else [_dot_nt(qss[h], kn_ref[:, lanes[h]]) for h in heads])
        vs = [vc_ref[0, pl.ds(j * tk * H_A + h, tk, stride=H_A), :].astype(BF16) for h in heads]
        _flash_steps(ss, vs, m_sc, l_sc, acc_sc)
        ss = ss_next
    _flash_steps(ss, [vn_ref[:, lanes[h]] for h in heads], m_sc, l_sc, acc_sc)
    for h in heads:
        inv = acc_sc[h] / l_sc[h]
        o = inv[:tq] - lam_ref[0] * inv[tq:]
        o_ref[:, lanes[h]] = (_rms(o, g_ref[...]) * out_scale).astype(o_ref.dtype)


def _diff_sample(lam, qa, kcache, vcache, kb, vb, g_subln, *, layer, batch, seq, past, out_scale):
    tk = min(512, past)
    smem = pl.BlockSpec(memory_space=pltpu.SMEM)
    new = pl.BlockSpec((seq, COLS_A), lambda b: (b, 0))
    kcache_spec = pl.BlockSpec((COLS_A, past), lambda b: (layer * batch + b, 0))
    vcache_spec = pl.BlockSpec((1, past * H_A, DV_A), lambda b: (layer * batch + b, 0, 0))
    return pl.pallas_call(
        functools.partial(_diff_sample_kernel, tq=seq, tk=tk, n_cache_tiles=past // tk, out_scale=out_scale),
        grid=(batch,),
        in_specs=[smem, new, kcache_spec, vcache_spec, new, new, pl.BlockSpec((1, DV_A), lambda b: (0, 0))],
        out_specs=new,
        out_shape=jax.ShapeDtypeStruct((batch * seq, COLS_A), BF16),
        scratch_shapes=[pltpu.VMEM((H_A, 2 * seq, LANES), F32), pltpu.VMEM((H_A, 2 * seq, LANES), F32),
                        pltpu.VMEM((H_A, 2 * seq, DV_A), F32)],
        compiler_params=_cparams(("parallel",)),
        name="diff_attn_sample",
    )(lam, qa, kcache, vcache, kb, vb, g_subln)


def _mla_sample_kernel(qn_ref, qr_ref, cc_ref, rc_ref, cn_ref, rn_ref, wuk_ref, wz_ref, o_ref,
                       ql_sc, qr_sc, m_sc, l_sc, acc_sc, *, tq, tk, n_cache_tiles):
    qn = qn_ref[...]
    qr = qr_ref[...].astype(F32)
    for h in range(H_B):
        ql = jnp.dot(qn[:, h * QK_NOPE:(h + 1) * QK_NOPE], wuk_ref[h], preferred_element_type=F32)
        ql_sc[h * tq:(h + 1) * tq, :] = ql.astype(BF16)
        qr_sc[h * tq:(h + 1) * tq, :] = qr[:, h * QK_ROPE:(h + 1) * QK_ROPE].astype(BF16)
    _init_flash(m_sc, l_sc, acc_sc)
    n_chains = m_sc.shape[0]
    half = H_B * tq // n_chains
    parts = [slice(c * half, (c + 1) * half) for c in range(n_chains)]

    def cache_tile(j):
        c = cc_ref[0, j * tk:(j + 1) * tk, :].astype(BF16)
        rt = rc_ref[:, j * tk:(j + 1) * tk].astype(BF16)
        return c, [_dot_nt(ql_sc[p, :], c) + jnp.dot(qr_sc[p, :], rt, preferred_element_type=F32) for p in parts]

    def new_tile():
        c = cn_ref[...]
        return c, [_dot_nt(ql_sc[p, :], c) + _dot_nt(qr_sc[p, :], rn_ref[...]) for p in parts]

    c, ss = cache_tile(0)
    for j in range(n_cache_tiles):
        c_next, ss_next = cache_tile(j + 1) if j + 1 < n_cache_tiles else new_tile()
        _flash_steps(ss, [c] * n_chains, m_sc, l_sc, acc_sc)
        c, ss = c_next, ss_next
    _flash_steps(ss, [c] * n_chains, m_sc, l_sc, acc_sc)
    ob = jnp.zeros(o_ref.shape, F32)
    heads_per_chain = H_B // n_chains
    for h in range(H_B):
        c, r = divmod(h, heads_per_chain)
        o_lat = (acc_sc[c, r * tq:(r + 1) * tq, :] / _widen(l_sc[c, r * tq:(r + 1) * tq, :], KV_LORA)).astype(BF16)
        ob += jnp.dot(o_lat, wz_ref[h], preferred_element_type=F32)
    o_ref[...] = ob.astype(o_ref.dtype)


def _mla_sample(qn, qr, ccache, rcache, ckvb, krb, wuk_t, wz, *, layer, batch, seq, past):
    tk = min(512, past)
    rows = H_B * seq
    return pl.pallas_call(
        functools.partial(_mla_sample_kernel, tq=seq, tk=tk, n_cache_tiles=past // tk),
        grid=(batch,),
        in_specs=[pl.BlockSpec((seq, COLS_QN), lambda b: (b, 0)),
                  pl.BlockSpec((seq, COLS_QR), lambda b: (b, 0)),
                  pl.BlockSpec((1, past, KV_LORA), lambda b: (layer * batch + b, 0, 0)),
                  pl.BlockSpec((QK_ROPE, past), lambda b: (layer * batch + b, 0)),
                  pl.BlockSpec((seq, KV_LORA), lambda b: (b, 0)),
                  pl.BlockSpec((seq, QK_ROPE), lambda b: (b, 0)),
                  pl.BlockSpec((H_B, QK_NOPE, KV_LORA), lambda b: (0, 0, 0)),
                  pl.BlockSpec((H_B, KV_LORA, COLS_VB), lambda b: (0, 0, 0))],
        out_specs=pl.BlockSpec((seq, COLS_VB), lambda b: (b, 0)),
        out_shape=jax.ShapeDtypeStruct((batch * seq, COLS_VB), BF16),
        scratch_shapes=[pltpu.VMEM((rows, KV_LORA), BF16), pltpu.VMEM((rows, QK_ROPE), BF16),
                        pltpu.VMEM((2, rows // 2, LANES), F32), pltpu.VMEM((2, rows // 2, LANES), F32),
                        pltpu.VMEM((2, rows // 2, KV_LORA), F32)],
        compiler_params=_cparams(("parallel",)),
        name="mla_attn_sample",
    )(qn, qr, ccache, rcache, ckvb, krb, wuk_t, wz)


def _merge_kernel(oa_ref, ob_ref, gate_ref, x_ref, wa_ref, wb_ref, wo_ref, gf_ref, wr_ref, br_ref, tri_ref,
                  x1_ref, h_ref, route_ref, count_ref, routet_ref, run_sc, *, tiled_t):
    oa = _load_tiles_t(oa_ref) if tiled_t else oa_ref[...]
    ob = _load_tiles_t(ob_ref) if tiled_t else ob_ref[...]
    tm = x_ref.shape[0]
    subs = [slice(r * MERGE_SUB, (r + 1) * MERGE_SUB) for r in range(tm // MERGE_SUB)]
    yas = [jnp.dot(oa[s], wa_ref[...], preferred_element_type=F32) for s in subs]
    ybs = [jnp.dot(ob[s], wb_ref[...], preferred_element_type=F32) for s in subs]
    merged = []
    for s, ya, yb in zip(subs, yas, ybs):
        gates = gate_ref[s, :].astype(F32)
        merged.append((gates[:, :D_MODEL] * ya + gates[:, D_MODEL:] * yb).astype(BF16))
    x1s = [x_ref[s, :] + jnp.dot(mg, wo_ref[...], preferred_element_type=F32) for s, mg in zip(subs, merged)]
    logit_parts = []
    for s, x1 in zip(subs, x1s):
        x1_ref[s, :] = x1
        h = _rms(x1, gf_ref[...])
        h_hi = h.astype(BF16)
        h_ref[s, :] = h_hi
        h_lo = (h - h_hi.astype(F32)).astype(BF16)
        logit_parts.append(jnp.dot(h_hi, wr_ref[0], preferred_element_type=F32)
                           + jnp.dot(h_lo, wr_ref[0], preferred_element_type=F32)
                           + jnp.dot(h_hi, wr_ref[1], preferred_element_type=F32))

    logits = jnp.concatenate(logit_parts, axis=0) + br_ref[...]
    lane = lax.broadcasted_iota(jnp.int32, logits.shape, 1)
    big = jnp.int32(ROUTE_COLS)

    def top1(mask):
        v = jnp.max(jnp.where(mask, logits, NEG_INF), axis=-1, keepdims=True)
        idx = jnp.min(jnp.where(mask & (logits == v), lane, big), axis=-1, keepdims=True)
        return v, idx

    gmask = lane < N_GROUPS
    gmax, gidx = top1(gmask)
    g_w = 1.0 / jnp.sum(jnp.where(gmask, jnp.exp(logits - gmax), 0.0), axis=-1, keepdims=True)
    lo = N_GROUPS + gidx * EXPERTS_PER_GROUP
    emask = (lane >= lo) & (lane < lo + EXPERTS_PER_GROUP)
    v1, i1 = top1(emask)
    v2, i2 = top1(emask & (lane != i1))
    e2 = jnp.exp(v2 - v1)
    w1 = g_w / (1.0 + e2)
    w2 = g_w * e2 / (1.0 + e2)
    @pl.when(pl.program_id(0) == 0)
    def _():
        run_sc[...] = jnp.zeros(run_sc.shape, F32)

    e1 = i1 - N_GROUPS
    e2i = i2 - N_GROUPS
    picks = jnp.where((lane == e1) | (lane == e2i), 1.0, 0.0)
    before = jnp.dot(tri_ref[...], picks.astype(BF16), preferred_element_type=F32) + run_sc[...]
    rank1 = jnp.sum(jnp.where(lane == e1, before, 0.0), axis=-1, keepdims=True)
    rank2 = jnp.sum(jnp.where(lane == e2i, before, 0.0), axis=-1, keepdims=True)
    run_sc[...] = run_sc[...] + jnp.sum(picks, axis=0, keepdims=True)
    count_ref[...] = run_sc[...]

    vals = [e1.astype(F32), e2i.astype(F32), w1, w2, rank1, rank2]
    route = jnp.zeros(logits.shape, F32)
    for n, v in enumerate(vals):
        route = jnp.where(lane == n, v, route)
    route_ref[...] = route
    routet_ref[...] = route.T[:routet_ref.shape[0], :]


def _merge(oa, ob, gates, x, w, *, tm, tiled_t):
    t = x.shape[0]
    tok = lambda i: (i, 0)
    full = lambda i: (0, 0)
    if tiled_t:
        o_specs = [pl.BlockSpec((tm // TQ, COLS_A, TQ), lambda i: (i, 0, 0)),
                   pl.BlockSpec((tm // TQ, COLS_VB, TQ), lambda i: (i, 0, 0))]
    else:
        o_specs = [pl.BlockSpec((tm, COLS_A), tok), pl.BlockSpec((tm, COLS_VB), tok)]
    return pl.pallas_call(
        functools.partial(_merge_kernel, tiled_t=tiled_t),
        grid=(t // tm,),
        in_specs=o_specs + [
            pl.BlockSpec((tm, 2 * D_MODEL), tok), pl.BlockSpec((tm, D_MODEL), tok),
            pl.BlockSpec((COLS_A, D_MODEL), full), pl.BlockSpec((COLS_VB, D_MODEL), full),
            pl.BlockSpec((D_MODEL, D_MODEL), full), pl.BlockSpec((1, D_MODEL), full),
            pl.BlockSpec((2, D_MODEL, ROUTE_COLS), lambda i: (0, 0, 0)), pl.BlockSpec((1, ROUTE_COLS), full),
            pl.BlockSpec((tm, tm), full)],
        out_specs=[pl.BlockSpec((tm, D_MODEL), tok), pl.BlockSpec((tm, D_MODEL), tok),
                   pl.BlockSpec((tm, ROUTE_COLS), tok), pl.BlockSpec((1, ROUTE_COLS), full),
                   pl.BlockSpec((ROUTE_ROWS, tm), lambda i: (0, i))],
        out_shape=[jax.ShapeDtypeStruct((t, D_MODEL), F32), jax.ShapeDtypeStruct((t, D_MODEL), BF16),
                   jax.ShapeDtypeStruct((t, ROUTE_COLS), F32), jax.ShapeDtypeStruct((1, ROUTE_COLS), F32),
                   jax.ShapeDtypeStruct((ROUTE_ROWS, t), F32)],
        scratch_shapes=[pltpu.VMEM((1, ROUTE_COLS), F32)],
        compiler_params=_cparams(("arbitrary",)),
        name="merge_router",
    )(oa, ob, gates, x, w['wa'], w['wb'], w['wo'], w['g_ffn'], w['w_route'], w['b_route'],
      jnp.tril(jnp.ones((tm, tm), BF16), -1))


def _expert_kernel(blk_e_ref, n_used_ref, x_ref, wgu_ref, wdn_ref, y_ref):
    i = pl.program_id(0)

    @pl.when(i < n_used_ref[0])
    def _():
        gu = jnp.dot(x_ref[...], wgu_ref[0].astype(BF16), preferred_element_type=F32)
        gate, up = gu[:, :D_EXPERT], gu[:, D_EXPERT:]
        a = (gate * jax.nn.sigmoid(gate) * up).astype(BF16)
        y_ref[...] = jnp.dot(a, wdn_ref[0].astype(BF16), preferred_element_type=F32).astype(y_ref.dtype)

    @pl.when(i >= n_used_ref[0])
    def _():
        y_ref[...] = jnp.zeros(y_ref.shape, y_ref.dtype)


def _experts(blk_e, n_used, xs, wgu, wdn, *, layer, bm):
    n_slots = xs.shape[0]
    grid_spec = pltpu.PrefetchScalarGridSpec(
        num_scalar_prefetch=2,
        grid=(n_slots // bm,),
        in_specs=[pl.BlockSpec((bm, D_MODEL), lambda i, be, nu: (i, 0)),
                  pl.BlockSpec((1, D_MODEL, 2 * D_EXPERT), lambda i, be, nu: (layer * N_EXPERTS + be[i], 0, 0)),
                  pl.BlockSpec((1, D_EXPERT, D_MODEL), lambda i, be, nu: (layer * N_EXPERTS + be[i], 0, 0))],
        out_specs=pl.BlockSpec((bm, D_MODEL), lambda i, be, nu: (i, 0)),
    )
    return pl.pallas_call(
        _expert_kernel,
        grid_spec=grid_spec,
        out_shape=jax.ShapeDtypeStruct((n_slots, D_MODEL), BF16),
        compiler_params=_cparams(("arbitrary",)),
        name="experts",
    )(blk_e, n_used, xs, wgu, wdn)


def _combine_kernel(x_ref, y0_ref, y1_ref, route_ref, g_ref, o_ref, *, final_norm):
    r = route_ref[...]
    y = x_ref[...] + r[:, 2:3] * y0_ref[...].astype(F32) + r[:, 3:4] * y1_ref[...].astype(F32)
    if final_norm:
        y = _rms(y, g_ref[...])
    o_ref[...] = y


def _combine(x1, y01, route, g_final, *, tm, final_norm):
    t = x1.shape[0]
    nt = t // tm
    tok = lambda i: (i, 0)
    return pl.pallas_call(
        functools.partial(_combine_kernel, final_norm=final_norm),
        grid=(nt,),
        in_specs=[pl.BlockSpec((tm, D_MODEL), tok), pl.BlockSpec((tm, D_MODEL), tok),
                  pl.BlockSpec((tm, D_MODEL), lambda i: (nt + i, 0)), pl.BlockSpec((tm, ROUTE_COLS), tok),
                  pl.BlockSpec((1, D_MODEL), lambda i: (0, 0))],
        out_specs=pl.BlockSpec((tm, D_MODEL), tok),
        out_shape=jax.ShapeDtypeStruct((t, D_MODEL), F32),
        compiler_params=_cparams(("parallel",)),
        name="combine",
    )(x1, y01, y01, route, g_final)


def _transpose_cast_kernel(x_ref, o_ref):
    o_ref[...] = x_ref[...].T.astype(o_ref.dtype)


def _transpose_cast(xt, block=256):
    c, d = xt.shape
    return pl.pallas_call(
        _transpose_cast_kernel,
        grid=(c // block,),
        in_specs=[pl.BlockSpec((block, d), lambda i: (i, 0))],
        out_specs=pl.BlockSpec((d, block), lambda i: (0, i)),
        out_shape=jax.ShapeDtypeStruct((d, c), BF16),
        compiler_params=_cparams(("parallel",)),
        name="weight_transpose",
    )(xt)


def _rope_tables(pos):
    lane = jnp.arange(LANES, dtype=jnp.int32)

    def tab(chunk):
        half = chunk // 2
        inv = ROPE_THETA ** (-(lane % half).astype(F32) / half)
        ang = pos.astype(F32)[:, None] * inv[None, :]
        first = (lane % chunk) < half
        return jnp.cos(ang), jnp.where(first[None, :], -jnp.sin(ang), jnp.sin(ang))

    c64, s64 = tab(DH_A)
    c32, s32 = tab(QK_ROPE)
    rope_lane = ((lane >= QK_NOPE) & (lane < QK_NOPE + QK_ROPE))[None, :]
    return dict(c64=c64, s64=s64, c32=c32, s32=s32,
                cq=jnp.where(rope_lane, c32, 1.0), sq=jnp.where(rope_lane, s32, 0.0))


def _slot_tokens_kernel(dest_ref, o_ref, *, n_tok, n_asg, n_slots):
    for base in range(0, n_slots, n_tok):
        def fill(t, carry, base=base):
            o_ref[base + t] = t
            return carry

        lax.fori_loop(0, min(n_tok, n_slots - base), fill, 0, unroll=8)

    for base in range(0, n_asg, n_tok):
        def place(t, carry, base=base):
            o_ref[dest_ref[base + t]] = t
            return carry

        lax.fori_loop(0, n_tok, place, 0, unroll=8)


def _slot_tokens(dest_flat, n_tok, n_slots):
    n_asg = dest_flat.shape[0]
    smem = pl.BlockSpec(memory_space=pltpu.SMEM)
    return pl.pallas_call(
        functools.partial(_slot_tokens_kernel, n_tok=n_tok, n_asg=n_asg, n_slots=n_slots),
        in_specs=[smem],
        out_specs=smem,
        out_shape=jax.ShapeDtypeStruct((n_slots,), jnp.int32),
        name="slot_tokens",
    )(dest_flat)


def _moe_plan(route_t, counts, bm):
    n_tok = route_t.shape[1]
    n_asg = n_tok * TOP_K
    e = route_t[:TOP_K].astype(jnp.int32)
    rank = route_t[4:4 + TOP_K].astype(jnp.int32)
    counts = counts[0, :N_EXPERTS].astype(jnp.int32)
    padded = (counts + bm - 1) // bm * bm
    pad_end = jnp.cumsum(padded)
    pad_start = pad_end - padded
    start_of = jnp.zeros_like(e)
    for j in range(N_EXPERTS):
        start_of = jnp.where(e == j, pad_start[j], start_of)
    dest = start_of + rank
    n_slots = n_asg + N_EXPERTS * bm
    n_blk = n_slots // bm
    slot_tok = _slot_tokens(dest.reshape(n_asg), n_tok, n_slots)
    blk_start = jnp.arange(n_blk, dtype=jnp.int32) * bm
    blk_e = jnp.minimum(jnp.sum((pad_end[None, :] <= blk_start[:, None]).astype(jnp.int32), axis=1),
                        N_EXPERTS - 1)
    n_used = (pad_end[-1:] // bm).astype(jnp.int32)
    return slot_tok, blk_e, n_used, dest


def _head_groups(parts):
    rows = parts[0].shape[0]
    used = sum(p.shape[-1] for p in parts)
    pad = jnp.zeros((rows, H_B, LANES - used), parts[0].dtype)
    return jnp.concatenate(list(parts) + [pad], axis=-1).reshape(rows, H_B * LANES)


def _layer_weights(l, g_attn, w_in, g_subln, w_a, g_q_lat, w_uq, g_kv_lat, w_uk, w_uv, w_b, w_o, g_ffn,
                   w_group, b_group, w_router, b_router):
    wt = jnp.swapaxes(w_in, 1, 2)[l]
    o_kr = 3 * COLS_A + Q_LORA + KV_LORA
    w_pack = _transpose_cast(jnp.concatenate(
        [wt[:o_kr], jnp.pad(wt[o_kr:o_kr + QK_ROPE], ((0, KR_PAD - QK_ROPE), (0, 0))), wt[o_kr + QK_ROPE:]],
        axis=0))
    wq = w_uq[l].reshape(Q_LORA, H_B, QK_NOPE + QK_ROPE)
    w_uq_p = jnp.concatenate([wq[:, :, :QK_NOPE].reshape(Q_LORA, COLS_QN),
                              wq[:, :, QK_NOPE:].reshape(Q_LORA, COLS_QR)], axis=1).astype(BF16)
    w_uq_g = _head_groups([wq]).astype(BF16)
    w_kn = _head_groups([w_uk[l]]).astype(BF16)
    place = jnp.pad(jnp.eye(QK_ROPE, dtype=F32), ((0, KR_PAD - QK_ROPE), (0, 0)))
    w_krp = _head_groups([jnp.zeros((KR_PAD, H_B, QK_NOPE), F32),
                          jnp.broadcast_to(place[:, None, :], (KR_PAD, H_B, QK_ROPE))]).astype(BF16)
    w_uv_all = w_uv[l].reshape(KV_LORA, COLS_VB).astype(BF16)
    wuk_t = jnp.transpose(w_uk[l], (1, 2, 0)).astype(BF16)
    wuv = jnp.transpose(w_uv[l], (1, 0, 2))
    eye = jnp.eye(H_B, dtype=F32)
    wz = (wuv[:, :, None, :] * eye[:, None, :, None]).reshape(H_B, KV_LORA, COLS_VB).astype(BF16)
    w_route = jnp.concatenate(
        [w_group[l], jnp.transpose(w_router[l], (1, 0, 2)).reshape(D_MODEL, N_EXPERTS),
         jnp.zeros((D_MODEL, ROUTE_COLS - N_GROUPS - N_EXPERTS), F32)], axis=1)
    w_route_hi = w_route.astype(BF16)
    w_route = jnp.stack([w_route_hi, (w_route - w_route_hi.astype(F32)).astype(BF16)])
    b_route = jnp.concatenate([b_group[l], b_router[l].reshape(N_EXPERTS),
                               jnp.zeros((ROUTE_COLS - N_GROUPS - N_EXPERTS,), F32)])[None, :]
    return dict(g_attn=g_attn[l][None, :], w_pack=w_pack, g_q=g_q_lat[l][None, :], g_kv=g_kv_lat[l][None, :],
                w_uq_p=w_uq_p, w_uq_g=w_uq_g, w_kn=w_kn, w_krp=w_krp, w_uv_all=w_uv_all, wuk_t=wuk_t, wz=wz,
                g_subln=g_subln[l][None, :], g_subln_col=g_subln[l][:, None],
                wa=w_a[l].astype(BF16), wb=w_b[l].astype(BF16), wo=w_o[l].astype(BF16),
                g_ffn=g_ffn[l][None, :], w_route=w_route, b_route=b_route)


def _trunk(x, pos_tile, n_pos_tiles, tm, caches, lws, lams, wgu, wdn, g_final, *, batch, seq, bm):
    depth = len(lws)
    t = batch * seq
    x = x.reshape(t, D_MODEL)
    tabs = _rope_tables(pos_tile)
    prompt = caches is None
    ks, vs, cs, rs = [], [], [], []
    new_cache = None
    for l in range(depth):
        w = lws[l]
        out_scale = 1.0 - (0.8 - 0.6 * math.exp(-0.3 * l))
        outs = _inproj(x, w, tabs, tm=tm, n_pos_tiles=n_pos_tiles, prompt=prompt, layer=l, depth=depth,
                       batch=batch, prev=new_cache)
        k, v, ckv, kr = outs[:4]
        if prompt:
            new_cache = (k, v, ckv, kr)
            qat, kb, vat, qbt, kq, vbt, gates = outs[4:]
            oa = _attn_t(qat, kb, vat, batch=batch, seq=seq, n_heads=H_A, dv=DV_A, n_maps=2, lam=lams[l],
                         g=w['g_subln_col'], out_scale=out_scale, name="diff_attn_prompt")
            ob = _attn_t(qbt, kq, vbt, batch=batch, seq=seq, n_heads=H_B, dv=V_B, n_maps=1,
                         name="mla_attn_prompt")
        else:
            qa, kb, vb, ckvb, krb, qn, qr, gates = outs[4:]
            kc, vc, cc, rc, past = caches
            oa = _diff_sample(lams[l], qa, kc, vc, kb, vb, w['g_subln'], layer=l, batch=batch, seq=seq,
                              past=past, out_scale=out_scale)
            ob = _mla_sample(qn, qr, cc, rc, ckvb, krb, w['wuk_t'], w['wz'], layer=l, batch=batch, seq=seq,
                             past=past)
        x1, h, route, counts, route_t = _merge(oa, ob, gates, x, w, tm=tm, tiled_t=prompt)
        slot_tok, blk_e, n_used, dest = _moe_plan(route_t, counts, bm)
        xs = h.at[slot_tok].get(mode="promise_in_bounds")
        y_slots = _experts(blk_e, n_used, xs, wgu, wdn, layer=l, bm=bm)
        y01 = y_slots.at[dest.reshape(TOP_K * t)].get(mode="promise_in_bounds")
        x = _combine(x1, y01, route, g_final, tm=tm, final_norm=(l == depth - 1))
        ks.append(k)
        vs.append(v)
        cs.append(ckv)
        rs.append(kr)
    y = x.reshape(batch, seq, D_MODEL)
    if prompt:
        k, v, ckv, kr = new_cache
        return (y,
                jnp.transpose(k.reshape(depth, batch, H_A, 2, DH_A, seq), (0, 1, 5, 2, 3, 4)),
                v.reshape(depth, batch, seq, H_A, DV_A),
                ckv.reshape(depth, batch, seq, KV_LORA),
                jnp.transpose(kr.reshape(depth, batch, QK_ROPE, seq), (0, 1, 3, 2)))
    return (y,
            jnp.stack(ks).reshape(depth, batch, seq, H_A, 2, DH_A),
            jnp.stack(vs).reshape(depth, batch, seq, H_A, DV_A),
            jnp.stack(cs).reshape(depth, batch, seq, KV_LORA),
            jnp.stack(rs).reshape(depth, batch, seq, QK_ROPE))


def kernel(x_prompt, x_sample, cache_diff_k, cache_diff_v, cache_mla_ckv, cache_mla_krope, g_attn, w_in,
           lambda_q1, lambda_k1, lambda_q2, lambda_k2, g_subln, w_a, g_q_lat, w_uq, g_kv_lat, w_uk, w_uv, w_b,
           w_o, g_ffn, w_group, b_group, w_router, b_router, w_gate_up, w_down, g_final):
    depth = w_in.shape[0]
    batch, seq, _ = x_prompt.shape
    dec_batch, dec_seq, _ = x_sample.shape
    past = cache_diff_k.shape[2]
    assert dec_seq == CHUNK and past % CHUNK == 0, "sample frames must form exactly one new chunk"
    assert seq % TQ == 0

    lws = [_layer_weights(l, g_attn, w_in, g_subln, w_a, g_q_lat, w_uq, g_kv_lat, w_uk, w_uv, w_b, w_o, g_ffn,
                          w_group, b_group, w_router, b_router) for l in range(depth)]
    lams = []
    for l in range(depth):
        lam_init = 0.8 - 0.6 * math.exp(-0.3 * l)
        lam = (jnp.exp(jnp.sum(lambda_q1[l] * lambda_k1[l])) - jnp.exp(jnp.sum(lambda_q2[l] * lambda_k2[l]))
               + lam_init)
        lams.append(lam.reshape(1).astype(F32))
    wgu = w_gate_up.reshape(depth * N_EXPERTS, D_MODEL, 2 * D_EXPERT)
    wdn = w_down.reshape(depth * N_EXPERTS, D_EXPERT, D_MODEL)
    g_fin = g_final[None, :]

    tm_p = min(512, seq)
    outs_p = _trunk(x_prompt, jnp.arange(seq, dtype=jnp.int32), seq // tm_p, tm_p, None, lws, lams, wgu, wdn,
                    g_fin, batch=batch, seq=seq, bm=min(512, batch * seq))

    tm_s = min(512, dec_batch * dec_seq)
    pos_tile = jnp.tile(past + jnp.arange(dec_seq, dtype=jnp.int32), tm_s // dec_seq)
    caches = (jnp.transpose(cache_diff_k, (0, 1, 3, 4, 5, 2)).reshape(depth * dec_batch * COLS_A, past),
              cache_diff_v.reshape(depth * dec_batch, past * H_A, DV_A),
              cache_mla_ckv.reshape(depth * dec_batch, past, KV_LORA),
              jnp.transpose(cache_mla_krope, (0, 1, 3, 2)).reshape(depth * dec_batch * QK_ROPE, past), past)
    outs_s = _trunk(x_sample, pos_tile, 1, tm_s, caches, lws, lams, wgu, wdn, g_fin,
                    batch=dec_batch, seq=dec_seq, bm=min(128, dec_batch * dec_seq))

    return (outs_p[0], outs_s[0]) + outs_p[1:] + outs_s[1:]
```

```python
import functools
import math

import jax
import jax.numpy as jnp
from jax import lax
from jax.experimental import pallas as pl
from jax.experimental.pallas import tpu as pltpu
from jax.experimental.pallas import tpu_sc as plsc

D_MODEL = 1024
CHUNK = 64
ROPE_THETA = 10000.0
EPS = 1e-6
H_A = 4
DH_A = 64
DV_A = 2 * DH_A
H_B = 8
Q_LORA = 384
KV_LORA = 256
QK_NOPE = 64
QK_ROPE = 32
V_B = 64
N_GROUPS = 4
EXPERTS_PER_GROUP = 8
N_EXPERTS = N_GROUPS * EXPERTS_PER_GROUP
TOP_K = 2
D_EXPERT = 512

COLS_A = H_A * 2 * DH_A
COLS_QR = H_B * QK_ROPE
COLS_QN = H_B * QK_NOPE
COLS_VB = H_B * V_B
LANES = 128
COLS_QB = H_B * LANES
KR_PAD = LANES
PACK_COLS = 3 * COLS_A + Q_LORA + KV_LORA + KR_PAD + 2 * D_MODEL
ROUTE_COLS = LANES
ROUTE_ROWS = 8
PACKED_COLS = D_MODEL // 2
ROW_PIECES = PACKED_COLS // LANES
SC_WINDOW = 128
TQ = 512
TK = 256
ONES_ROWS = 16
MERGE_SUB = 256
VMEM_LIMIT = 56 * 1024 * 1024
LOG2E = 1.4426950408889634
SCALE_A = DH_A ** -0.5 * LOG2E
SCALE_B = (QK_NOPE + QK_ROPE) ** -0.5 * LOG2E

F32 = jnp.float32
BF16 = jnp.bfloat16
NEG_INF = float("-inf")


def _cparams(sem):
    return pltpu.CompilerParams(dimension_semantics=sem, vmem_limit_bytes=VMEM_LIMIT)


def _rms(x, g):
    return x * lax.rsqrt(jnp.mean(x * x, axis=-1, keepdims=True) + EPS) * g


def _widen(tab, cols):
    reps = cols // LANES
    return tab if reps == 1 else jnp.concatenate([tab] * reps, axis=-1)


def _rope(x, cos, sin_signed, chunk):
    n = x.shape[-1]
    half = chunk // 2
    lane = lax.broadcasted_iota(jnp.int32, x.shape, 1)
    fwd = pltpu.roll(x, n - half, 1)
    bwd = pltpu.roll(x, half, 1)
    swapped = jnp.where((lane & (chunk - 1)) < half, fwd, bwd)
    return x * _widen(cos, n) + swapped * _widen(sin_signed, n)


def _dot_nt(a, b):
    return lax.dot_general(a, b, (((1,), (1,)), ((), ())), preferred_element_type=F32)


def _store_tiles_t(ref, val):
    tile = ref.shape[2]
    for r in range(ref.shape[0]):
        ref[r] = val[r * tile:(r + 1) * tile, :].T.astype(ref.dtype)


def _store_vt_ones(ref, val, n_heads, dv):
    dva = dv + ONES_ROWS
    ones = jnp.ones((ONES_ROWS, TK), ref.dtype)
    for r in range(ref.shape[0]):
        vt = val[r * TK:(r + 1) * TK, :].T.astype(ref.dtype)
        for h in range(n_heads):
            ref[r, h * dva:h * dva + dv, :] = vt[h * dv:(h + 1) * dv, :]
            ref[r, h * dva + dv:(h + 1) * dva, :] = ones


def _store_packed_rows(ref, row0, xf):
    rows = xf.shape[0]
    bits = pltpu.bitcast(xf, jnp.uint32)
    words = bits[:, :PACKED_COLS] | (bits[:, PACKED_COLS:] >> 16)
    for c in range(ROW_PIECES):
        ref[pl.ds(row0 * ROW_PIECES + c, rows, stride=ROW_PIECES), :] = words[:, c * LANES:(c + 1) * LANES]


def _load_packed_rows(ref, rows):
    pieces = [ref[pl.ds(c, rows, stride=ROW_PIECES), :] for c in range(ROW_PIECES)]
    hi = [pltpu.bitcast(p & jnp.uint32(0xFFFF0000), F32) for p in pieces]
    lo = [pltpu.bitcast(p << 16, F32) for p in pieces]
    return jnp.concatenate(hi + lo, axis=1).astype(BF16)


def _load_tiles_t(ref):
    return jnp.concatenate([ref[r].astype(F32).T for r in range(ref.shape[0])], axis=0).astype(BF16)


def _inproj_kernel(*refs, prompt):
    (x_ref, g_ref, w_ref, gq_ref, gkv_ref, wuq_ref, c64_ref, s64_ref, c32_ref, s32_ref) = refs[:10]
    x = x_ref[...]
    hb = _rms(x, g_ref[...]).astype(BF16)

    def proj(lo, hi):
        return jnp.dot(hb, w_ref[:, lo:hi], preferred_element_type=F32)

    c64, s64 = c64_ref[...], s64_ref[...]
    c32, s32 = c32_ref[...], s32_ref[...]
    o = 0
    qa = _rope(proj(o, o + COLS_A), c64, s64, DH_A) * SCALE_A
    o += COLS_A
    ka = _rope(proj(o, o + COLS_A), c64, s64, DH_A)
    o += COLS_A
    va = proj(o, o + COLS_A)
    o += COLS_A
    cq = _rms(proj(o, o + Q_LORA), gq_ref[...]).astype(BF16)
    o += Q_LORA
    qfull = jnp.dot(cq, wuq_ref[...], preferred_element_type=F32) * SCALE_B
    ckv = _rms(proj(o, o + KV_LORA), gkv_ref[...])
    ckvb = ckv.astype(BF16)
    o += KV_LORA
    kr_pad = _rope(proj(o, o + KR_PAD), c32, s32, QK_ROPE)
    o += KR_PAD
    gates = jax.nn.sigmoid(proj(o, o + 2 * D_MODEL)).astype(BF16)

    if prompt:
        cq_ref, sq_ref, wkn_ref, wkr_ref, wuv_ref = refs[10:15]
        (k_ref, v_ref, ckv_ref, kr_ref, qat_ref, kb_ref, vat_ref, qbt_ref, kq_ref, vbt_ref,
         gate_ref) = refs[len(refs) - 11:]
        k_ref[...] = ka.T
        for h in range(H_A):
            v_ref[pl.ds(h, va.shape[0], stride=H_A), :] = va[:, h * DV_A:(h + 1) * DV_A]
        ckv_ref[...] = ckv
        kr_ref[...] = kr_pad.T[:QK_ROPE, :]
        _store_tiles_t(qat_ref, qa)
        kb_ref[...] = ka.astype(BF16)
        _store_vt_ones(vat_ref, va, H_A, DV_A)
        _store_tiles_t(qbt_ref, _rope(qfull, cq_ref[...], sq_ref[...], QK_ROPE))
        kq = (jnp.dot(ckvb, wkn_ref[...], preferred_element_type=F32)
              + jnp.dot(kr_pad.astype(BF16), wkr_ref[...], preferred_element_type=F32))
        kq_ref[...] = kq.astype(BF16)
        _store_vt_ones(vbt_ref, jnp.dot(ckvb, wuv_ref[...], preferred_element_type=F32), H_B, V_B)
    else:
        (k_ref, v_ref, ckv_ref, kr_ref, qa_ref, kb_ref, vb_ref, ckvb_ref, krb_ref, qn_ref, qr_ref,
         gate_ref) = refs[10:]
        qa_ref[...] = qa.astype(BF16)
        kb_ref[...] = ka.astype(BF16)
        vb_ref[...] = va.astype(BF16)
        ckvb_ref[...] = ckvb
        krb_ref[...] = kr_pad[:, :QK_ROPE].astype(BF16)
        qn_ref[...] = qfull[:, :COLS_QN].astype(BF16)
        qr_ref[...] = _rope(qfull[:, COLS_QN:], c32, s32, QK_ROPE).astype(BF16)
        k_ref[...] = ka
        v_ref[...] = va
        ckv_ref[...] = ckv
        kr_ref[...] = kr_pad[:, :QK_ROPE]
    gate_ref[...] = gates


def _inproj(x, w, tabs, *, tm, n_pos_tiles, prompt, layer=0, depth=1, batch=1, prev=None):
    t = x.shape[0]
    nt = t // tm
    tok = lambda i: (i, 0)
    full = lambda i: (0, 0)
    pos = lambda i: (i % n_pos_tiles, 0)
    tile3 = lambda i: (i, 0, 0)
    wuq = w['w_uq_g'] if prompt else w['w_uq_p']
    ins = [x, w['g_attn'], w['w_pack'], w['g_q'], w['g_kv'], wuq, tabs['c64'], tabs['s64'], tabs['c32'], tabs['s32']]
    in_specs = [pl.BlockSpec((tm, D_MODEL), tok), pl.BlockSpec((1, D_MODEL), full),
                pl.BlockSpec((D_MODEL, PACK_COLS), full), pl.BlockSpec((1, Q_LORA), full),
                pl.BlockSpec((1, KV_LORA), full), pl.BlockSpec(wuq.shape, full)] + [pl.BlockSpec((tm, LANES), pos)] * 4
    if prompt:
        seq = t // batch
        per_b = seq // tm
        seq_minor = lambda i: (layer * batch + i // per_b, i % per_b)
        tok_l = lambda i: (layer * nt + i, 0)
        out_specs = [pl.BlockSpec((COLS_A, tm), seq_minor), pl.BlockSpec((tm * H_A, DV_A), tok_l),
                     pl.BlockSpec((tm, KV_LORA), tok_l), pl.BlockSpec((QK_ROPE, tm), seq_minor)]
        out_shape = [jax.ShapeDtypeStruct((depth * batch * COLS_A, seq), F32),
                     jax.ShapeDtypeStruct((depth * t * H_A, DV_A), F32),
                     jax.ShapeDtypeStruct((depth * t, KV_LORA), F32),
                     jax.ShapeDtypeStruct((depth * batch * QK_ROPE, seq), F32)]
    else:
        leaves = [(COLS_A, F32), (COLS_A, F32), (KV_LORA, F32), (QK_ROPE, F32)]
        out_specs = [pl.BlockSpec((tm, c), tok) for c, _ in leaves]
        out_shape = [jax.ShapeDtypeStruct((t, c), d) for c, d in leaves]

    def add2d(c):
        out_specs.append(pl.BlockSpec((tm, c), tok))
        out_shape.append(jax.ShapeDtypeStruct((t, c), BF16))

    def add3d(c, tile):
        out_specs.append(pl.BlockSpec((tm // tile, c, tile), tile3))
        out_shape.append(jax.ShapeDtypeStruct((t // tile, c, tile), BF16))

    if prompt:
        ins += [tabs['cq'], tabs['sq'], w['w_kn'], w['w_krp'], w['w_uv_all']]
        in_specs += [pl.BlockSpec((tm, LANES), pos)] * 2 + [pl.BlockSpec(w[n].shape, full)
                                                             for n in ('w_kn', 'w_krp', 'w_uv_all')]
        add3d(COLS_A, TQ), add2d(COLS_A), add3d(H_A * (DV_A + ONES_ROWS), TK)
        add3d(COLS_QB, TQ), add2d(COLS_QB), add3d(H_B * (V_B + ONES_ROWS), TK)
    else:
        for c in (COLS_A, COLS_A, COLS_A, KV_LORA, QK_ROPE, COLS_QN, COLS_QR):
            add2d(c)
    add2d(2 * D_MODEL)
    aliases = {}
    if prev is not None:
        aliases = {len(ins) + n: n for n in range(len(prev))}
        ins += list(prev)
        in_specs += [pl.BlockSpec(memory_space=pl.ANY)] * len(prev)
    return pl.pallas_call(
        functools.partial(_inproj_kernel, prompt=prompt),
        grid=(nt,),
        in_specs=in_specs,
        out_specs=out_specs,
        out_shape=out_shape,
        input_output_aliases=aliases,
        compiler_params=_cparams(("parallel",)),
        name="inproj_prompt" if prompt else "inproj_sample",
    )(*ins)


def _attn_t_kernel(*refs, n_heads, dv, n_maps, out_scale):
    if n_maps == 2:
        lam_ref, qt_ref, k_ref, vt_ref, g_ref, o_ref, q_sc, m_sc, acc_sc, sta_sc, stb_sc, mxa_sc, mxb_sc = refs
    else:
        qt_ref, k_ref, vt_ref, o_ref, m_sc, acc_sc, sta_sc, stb_sc, mxa_sc, mxb_sc = refs
    i = pl.program_id(1)
    dva = dv + ONES_ROWS
    width = n_maps * TQ
    heads = range(n_heads)
    if n_maps == 2:
        for h in heads:
            qt = qt_ref[0, h * LANES:(h + 1) * LANES, :]
            row = lax.broadcasted_iota(jnp.int32, qt.shape, 0)
            zero = jnp.zeros_like(qt)
            q_sc[h, :, :TQ] = jnp.where(row < DH_A, qt, zero)
            q_sc[h, :, TQ:] = jnp.where(row >= DH_A, qt, zero)
    m_sc[...] = jnp.full(m_sc.shape, NEG_INF, F32)
    acc_sc[...] = jnp.zeros(acc_sc.shape, F32)
    rel = ((lax.broadcasted_iota(jnp.int32, (TK, width), 1) & (TQ - 1)) // CHUNK
           - lax.broadcasted_iota(jnp.int32, (TK, width), 0) // CHUNK)

    def score_head(j, h, st_ref, mx_ref):
        rows = pl.ds(pl.multiple_of(j * TK, TK), TK)
        st = jnp.dot(k_ref[rows, h * LANES:(h + 1) * LANES],
                     q_sc[h] if n_maps == 2 else qt_ref[0, h * LANES:(h + 1) * LANES, :],
                     preferred_element_type=F32)
        st_ref[h] = st
        mx_ref[h] = jnp.max(st, axis=0, keepdims=True)

    def softmax_head(j, h, st_ref, mx_ref, masked):
        st = st_ref[h]
        if masked:
            st = jnp.where(rel >= j * (TK // CHUNK) - i * (TQ // CHUNK), st, NEG_INF)
            tile_max = jnp.max(st, axis=0, keepdims=True)
        else:
            tile_max = mx_ref[h]
        m = m_sc[h]
        m_new = jnp.maximum(m, tile_max)
        m_sc[h] = m_new
        return jnp.exp2(st - m_new).astype(BF16), jnp.exp2(m - m_new)

    def value_head(j, h, pt, alpha):
        pv = jnp.dot(vt_ref[j, h * dva:(h + 1) * dva, :], pt, preferred_element_type=F32)
        acc_sc[h] = alpha * acc_sc[h] + pv

    def stage(j, cur, masked, nxt=None):
        pending = None
        for h in heads:
            if nxt is not None:
                score_head(j + 1, h, *nxt)
            pt, alpha = softmax_head(j, h, *cur, masked)
            if pending is not None:
                value_head(j, *pending)
            pending = (h, pt, alpha)
        value_head(j, *pending)

    buf_a, buf_b = (sta_sc, mxa_sc), (stb_sc, mxb_sc)

    def full_pair(p, carry):
        j = 2 * p
        stage(j, buf_a, False, buf_b)
        stage(j + 1, buf_b, False, buf_a)
        return carry

    assert TQ == 2 * TK
    for h in heads:
        score_head(0, h, *buf_a)
    lax.fori_loop(0, i, full_pair, 0)
    stage(2 * i, buf_a, True, buf_b)
    stage(2 * i + 1, buf_b, True)


    for h in heads:
        acc = acc_sc[h]
        inv = acc[:dv] / acc[dv:dv + 1]
        if n_maps == 2:
            o = inv[:, :TQ] - lam_ref[0] * inv[:, TQ:]
            o = o * lax.rsqrt(jnp.mean(o * o, axis=0, keepdims=True) + EPS) * g_ref[...] * out_scale
        else:
            o = inv
        o_ref[0, h * dv:(h + 1) * dv, :] = o.astype(o_ref.dtype)


def _attn_t(qt, k, vt, *, batch, seq, n_heads, dv, n_maps, lam=None, g=None, out_scale=1.0, name):
    nq = seq // TQ
    dva = dv + ONES_ROWS
    width = n_maps * TQ
    once = pl.Buffered(1)
    ins, in_specs, scratch = [], [], []
    if n_maps == 2:
        ins.append(lam)
        in_specs.append(pl.BlockSpec(memory_space=pltpu.SMEM))
        scratch.append(pltpu.VMEM((n_heads, LANES, width), BF16))
    ins += [qt, k, vt]
    in_specs += [pl.BlockSpec((1, n_heads * LANES, TQ), lambda b, i: (b * nq + i, 0, 0)),
                 pl.BlockSpec((seq, n_heads * LANES), lambda b, i: (b, 0), pipeline_mode=once),
                 pl.BlockSpec((seq // TK, n_heads * dva, TK), lambda b, i: (b, 0, 0), pipeline_mode=once)]
    if n_maps == 2:
        ins.append(g)
        in_specs.append(pl.BlockSpec((dv, 1), lambda b, i: (0, 0)))
    return pl.pallas_call(
        functools.partial(_attn_t_kernel, n_heads=n_heads, dv=dv, n_maps=n_maps, out_scale=out_scale),
        grid=(batch, nq),
        in_specs=in_specs,
        out_specs=pl.BlockSpec((1, n_heads * dv, TQ), lambda b, i: (b * nq + i, 0, 0)),
        out_shape=jax.ShapeDtypeStruct((batch * nq, n_heads * dv, TQ), BF16),
        scratch_shapes=scratch + [pltpu.VMEM((n_heads, 1, width), F32), pltpu.VMEM((n_heads, dva, width), F32),
                                  pltpu.VMEM((n_heads, TK, width), F32), pltpu.VMEM((n_heads, TK, width), F32),
                                  pltpu.VMEM((n_heads, 1, width), F32), pltpu.VMEM((n_heads, 1, width), F32)],
        compiler_params=_cparams(("parallel", "arbitrary")),
        name=name,
    )(*ins)


def _flash_steps(ss, vs, m_sc, l_sc, acc_sc):
    ps, alphas = [], []
    for c, s in enumerate(ss):
        m_prev = m_sc[c]
        m_new = jnp.maximum(m_prev, jnp.max(s, axis=-1, keepdims=True))
        alpha = jnp.exp2(m_prev - m_new)
        cols = s.shape[-1]
        p = jnp.exp2(s - (_widen(m_new, cols) if cols % LANES == 0 else m_new[:, :cols]))
        l_sc[c] = alpha * l_sc[c] + jnp.sum(p, axis=-1, keepdims=True)
        m_sc[c] = m_new
        ps.append(p.astype(BF16))
        alphas.append(alpha)
    pvs = [jnp.dot(p, v, preferred_element_type=F32) for p, v in zip(ps, vs)]
    for c, pv in enumerate(pvs):
        acc_sc[c] = _widen(alphas[c], acc_sc.shape[-1]) * acc_sc[c] + pv


def _init_flash(m_sc, l_sc, acc_sc):
    m_sc[...] = jnp.full(m_sc.shape, NEG_INF, F32)
    l_sc[...] = jnp.zeros(l_sc.shape, F32)
    acc_sc[...] = jnp.zeros(acc_sc.shape, F32)


def _stack_maps(q):
    lane = lax.broadcasted_iota(jnp.int32, q.shape, 1)
    zero = jnp.zeros_like(q)
    return jnp.concatenate([jnp.where(lane < DH_A, q, zero), jnp.where(lane >= DH_A, q, zero)], axis=0)


def _diff_sample_kernel(lam_ref, q_ref, kc_ref, vc_ref, kn_ref, vn_ref, g_ref, o_ref, m_sc, l_sc, acc_sc,
                        *, tq, tk, n_cache_tiles, out_scale):
    heads = range(H_A)
    lanes = [slice(h * DV_A, (h + 1) * DV_A) for h in heads]
    qss = [_stack_maps(q_ref[:, lanes[h]]) for h in heads]
    _init_flash(m_sc, l_sc, acc_sc)

    def cache_scores(j):
        return [jnp.dot(qss[h], kc_ref[lanes[h], j * tk:(j + 1) * tk].astype(BF16), preferred_element_type=F32)
                for h in heads]

    ss = cache_scores(0)
    for j in range(n_cache_tiles):
        ss_next = (cache_scores(j + 1) if j + 1 < n_cache_tiles
                   else [_dot_nt(qss[h], kn_ref[:, lanes[h]]) for h in heads])
        vs = [vc_ref[0, pl.ds(j * tk * H_A + h, tk, stride=H_A), :].astype(BF16) for h in heads]
        _flash_steps(ss, vs, m_sc, l_sc, acc_sc)
        ss = ss_next
    _flash_steps(ss, [vn_ref[:, lanes[h]] for h in heads], m_sc, l_sc, acc_sc)
    for h in heads:
        inv = acc_sc[h] / l_sc[h]
        o = inv[:tq] - lam_ref[0] * inv[tq:]
        o_ref[:, lanes[h]] = (_rms(o, g_ref[...]) * out_scale).astype(o_ref.dtype)


def _diff_sample(lam, qa, kcache, vcache, kb, vb, g_subln, *, layer, batch, seq, past, out_scale):
    tk = min(512, past)
    smem = pl.BlockSpec(memory_space=pltpu.SMEM)
    new = pl.BlockSpec((seq, COLS_A), lambda b: (b, 0))
    kcache_spec = pl.BlockSpec((COLS_A, past), lambda b: (layer * batch + b, 0))
    vcache_spec = pl.BlockSpec((1, past * H_A, DV_A), lambda b: (layer * batch + b, 0, 0))
    return pl.pallas_call(
        functools.partial(_diff_sample_kernel, tq=seq, tk=tk, n_cache_tiles=past // tk, out_scale=out_scale),
        grid=(batch,),
        in_specs=[smem, new, kcache_spec, vcache_spec, new, new, pl.BlockSpec((1, DV_A), lambda b: (0, 0))],
        out_specs=new,
        out_shape=jax.ShapeDtypeStruct((batch * seq, COLS_A), BF16),
        scratch_shapes=[pltpu.VMEM((H_A, 2 * seq, LANES), F32), pltpu.VMEM((H_A, 2 * seq, LANES), F32),
                        pltpu.VMEM((H_A, 2 * seq, DV_A), F32)],
        compiler_params=_cparams(("parallel",)),
        name="diff_attn_sample",
    )(lam, qa, kcache, vcache, kb, vb, g_subln)


def _mla_sample_kernel(qn_ref, qr_ref, cc_ref, rc_ref, cn_ref, rn_ref, wuk_ref, wz_ref, o_ref,
                       ql_sc, qr_sc, m_sc, l_sc, acc_sc, *, tq, tk, n_cache_tiles):
    qn = qn_ref[...]
    qr = qr_ref[...].astype(F32)
    for h in range(H_B):
        ql = jnp.dot(qn[:, h * QK_NOPE:(h + 1) * QK_NOPE], wuk_ref[h], preferred_element_type=F32)
        ql_sc[h * tq:(h + 1) * tq, :] = ql.astype(BF16)
        qr_sc[h * tq:(h + 1) * tq, :] = qr[:, h * QK_ROPE:(h + 1) * QK_ROPE].astype(BF16)
    _init_flash(m_sc, l_sc, acc_sc)
    n_chains = m_sc.shape[0]
    half = H_B * tq // n_chains
    parts = [slice(c * half, (c + 1) * half) for c in range(n_chains)]

    def cache_tile(j):
        c = cc_ref[0, j * tk:(j + 1) * tk, :].astype(BF16)
        rt = rc_ref[:, j * tk:(j + 1) * tk].astype(BF16)
        return c, [_dot_nt(ql_sc[p, :], c) + jnp.dot(qr_sc[p, :], rt, preferred_element_type=F32) for p in parts]

    def new_tile():
        c = cn_ref[...]
        return c, [_dot_nt(ql_sc[p, :], c) + _dot_nt(qr_sc[p, :], rn_ref[...]) for p in parts]

    c, ss = cache_tile(0)
    for j in range(n_cache_tiles):
        c_next, ss_next = cache_tile(j + 1) if j + 1 < n_cache_tiles else new_tile()
        _flash_steps(ss, [c] * n_chains, m_sc, l_sc, acc_sc)
        c, ss = c_next, ss_next
    _flash_steps(ss, [c] * n_chains, m_sc, l_sc, acc_sc)
    ob = jnp.zeros(o_ref.shape, F32)
    heads_per_chain = H_B // n_chains
    for h in range(H_B):
        c, r = divmod(h, heads_per_chain)
        o_lat = (acc_sc[c, r * tq:(r + 1) * tq, :] / _widen(l_sc[c, r * tq:(r + 1) * tq, :], KV_LORA)).astype(BF16)
        ob += jnp.dot(o_lat, wz_ref[h], preferred_element_type=F32)
    o_ref[...] = ob.astype(o_ref.dtype)


def _mla_sample(qn, qr, ccache, rcache, ckvb, krb, wuk_t, wz, *, layer, batch, seq, past):
    tk = min(512, past)
    rows = H_B * seq
    return pl.pallas_call(
        functools.partial(_mla_sample_kernel, tq=seq, tk=tk, n_cache_tiles=past // tk),
        grid=(batch,),
        in_specs=[pl.BlockSpec((seq, COLS_QN), lambda b: (b, 0)),
                  pl.BlockSpec((seq, COLS_QR), lambda b: (b, 0)),
                  pl.BlockSpec((1, past, KV_LORA), lambda b: (layer * batch + b, 0, 0)),
                  pl.BlockSpec((QK_ROPE, past), lambda b: (layer * batch + b, 0)),
                  pl.BlockSpec((seq, KV_LORA), lambda b: (b, 0)),
                  pl.BlockSpec((seq, QK_ROPE), lambda b: (b, 0)),
                  pl.BlockSpec((H_B, QK_NOPE, KV_LORA), lambda b: (0, 0, 0)),
                  pl.BlockSpec((H_B, KV_LORA, COLS_VB), lambda b: (0, 0, 0))],
        out_specs=pl.BlockSpec((seq, COLS_VB), lambda b: (b, 0)),
        out_shape=jax.ShapeDtypeStruct((batch * seq, COLS_VB), BF16),
        scratch_shapes=[pltpu.VMEM((rows, KV_LORA), BF16), pltpu.VMEM((rows, QK_ROPE), BF16),
                        pltpu.VMEM((2, rows // 2, LANES), F32), pltpu.VMEM((2, rows // 2, LANES), F32),
                        pltpu.VMEM((2, rows // 2, KV_LORA), F32)],
        compiler_params=_cparams(("parallel",)),
        name="mla_attn_sample",
    )(qn, qr, ccache, rcache, ckvb, krb, wuk_t, wz)


def _merge_kernel(oa_ref, ob_ref, gate_ref, x_ref, wa_ref, wb_ref, wo_ref, gf_ref, wr_ref, br_ref, tri_ref,
                  x1_ref, h_ref, route_ref, count_ref, routet_ref, run_sc, *, tiled_t):
    oa = _load_tiles_t(oa_ref) if tiled_t else oa_ref[...]
    ob = _load_tiles_t(ob_ref) if tiled_t else ob_ref[...]
    tm = x_ref.shape[0]
    subs = [slice(r * MERGE_SUB, (r + 1) * MERGE_SUB) for r in range(tm // MERGE_SUB)]
    yas = [jnp.dot(oa[s], wa_ref[...], preferred_element_type=F32) for s in subs]
    ybs = [jnp.dot(ob[s], wb_ref[...], preferred_element_type=F32) for s in subs]
    merged = []
    for s, ya, yb in zip(subs, yas, ybs):
        gates = gate_ref[s, :].astype(F32)
        merged.append((gates[:, :D_MODEL] * ya + gates[:, D_MODEL:] * yb).astype(BF16))
    x1s = [x_ref[s, :] + jnp.dot(mg, wo_ref[...], preferred_element_type=F32) for s, mg in zip(subs, merged)]
    logit_parts = []
    for s, x1 in zip(subs, x1s):
        x1_ref[s, :] = x1
        h = _rms(x1, gf_ref[...])
        h_hi = h.astype(BF16)
        _store_packed_rows(h_ref, s.start, h_hi.astype(F32))
        h_lo = (h - h_hi.astype(F32)).astype(BF16)
        logit_parts.append(jnp.dot(h_hi, wr_ref[0], preferred_element_type=F32)
                           + jnp.dot(h_lo, wr_ref[0], preferred_element_type=F32)
                           + jnp.dot(h_hi, wr_ref[1], preferred_element_type=F32))

    logits = jnp.concatenate(logit_parts, axis=0) + br_ref[...]
    lane = lax.broadcasted_iota(jnp.int32, logits.shape, 1)
    big = jnp.int32(ROUTE_COLS)

    def top1(mask):
        v = jnp.max(jnp.where(mask, logits, NEG_INF), axis=-1, keepdims=True)
        idx = jnp.min(jnp.where(mask & (logits == v), lane, big), axis=-1, keepdims=True)
        return v, idx

    gmask = lane < N_GROUPS
    gmax, gidx = top1(gmask)
    g_w = 1.0 / jnp.sum(jnp.where(gmask, jnp.exp(logits - gmax), 0.0), axis=-1, keepdims=True)
    lo = N_GROUPS + gidx * EXPERTS_PER_GROUP
    emask = (lane >= lo) & (lane < lo + EXPERTS_PER_GROUP)
    v1, i1 = top1(emask)
    v2, i2 = top1(emask & (lane != i1))
    e2 = jnp.exp(v2 - v1)
    w1 = g_w / (1.0 + e2)
    w2 = g_w * e2 / (1.0 + e2)
    @pl.when(pl.program_id(0) == 0)
    def _():
        run_sc[...] = jnp.zeros(run_sc.shape, F32)

    e1 = i1 - N_GROUPS
    e2i = i2 - N_GROUPS
    picks = jnp.where((lane == e1) | (lane == e2i), 1.0, 0.0)
    before = jnp.dot(tri_ref[...], picks.astype(BF16), preferred_element_type=F32) + run_sc[...]
    rank1 = jnp.sum(jnp.where(lane == e1, before, 0.0), axis=-1, keepdims=True)
    rank2 = jnp.sum(jnp.where(lane == e2i, before, 0.0), axis=-1, keepdims=True)
    run_sc[...] = run_sc[...] + jnp.sum(picks, axis=0, keepdims=True)
    count_ref[...] = run_sc[...]

    vals = [e1.astype(F32), e2i.astype(F32), w1, w2, rank1, rank2]
    route = jnp.zeros(logits.shape, F32)
    for n, v in enumerate(vals):
        route = jnp.where(lane == n, v, route)
    route_ref[...] = route
    routet_ref[...] = route.T[:routet_ref.shape[0], :]


def _merge(oa, ob, gates, x, w, *, tm, tiled_t):
    t = x.shape[0]
    tok = lambda i: (i, 0)
    full = lambda i: (0, 0)
    if tiled_t:
        o_specs = [pl.BlockSpec((tm // TQ, COLS_A, TQ), lambda i: (i, 0, 0)),
                   pl.BlockSpec((tm // TQ, COLS_VB, TQ), lambda i: (i, 0, 0))]
    else:
        o_specs = [pl.BlockSpec((tm, COLS_A), tok), pl.BlockSpec((tm, COLS_VB), tok)]
    return pl.pallas_call(
        functools.partial(_merge_kernel, tiled_t=tiled_t),
        grid=(t // tm,),
        in_specs=o_specs + [
            pl.BlockSpec((tm, 2 * D_MODEL), tok), pl.BlockSpec((tm, D_MODEL), tok),
            pl.BlockSpec((COLS_A, D_MODEL), full), pl.BlockSpec((COLS_VB, D_MODEL), full),
            pl.BlockSpec((D_MODEL, D_MODEL), full), pl.BlockSpec((1, D_MODEL), full),
            pl.BlockSpec((2, D_MODEL, ROUTE_COLS), lambda i: (0, 0, 0)), pl.BlockSpec((1, ROUTE_COLS), full),
            pl.BlockSpec((tm, tm), full)],
        out_specs=[pl.BlockSpec((tm, D_MODEL), tok), pl.BlockSpec((tm * ROW_PIECES, LANES), tok),
                   pl.BlockSpec((tm, ROUTE_COLS), tok), pl.BlockSpec((1, ROUTE_COLS), full),
                   pl.BlockSpec((ROUTE_ROWS, tm), lambda i: (0, i))],
        out_shape=[jax.ShapeDtypeStruct((t, D_MODEL), F32),
                   jax.ShapeDtypeStruct((t * ROW_PIECES, LANES), jnp.uint32),
                   jax.ShapeDtypeStruct((t, ROUTE_COLS), F32), jax.ShapeDtypeStruct((1, ROUTE_COLS), F32),
                   jax.ShapeDtypeStruct((ROUTE_ROWS, t), F32)],
        scratch_shapes=[pltpu.VMEM((1, ROUTE_COLS), F32)],
        compiler_params=_cparams(("arbitrary",)),
        name="merge_router",
    )(oa, ob, gates, x, w['wa'], w['wb'], w['wo'], w['g_ffn'], w['w_route'], w['b_route'],
      jnp.tril(jnp.ones((tm, tm), BF16), -1))


def _expert_kernel(blk_e_ref, n_used_ref, blk_valid_ref, x_ref, wgu_ref, wdn_ref, y_ref):
    i = pl.program_id(0)
    bm = y_ref.shape[0]

    @pl.when(i < n_used_ref[0])
    def _():
        x = _load_packed_rows(x_ref, bm)
        row = lax.broadcasted_iota(jnp.int32, (bm, 1), 0)
        x = jnp.where(row < blk_valid_ref[i], x, jnp.zeros_like(x))
        gu = jnp.dot(x, wgu_ref[0].astype(BF16), preferred_element_type=F32)
        gate, up = gu[:, :D_EXPERT], gu[:, D_EXPERT:]
        a = (gate * jax.nn.sigmoid(gate) * up).astype(BF16)
        y_ref[...] = jnp.dot(a, wdn_ref[0].astype(BF16), preferred_element_type=F32).astype(y_ref.dtype)

    @pl.when(i >= n_used_ref[0])
    def _():
        y_ref[...] = jnp.zeros(y_ref.shape, y_ref.dtype)


def _experts(blk_e, n_used, blk_valid, xs_rows, wgu, wdn, *, layer, bm):
    n_slots = xs_rows.shape[0] // ROW_PIECES
    grid_spec = pltpu.PrefetchScalarGridSpec(
        num_scalar_prefetch=3,
        grid=(n_slots // bm,),
        in_specs=[pl.BlockSpec((bm * ROW_PIECES, LANES), lambda i, be, nu, bv: (i, 0)),
                  pl.BlockSpec((1, D_MODEL, 2 * D_EXPERT),
                               lambda i, be, nu, bv: (layer * N_EXPERTS + be[i], 0, 0)),
                  pl.BlockSpec((1, D_EXPERT, D_MODEL),
                               lambda i, be, nu, bv: (layer * N_EXPERTS + be[i], 0, 0))],
        out_specs=pl.BlockSpec((bm, D_MODEL), lambda i, be, nu, bv: (i, 0)),
    )
    return pl.pallas_call(
        _expert_kernel,
        grid_spec=grid_spec,
        out_shape=jax.ShapeDtypeStruct((n_slots, D_MODEL), BF16),
        compiler_params=_cparams(("arbitrary",)),
        name="experts",
    )(blk_e, n_used, blk_valid, xs_rows, wgu, wdn)


def _combine_kernel(x_ref, y0_ref, y1_ref, route_ref, g_ref, o_ref, *, final_norm):
    r = route_ref[...]
    y = x_ref[...] + r[:, 2:3] * y0_ref[...].astype(F32) + r[:, 3:4] * y1_ref[...].astype(F32)
    if final_norm:
        y = _rms(y, g_ref[...])
    o_ref[...] = y


def _combine(x1, y01, route, g_final, *, tm, final_norm):
    t = x1.shape[0]
    nt = t // tm
    tok = lambda i: (i, 0)
    return pl.pallas_call(
        functools.partial(_combine_kernel, final_norm=final_norm),
        grid=(nt,),
        in_specs=[pl.BlockSpec((tm, D_MODEL), tok), pl.BlockSpec((tm, D_MODEL), tok),
                  pl.BlockSpec((tm, D_MODEL), lambda i: (nt + i, 0)), pl.BlockSpec((tm, ROUTE_COLS), tok),
                  pl.BlockSpec((1, D_MODEL), lambda i: (0, 0))],
        out_specs=pl.BlockSpec((tm, D_MODEL), tok),
        out_shape=jax.ShapeDtypeStruct((t, D_MODEL), F32),
        compiler_params=_cparams(("parallel",)),
        name="combine",
    )(x1, y01, y01, route, g_final)


def _transpose_cast_kernel(x_ref, o_ref):
    o_ref[...] = x_ref[...].T.astype(o_ref.dtype)


def _transpose_cast(xt, block=256):
    c, d = xt.shape
    return pl.pallas_call(
        _transpose_cast_kernel,
        grid=(c // block,),
        in_specs=[pl.BlockSpec((block, d), lambda i: (i, 0))],
        out_specs=pl.BlockSpec((d, block), lambda i: (0, i)),
        out_shape=jax.ShapeDtypeStruct((d, c), BF16),
        compiler_params=_cparams(("parallel",)),
        name="weight_transpose",
    )(xt)


def _rope_tables(pos):
    lane = jnp.arange(LANES, dtype=jnp.int32)

    def tab(chunk):
        half = chunk // 2
        inv = ROPE_THETA ** (-(lane % half).astype(F32) / half)
        ang = pos.astype(F32)[:, None] * inv[None, :]
        first = (lane % chunk) < half
        return jnp.cos(ang), jnp.where(first[None, :], -jnp.sin(ang), jnp.sin(ang))

    c64, s64 = tab(DH_A)
    c32, s32 = tab(QK_ROPE)
    rope_lane = ((lane >= QK_NOPE) & (lane < QK_NOPE + QK_ROPE))[None, :]
    return dict(c64=c64, s64=s64, c32=c32, s32=s32,
                cq=jnp.where(rope_lane, c32, 1.0), sq=jnp.where(rope_lane, s32, 0.0))


def _sc_dispatch(rows, idx0, idx1, n_out_rows):
    n = rows.shape[0]
    mesh = plsc.VectorSubcoreMesh(core_axis_name="core", subcore_axis_name="subcore")

    @pl.kernel(out_type=jax.ShapeDtypeStruct((n_out_rows, LANES), rows.dtype), mesh=mesh, scratch_types=[])
    def scatter_rows(x_hbm, i0_hbm, i1_hbm, o_hbm):
        def body(x_vmem, i0_vmem, i1_vmem):
            pltpu.sync_copy(x_vmem, o_hbm.at[i0_vmem.at[0]])
            pltpu.sync_copy(x_vmem, o_hbm.at[i1_vmem.at[0]])

        pltpu.emit_pipeline(
            body,
            grid=(n // SC_WINDOW,),
            in_specs=[pl.BlockSpec((SC_WINDOW, LANES), index_map=lambda i: (i, 0)),
                      pl.BlockSpec((1, SC_WINDOW), index_map=lambda i: (0, i)),
                      pl.BlockSpec((1, SC_WINDOW), index_map=lambda i: (0, i))],
            out_specs=[],
            core_axis_name="subcore",
            dimension_semantics=(pltpu.PARALLEL,),
        )(x_hbm, i0_hbm, i1_hbm)

    return scatter_rows(rows, idx0, idx1)


def _moe_plan(route_t, counts, bm):
    n_tok = route_t.shape[1]
    n_asg = n_tok * TOP_K
    e = route_t[:TOP_K].astype(jnp.int32)
    rank = route_t[4:4 + TOP_K].astype(jnp.int32)
    counts = counts[0, :N_EXPERTS].astype(jnp.int32)
    padded = (counts + bm - 1) // bm * bm
    pad_end = jnp.cumsum(padded)
    pad_start = pad_end - padded
    start_of = jnp.zeros_like(e)
    for j in range(N_EXPERTS):
        start_of = jnp.where(e == j, pad_start[j], start_of)
    dest = start_of + rank
    piece = jnp.arange(n_tok * ROW_PIECES, dtype=jnp.int32) % ROW_PIECES
    dest_rows = jnp.repeat(dest, ROW_PIECES, axis=1) * ROW_PIECES + piece[None, :]
    n_slots = n_asg + N_EXPERTS * bm
    n_blk = n_slots // bm
    blk_start = jnp.arange(n_blk, dtype=jnp.int32) * bm
    blk_e = jnp.minimum(jnp.sum((pad_end[None, :] <= blk_start[:, None]).astype(jnp.int32), axis=1),
                        N_EXPERTS - 1)
    n_used = (pad_end[-1:] // bm).astype(jnp.int32)
    filled_end = pad_start + counts
    end_of_blk = jnp.sum(jnp.where(blk_e[:, None] == jnp.arange(N_EXPERTS, dtype=jnp.int32)[None, :],
                                   filled_end[None, :], 0), axis=1)
    blk_valid = jnp.clip(end_of_blk - blk_start, 0, bm).astype(jnp.int32)
    return dest, dest_rows, blk_e, n_used, blk_valid, n_slots


def _head_groups(parts):
    rows = parts[0].shape[0]
    used = sum(p.shape[-1] for p in parts)
    pad = jnp.zeros((rows, H_B, LANES - used), parts[0].dtype)
    return jnp.concatenate(list(parts) + [pad], axis=-1).reshape(rows, H_B * LANES)


def _layer_weights(l, g_attn, w_in, g_subln, w_a, g_q_lat, w_uq, g_kv_lat, w_uk, w_uv, w_b, w_o, g_ffn,
                   w_group, b_group, w_router, b_router):
    wt = jnp.swapaxes(w_in, 1, 2)[l]
    o_kr = 3 * COLS_A + Q_LORA + KV_LORA
    w_pack = _transpose_cast(jnp.concatenate(
        [wt[:o_kr], jnp.pad(wt[o_kr:o_kr + QK_ROPE], ((0, KR_PAD - QK_ROPE), (0, 0))), wt[o_kr + QK_ROPE:]],
        axis=0))
    wq = w_uq[l].reshape(Q_LORA, H_B, QK_NOPE + QK_ROPE)
    w_uq_p = jnp.concatenate([wq[:, :, :QK_NOPE].reshape(Q_LORA, COLS_QN),
                              wq[:, :, QK_NOPE:].reshape(Q_LORA, COLS_QR)], axis=1).astype(BF16)
    w_uq_g = _head_groups([wq]).astype(BF16)
    w_kn = _head_groups([w_uk[l]]).astype(BF16)
    place = jnp.pad(jnp.eye(QK_ROPE, dtype=F32), ((0, KR_PAD - QK_ROPE), (0, 0)))
    w_krp = _head_groups([jnp.zeros((KR_PAD, H_B, QK_NOPE), F32),
                          jnp.broadcast_to(place[:, None, :], (KR_PAD, H_B, QK_ROPE))]).astype(BF16)
    w_uv_all = w_uv[l].reshape(KV_LORA, COLS_VB).astype(BF16)
    wuk_t = jnp.transpose(w_uk[l], (1, 2, 0)).astype(BF16)
    wuv = jnp.transpose(w_uv[l], (1, 0, 2))
    eye = jnp.eye(H_B, dtype=F32)
    wz = (wuv[:, :, None, :] * eye[:, None, :, None]).reshape(H_B, KV_LORA, COLS_VB).astype(BF16)
    w_route = jnp.concatenate(
        [w_group[l], jnp.transpose(w_router[l], (1, 0, 2)).reshape(D_MODEL, N_EXPERTS),
         jnp.zeros((D_MODEL, ROUTE_COLS - N_GROUPS - N_EXPERTS), F32)], axis=1)
    w_route_hi = w_route.astype(BF16)
    w_route = jnp.stack([w_route_hi, (w_route - w_route_hi.astype(F32)).astype(BF16)])
    b_route = jnp.concatenate([b_group[l], b_router[l].reshape(N_EXPERTS),
                               jnp.zeros((ROUTE_COLS - N_GROUPS - N_EXPERTS,), F32)])[None, :]
    return dict(g_attn=g_attn[l][None, :], w_pack=w_pack, g_q=g_q_lat[l][None, :], g_kv=g_kv_lat[l][None, :],
                w_uq_p=w_uq_p, w_uq_g=w_uq_g, w_kn=w_kn, w_krp=w_krp, w_uv_all=w_uv_all, wuk_t=wuk_t, wz=wz,
                g_subln=g_subln[l][None, :], g_subln_col=g_subln[l][:, None],
                wa=w_a[l].astype(BF16), wb=w_b[l].astype(BF16), wo=w_o[l].astype(BF16),
                g_ffn=g_ffn[l][None, :], w_route=w_route, b_route=b_route)


def _trunk(x, pos_tile, n_pos_tiles, tm, caches, lws, lams, wgu, wdn, g_final, *, batch, seq, bm):
    depth = len(lws)
    t = batch * seq
    x = x.reshape(t, D_MODEL)
    tabs = _rope_tables(pos_tile)
    prompt = caches is None
    ks, vs, cs, rs = [], [], [], []
    new_cache = None
    for l in range(depth):
        w = lws[l]
        out_scale = 1.0 - (0.8 - 0.6 * math.exp(-0.3 * l))
        outs = _inproj(x, w, tabs, tm=tm, n_pos_tiles=n_pos_tiles, prompt=prompt, layer=l, depth=depth,
                       batch=batch, prev=new_cache)
        k, v, ckv, kr = outs[:4]
        if prompt:
            new_cache = (k, v, ckv, kr)
            qat, kb, vat, qbt, kq, vbt, gates = outs[4:]
            oa = _attn_t(qat, kb, vat, batch=batch, seq=seq, n_heads=H_A, dv=DV_A, n_maps=2, lam=lams[l],
                         g=w['g_subln_col'], out_scale=out_scale, name="diff_attn_prompt")
            ob = _attn_t(qbt, kq, vbt, batch=batch, seq=seq, n_heads=H_B, dv=V_B, n_maps=1,
                         name="mla_attn_prompt")
        else:
            qa, kb, vb, ckvb, krb, qn, qr, gates = outs[4:]
            kc, vc, cc, rc, past = caches
            oa = _diff_sample(lams[l], qa, kc, vc, kb, vb, w['g_subln'], layer=l, batch=batch, seq=seq,
                              past=past, out_scale=out_scale)
            ob = _mla_sample(qn, qr, cc, rc, ckvb, krb, w['wuk_t'], w['wz'], layer=l, batch=batch, seq=seq,
                             past=past)
        x1, h_rows, route, counts, route_t = _merge(oa, ob, gates, x, w, tm=tm, tiled_t=prompt)
        dest, dest_rows, blk_e, n_used, blk_valid, n_slots = _moe_plan(route_t, counts, bm)
        xs_rows = _sc_dispatch(h_rows, dest_rows[0:1], dest_rows[1:2], n_slots * ROW_PIECES)
        y_slots = _experts(blk_e, n_used, blk_valid, xs_rows, wgu, wdn, layer=l, bm=bm)
        y01 = y_slots.at[dest.reshape(TOP_K * t)].get(mode="promise_in_bounds")
        x = _combine(x1, y01, route, g_final, tm=tm, final_norm=(l == depth - 1))
        ks.append(k)
        vs.append(v)
        cs.append(ckv)
        rs.append(kr)
    y = x.reshape(batch, seq, D_MODEL)
    if prompt:
        k, v, ckv, kr = new_cache
        return (y,
                jnp.transpose(k.reshape(depth, batch, H_A, 2, DH_A, seq), (0, 1, 5, 2, 3, 4)),
                v.reshape(depth, batch, seq, H_A, DV_A),
                ckv.reshape(depth, batch, seq, KV_LORA),
                jnp.transpose(kr.reshape(depth, batch, QK_ROPE, seq), (0, 1, 3, 2)))
    return (y,
            jnp.stack(ks).reshape(depth, batch, seq, H_A, 2, DH_A),
            jnp.stack(vs).reshape(depth, batch, seq, H_A, DV_A),
            jnp.stack(cs).reshape(depth, batch, seq, KV_LORA),
            jnp.stack(rs).reshape(depth, batch, seq, QK_ROPE))


def kernel(x_prompt, x_sample, cache_diff_k, cache_diff_v, cache_mla_ckv, cache_mla_krope, g_attn, w_in,
           lambda_q1, lambda_k1, lambda_q2, lambda_k2, g_subln, w_a, g_q_lat, w_uq, g_kv_lat, w_uk, w_uv, w_b,
           w_o, g_ffn, w_group, b_group, w_router, b_router, w_gate_up, w_down, g_final):
    depth = w_in.shape[0]
    batch, seq, _ = x_prompt.shape
    dec_batch, dec_seq, _ = x_sample.shape
    past = cache_diff_k.shape[2]
    assert dec_seq == CHUNK and past % CHUNK == 0, "sample frames must form exactly one new chunk"
    assert seq % TQ == 0

    lws = [_layer_weights(l, g_attn, w_in, g_subln, w_a, g_q_lat, w_uq, g_kv_lat, w_uk, w_uv, w_b, w_o, g_ffn,
                          w_group, b_group, w_router, b_router) for l in range(depth)]
    lams = []
    for l in range(depth):
        lam_init = 0.8 - 0.6 * math.exp(-0.3 * l)
        lam = (jnp.exp(jnp.sum(lambda_q1[l] * lambda_k1[l])) - jnp.exp(jnp.sum(lambda_q2[l] * lambda_k2[l]))
               + lam_init)
        lams.append(lam.reshape(1).astype(F32))
    wgu = w_gate_up.reshape(depth * N_EXPERTS, D_MODEL, 2 * D_EXPERT)
    wdn = w_down.reshape(depth * N_EXPERTS, D_EXPERT, D_MODEL)
    g_fin = g_final[None, :]

    tm_p = min(512, seq)
    outs_p = _trunk(x_prompt, jnp.arange(seq, dtype=jnp.int32), seq // tm_p, tm_p, None, lws, lams, wgu, wdn,
                    g_fin, batch=batch, seq=seq, bm=min(512, batch * seq))

    tm_s = min(512, dec_batch * dec_seq)
    pos_tile = jnp.tile(past + jnp.arange(dec_seq, dtype=jnp.int32), tm_s // dec_seq)
    caches = (jnp.transpose(cache_diff_k, (0, 1, 3, 4, 5, 2)).reshape(depth * dec_batch * COLS_A, past),
              cache_diff_v.reshape(depth * dec_batch, past * H_A, DV_A),
              cache_mla_ckv.reshape(depth * dec_batch, past, KV_LORA),
              jnp.transpose(cache_mla_krope, (0, 1, 3, 2)).reshape(depth * dec_batch * QK_ROPE, past), past)
    outs_s = _trunk(x_sample, pos_tile, 1, tm_s, caches, lws, lams, wgu, wdn, g_fin,
                    batch=dec_batch, seq=dec_seq, bm=min(128, dec_batch * dec_seq))

    return (outs_p[0], outs_s[0]) + outs_p[1:] + outs_s[1:]
```

```python
import functools
import math

import jax
import jax.numpy as jnp
from jax import lax
from jax.experimental import pallas as pl
from jax.experimental.pallas import tpu as pltpu
from jax.experimental.pallas import tpu_sc as plsc

D_MODEL = 1024
CHUNK = 64
ROPE_THETA = 10000.0
EPS = 1e-6
H_A = 4
DH_A = 64
DV_A = 2 * DH_A
H_B = 8
Q_LORA = 384
KV_LORA = 256
QK_NOPE = 64
QK_ROPE = 32
V_B = 64
N_GROUPS = 4
EXPERTS_PER_GROUP = 8
N_EXPERTS = N_GROUPS * EXPERTS_PER_GROUP
TOP_K = 2
D_EXPERT = 512

COLS_A = H_A * 2 * DH_A
COLS_QR = H_B * QK_ROPE
COLS_QN = H_B * QK_NOPE
COLS_VB = H_B * V_B
LANES = 128
COLS_QB = H_B * LANES
KR_PAD = LANES
PACK_COLS = 3 * COLS_A + Q_LORA + KV_LORA + KR_PAD + 2 * D_MODEL
ROUTE_COLS = LANES
ROUTE_ROWS = 8
PACKED_COLS = D_MODEL // 2
ROW_PIECES = PACKED_COLS // LANES
SC_WINDOW = 128
TQ = 512
TK = 256
ONES_ROWS = 16
MERGE_SUB = 256
VMEM_LIMIT = 56 * 1024 * 1024
LOG2E = 1.4426950408889634
SCALE_A = DH_A ** -0.5 * LOG2E
SCALE_B = (QK_NOPE + QK_ROPE) ** -0.5 * LOG2E

F32 = jnp.float32
BF16 = jnp.bfloat16
NEG_INF = float("-inf")


def _cparams(sem):
    return pltpu.CompilerParams(dimension_semantics=sem, vmem_limit_bytes=VMEM_LIMIT)


def _rms(x, g):
    return x * lax.rsqrt(jnp.mean(x * x, axis=-1, keepdims=True) + EPS) * g


def _widen(tab, cols):
    reps = cols // LANES
    return tab if reps == 1 else jnp.concatenate([tab] * reps, axis=-1)


def _rope(x, cos, sin_signed, chunk):
    n = x.shape[-1]
    half = chunk // 2
    lane = lax.broadcasted_iota(jnp.int32, x.shape, 1)
    fwd = pltpu.roll(x, n - half, 1)
    bwd = pltpu.roll(x, half, 1)
    swapped = jnp.where((lane & (chunk - 1)) < half, fwd, bwd)
    return x * _widen(cos, n) + swapped * _widen(sin_signed, n)


def _dot_nt(a, b):
    return lax.dot_general(a, b, (((1,), (1,)), ((), ())), preferred_element_type=F32)


def _store_tiles_t(ref, val):
    tile = ref.shape[2]
    for r in range(ref.shape[0]):
        ref[r] = val[r * tile:(r + 1) * tile, :].T.astype(ref.dtype)


def _store_vt_ones(ref, val, n_heads, dv):
    dva = dv + ONES_ROWS
    ones = jnp.ones((ONES_ROWS, TK), ref.dtype)
    for r in range(ref.shape[0]):
        vt = val[r * TK:(r + 1) * TK, :].T.astype(ref.dtype)
        for h in range(n_heads):
            ref[r, h * dva:h * dva + dv, :] = vt[h * dv:(h + 1) * dv, :]
            ref[r, h * dva + dv:(h + 1) * dva, :] = ones


def _store_packed_rows(ref, row0, xf):
    rows = xf.shape[0]
    bits = pltpu.bitcast(xf, jnp.uint32)
    words = bits[:, :PACKED_COLS] | (bits[:, PACKED_COLS:] >> 16)
    for c in range(ROW_PIECES):
        ref[pl.ds(row0 * ROW_PIECES + c, rows, stride=ROW_PIECES), :] = words[:, c * LANES:(c + 1) * LANES]


def _load_packed_rows(ref, rows):
    pieces = [ref[pl.ds(c, rows, stride=ROW_PIECES), :] for c in range(ROW_PIECES)]
    hi = [pltpu.bitcast(p & jnp.uint32(0xFFFF0000), F32) for p in pieces]
    lo = [pltpu.bitcast(p << 16, F32) for p in pieces]
    return jnp.concatenate(hi + lo, axis=1).astype(BF16)


def _load_tiles_t(ref):
    return jnp.concatenate([ref[r].astype(F32).T for r in range(ref.shape[0])], axis=0).astype(BF16)


def _inproj_kernel(*refs, prompt):
    (x_ref, g_ref, w_ref, gq_ref, gkv_ref, wuq_ref, c64_ref, s64_ref, c32_ref, s32_ref) = refs[:10]
    x = x_ref[...]
    hb = _rms(x, g_ref[...]).astype(BF16)

    def proj(lo, hi):
        return jnp.dot(hb, w_ref[:, lo:hi], preferred_element_type=F32)

    c64, s64 = c64_ref[...], s64_ref[...]
    c32, s32 = c32_ref[...], s32_ref[...]
    o = 0
    qa = _rope(proj(o, o + COLS_A), c64, s64, DH_A) * SCALE_A
    o += COLS_A
    ka = _rope(proj(o, o + COLS_A), c64, s64, DH_A)
    o += COLS_A
    va = proj(o, o + COLS_A)
    o += COLS_A
    cq = _rms(proj(o, o + Q_LORA), gq_ref[...]).astype(BF16)
    o += Q_LORA
    qfull = jnp.dot(cq, wuq_ref[...], preferred_element_type=F32) * SCALE_B
    ckv = _rms(proj(o, o + KV_LORA), gkv_ref[...])
    ckvb = ckv.astype(BF16)
    o += KV_LORA
    kr_pad = _rope(proj(o, o + KR_PAD), c32, s32, QK_ROPE)
    o += KR_PAD
    gates = jax.nn.sigmoid(proj(o, o + 2 * D_MODEL)).astype(BF16)

    if prompt:
        cq_ref, sq_ref, wkn_ref, wkr_ref, wuv_ref = refs[10:15]
        (k_ref, v_ref, ckv_ref, kr_ref, qat_ref, kb_ref, vat_ref, qbt_ref, kq_ref, vbt_ref,
         gate_ref) = refs[len(refs) - 11:]
        k_ref[...] = ka.T
        for h in range(H_A):
            v_ref[pl.ds(h, va.shape[0], stride=H_A), :] = va[:, h * DV_A:(h + 1) * DV_A]
        ckv_ref[...] = ckv
        kr_ref[...] = kr_pad.T[:QK_ROPE, :]
        _store_tiles_t(qat_ref, qa)
        kb_ref[...] = ka.astype(BF16)
        _store_vt_ones(vat_ref, va, H_A, DV_A)
        _store_tiles_t(qbt_ref, _rope(qfull, cq_ref[...], sq_ref[...], QK_ROPE))
        kq = (jnp.dot(ckvb, wkn_ref[...], preferred_element_type=F32)
              + jnp.dot(kr_pad.astype(BF16), wkr_ref[...], preferred_element_type=F32))
        kq_ref[...] = kq.astype(BF16)
        _store_vt_ones(vbt_ref, jnp.dot(ckvb, wuv_ref[...], preferred_element_type=F32), H_B, V_B)
    else:
        (k_ref, v_ref, ckv_ref, kr_ref, qa_ref, kb_ref, vb_ref, ckvb_ref, krb_ref, qn_ref, qr_ref,
         gate_ref) = refs[10:]
        qa_ref[...] = qa.astype(BF16)
        kb_ref[...] = ka.astype(BF16)
        vb_ref[...] = va.astype(BF16)
        ckvb_ref[...] = ckvb
        krb_ref[...] = kr_pad[:, :QK_ROPE].astype(BF16)
        qn_ref[...] = qfull[:, :COLS_QN].astype(BF16)
        qr_ref[...] = _rope(qfull[:, COLS_QN:], c32, s32, QK_ROPE).astype(BF16)
        k_ref[...] = ka
        v_ref[...] = va
        ckv_ref[...] = ckv
        kr_ref[...] = kr_pad[:, :QK_ROPE]
    gate_ref[...] = gates


def _inproj(x, w, tabs, *, tm, n_pos_tiles, prompt, layer=0, depth=1, batch=1, prev=None):
    t = x.shape[0]
    nt = t // tm
    tok = lambda i: (i, 0)
    full = lambda i: (0, 0)
    pos = lambda i: (i % n_pos_tiles, 0)
    tile3 = lambda i: (i, 0, 0)
    wuq = w['w_uq_g'] if prompt else w['w_uq_p']
    ins = [x, w['g_attn'], w['w_pack'], w['g_q'], w['g_kv'], wuq, tabs['c64'], tabs['s64'], tabs['c32'], tabs['s32']]
    in_specs = [pl.BlockSpec((tm, D_MODEL), tok), pl.BlockSpec((1, D_MODEL), full),
                pl.BlockSpec((D_MODEL, PACK_COLS), full), pl.BlockSpec((1, Q_LORA), full),
                pl.BlockSpec((1, KV_LORA), full), pl.BlockSpec(wuq.shape, full)] + [pl.BlockSpec((tm, LANES), pos)] * 4
    if prompt:
        seq = t // batch
        per_b = seq // tm
        seq_minor = lambda i: (layer * batch + i // per_b, i % per_b)
        tok_l = lambda i: (layer * nt + i, 0)
        out_specs = [pl.BlockSpec((COLS_A, tm), seq_minor), pl.BlockSpec((tm * H_A, DV_A), tok_l),
                     pl.BlockSpec((tm, KV_LORA), tok_l), pl.BlockSpec((QK_ROPE, tm), seq_minor)]
        out_shape = [jax.ShapeDtypeStruct((depth * batch * COLS_A, seq), F32),
                     jax.ShapeDtypeStruct((depth * t * H_A, DV_A), F32),
                     jax.ShapeDtypeStruct((depth * t, KV_LORA), F32),
                     jax.ShapeDtypeStruct((depth * batch * QK_ROPE, seq), F32)]
    else:
        leaves = [(COLS_A, F32), (COLS_A, F32), (KV_LORA, F32), (QK_ROPE, F32)]
        out_specs = [pl.BlockSpec((tm, c), tok) for c, _ in leaves]
        out_shape = [jax.ShapeDtypeStruct((t, c), d) for c, d in leaves]

    def add2d(c):
        out_specs.append(pl.BlockSpec((tm, c), tok))
        out_shape.append(jax.ShapeDtypeStruct((t, c), BF16))

    def add3d(c, tile):
        out_specs.append(pl.BlockSpec((tm // tile, c, tile), tile3))
        out_shape.append(jax.ShapeDtypeStruct((t // tile, c, tile), BF16))

    if prompt:
        ins += [tabs['cq'], tabs['sq'], w['w_kn'], w['w_krp'], w['w_uv_all']]
        in_specs += [pl.BlockSpec((tm, LANES), pos)] * 2 + [pl.BlockSpec(w[n].shape, full)
                                                             for n in ('w_kn', 'w_krp', 'w_uv_all')]
        add3d(COLS_A, TQ), add2d(COLS_A), add3d(H_A * (DV_A + ONES_ROWS), TK)
        add3d(COLS_QB, TQ), add2d(COLS_QB), add3d(H_B * (V_B + ONES_ROWS), TK)
    else:
        for c in (COLS_A, COLS_A, COLS_A, KV_LORA, QK_ROPE, COLS_QN, COLS_QR):
            add2d(c)
    add2d(2 * D_MODEL)
    aliases = {}
    if prev is not None:
        aliases = {len(ins) + n: n for n in range(len(prev))}
        ins += list(prev)
        in_specs += [pl.BlockSpec(memory_space=pl.ANY)] * len(prev)
    return pl.pallas_call(
        functools.partial(_inproj_kernel, prompt=prompt),
        grid=(nt,),
        in_specs=in_specs,
        out_specs=out_specs,
        out_shape=out_shape,
        input_output_aliases=aliases,
        compiler_params=_cparams(("parallel",)),
        name="inproj_prompt" if prompt else "inproj_sample",
    )(*ins)


def _attn_t_kernel(*refs, n_heads, dv, n_maps, out_scale):
    if n_maps == 2:
        lam_ref, qt_ref, k_ref, vt_ref, g_ref, o_ref, q_sc, m_sc, acc_sc, sta_sc, stb_sc, mxa_sc, mxb_sc = refs
    else:
        qt_ref, k_ref, vt_ref, o_ref, m_sc, acc_sc, sta_sc, stb_sc, mxa_sc, mxb_sc = refs
    i = pl.program_id(1)
    dva = dv + ONES_ROWS
    width = n_maps * TQ
    heads = range(n_heads)
    if n_maps == 2:
        for h in heads:
            qt = qt_ref[0, h * LANES:(h + 1) * LANES, :]
            row = lax.broadcasted_iota(jnp.int32, qt.shape, 0)
            zero = jnp.zeros_like(qt)
            q_sc[h, :, :TQ] = jnp.where(row < DH_A, qt, zero)
            q_sc[h, :, TQ:] = jnp.where(row >= DH_A, qt, zero)
    m_sc[...] = jnp.full(m_sc.shape, NEG_INF, F32)
    acc_sc[...] = jnp.zeros(acc_sc.shape, F32)
    rel = ((lax.broadcasted_iota(jnp.int32, (TK, width), 1) & (TQ - 1)) // CHUNK
           - lax.broadcasted_iota(jnp.int32, (TK, width), 0) // CHUNK)

    def score_head(j, h, st_ref, mx_ref):
        rows = pl.ds(pl.multiple_of(j * TK, TK), TK)
        st = jnp.dot(k_ref[rows, h * LANES:(h + 1) * LANES],
                     q_sc[h] if n_maps == 2 else qt_ref[0, h * LANES:(h + 1) * LANES, :],
                     preferred_element_type=F32)
        st_ref[h] = st
        mx_ref[h] = jnp.max(st, axis=0, keepdims=True)

    def softmax_head(j, h, st_ref, mx_ref, masked):
        st = st_ref[h]
        if masked:
            st = jnp.where(rel >= j * (TK // CHUNK) - i * (TQ // CHUNK), st, NEG_INF)
            tile_max = jnp.max(st, axis=0, keepdims=True)
        else:
            tile_max = mx_ref[h]
        m = m_sc[h]
        m_new = jnp.maximum(m, tile_max)
        m_sc[h] = m_new
        return jnp.exp2(st - m_new).astype(BF16), jnp.exp2(m - m_new)

    def value_head(j, h, pt, alpha):
        pv = jnp.dot(vt_ref[j, h * dva:(h + 1) * dva, :], pt, preferred_element_type=F32)
        acc_sc[h] = alpha * acc_sc[h] + pv

    def stage(j, cur, masked, nxt=None):
        pending = None
        for h in heads:
            if nxt is not None:
                score_head(j + 1, h, *nxt)
            pt, alpha = softmax_head(j, h, *cur, masked)
            if pending is not None:
                value_head(j, *pending)
            pending = (h, pt, alpha)
        value_head(j, *pending)

    buf_a, buf_b = (sta_sc, mxa_sc), (stb_sc, mxb_sc)

    def full_pair(p, carry):
        j = 2 * p
        stage(j, buf_a, False, buf_b)
        stage(j + 1, buf_b, False, buf_a)
        return carry

    assert TQ == 2 * TK
    for h in heads:
        score_head(0, h, *buf_a)
    lax.fori_loop(0, i, full_pair, 0)
    stage(2 * i, buf_a, True, buf_b)
    stage(2 * i + 1, buf_b, True)


    for h in heads:
        acc = acc_sc[h]
        inv = acc[:dv] / acc[dv:dv + 1]
        if n_maps == 2:
            o = inv[:, :TQ] - lam_ref[0] * inv[:, TQ:]
            o = o * lax.rsqrt(jnp.mean(o * o, axis=0, keepdims=True) + EPS) * g_ref[...] * out_scale
        else:
            o = inv
        o_ref[0, h * dv:(h + 1) * dv, :] = o.astype(o_ref.dtype)


def _attn_t(qt, k, vt, *, batch, seq, n_heads, dv, n_maps, lam=None, g=None, out_scale=1.0, name):
    nq = seq // TQ
    dva = dv + ONES_ROWS
    width = n_maps * TQ
    once = pl.Buffered(1)
    ins, in_specs, scratch = [], [], []
    if n_maps == 2:
        ins.append(lam)
        in_specs.append(pl.BlockSpec(memory_space=pltpu.SMEM))
        scratch.append(pltpu.VMEM((n_heads, LANES, width), BF16))
    ins += [qt, k, vt]
    in_specs += [pl.BlockSpec((1, n_heads * LANES, TQ), lambda b, i: (b * nq + i, 0, 0)),
                 pl.BlockSpec((seq, n_heads * LANES), lambda b, i: (b, 0), pipeline_mode=once),
                 pl.BlockSpec((seq // TK, n_heads * dva, TK), lambda b, i: (b, 0, 0), pipeline_mode=once)]
    if n_maps == 2:
        ins.append(g)
        in_specs.append(pl.BlockSpec((dv, 1), lambda b, i: (0, 0)))
    return pl.pallas_call(
        functools.partial(_attn_t_kernel, n_heads=n_heads, dv=dv, n_maps=n_maps, out_scale=out_scale),
        grid=(batch, nq),
        in_specs=in_specs,
        out_specs=pl.BlockSpec((1, n_heads * dv, TQ), lambda b, i: (b * nq + i, 0, 0)),
        out_shape=jax.ShapeDtypeStruct((batch * nq, n_heads * dv, TQ), BF16),
        scratch_shapes=scratch + [pltpu.VMEM((n_heads, 1, width), F32), pltpu.VMEM((n_heads, dva, width), F32),
                                  pltpu.VMEM((n_heads, TK, width), F32), pltpu.VMEM((n_heads, TK, width), F32),
                                  pltpu.VMEM((n_heads, 1, width), F32), pltpu.VMEM((n_heads, 1, width), F32)],
        compiler_params=_cparams(("parallel", "arbitrary")),
        name=name,
    )(*ins)


def _flash_steps(ss, vs, m_sc, l_sc, acc_sc):
    ps, alphas = [], []
    for c, s in enumerate(ss):
        m_prev = m_sc[c]
        m_new = jnp.maximum(m_prev, jnp.max(s, axis=-1, keepdims=True))
        alpha = jnp.exp2(m_prev - m_new)
        cols = s.shape[-1]
        p = jnp.exp2(s - (_widen(m_new, cols) if cols % LANES == 0 else m_new[:, :cols]))
        l_sc[c] = alpha * l_sc[c] + jnp.sum(p, axis=-1, keepdims=True)
        m_sc[c] = m_new
        ps.append(p.astype(BF16))
        alphas.append(alpha)
    pvs = [jnp.dot(p, v, preferred_element_type=F32) for p, v in zip(ps, vs)]
    for c, pv in enumerate(pvs):
        acc_sc[c] = _widen(alphas[c], acc_sc.shape[-1]) * acc_sc[c] + pv


def _init_flash(m_sc, l_sc, acc_sc):
    m_sc[...] = jnp.full(m_sc.shape, NEG_INF, F32)
    l_sc[...] = jnp.zeros(l_sc.shape, F32)
    acc_sc[...] = jnp.zeros(acc_sc.shape, F32)


def _stack_maps(q):
    lane = lax.broadcasted_iota(jnp.int32, q.shape, 1)
    zero = jnp.zeros_like(q)
    return jnp.concatenate([jnp.where(lane < DH_A, q, zero), jnp.where(lane >= DH_A, q, zero)], axis=0)


def _diff_sample_kernel(lam_ref, q_ref, kc_ref, vc_ref, kn_ref, vn_ref, g_ref, o_ref, m_sc, l_sc, acc_sc,
                        *, tq, tk, n_cache_tiles, out_scale):
    heads = range(H_A)
    lanes = [slice(h * DV_A, (h + 1) * DV_A) for h in heads]
    qss = [_stack_maps(q_ref[:, lanes[h]]) for h in heads]
    _init_flash(m_sc, l_sc, acc_sc)

    def cache_scores(j):
        return [jnp.dot(qss[h], kc_ref[lanes[h], j * tk:(j + 1) * tk].astype(BF16), preferred_element_type=F32)
                for h in heads]

    ss = cache_scores(0)
    for j in range(n_cache_tiles):
        ss_next = (cache_scores(j + 1) if j + 1 < n_cache_tiles
                   else [_dot_nt(qss[h], kn_ref[:, lanes[h]]) for h in heads])
        vs = [vc_ref[0, pl.ds(j * tk * H_A + h, tk, stride=H_A), :].astype(BF16) for h in heads]
        _flash_steps(ss, vs, m_sc, l_sc, acc_sc)
        ss = ss_next
    _flash_steps(ss, [vn_ref[:, lanes[h]] for h in heads], m_sc, l_sc, acc_sc)
    for h in heads:
        inv = acc_sc[h] / l_sc[h]
        o = inv[:tq] - lam_ref[0] * inv[tq:]
        o_ref[:, lanes[h]] = (_rms(o, g_ref[...]) * out_scale).astype(o_ref.dtype)


def _diff_sample(lam, qa, kcache, vcache, kb, vb, g_subln, *, layer, batch, seq, past, out_scale):
    tk = min(512, past)
    smem = pl.BlockSpec(memory_space=pltpu.SMEM)
    new = pl.BlockSpec((seq, COLS_A), lambda b: (b, 0))
    kcache_spec = pl.BlockSpec((COLS_A, past), lambda b: (layer * batch + b, 0))
    vcache_spec = pl.BlockSpec((1, past * H_A, DV_A), lambda b: (layer * batch + b, 0, 0))
    return pl.pallas_call(
        functools.partial(_diff_sample_kernel, tq=seq, tk=tk, n_cache_tiles=past // tk, out_scale=out_scale),
        grid=(batch,),
        in_specs=[smem, new, kcache_spec, vcache_spec, new, new, pl.BlockSpec((1, DV_A), lambda b: (0, 0))],
        out_specs=new,
        out_shape=jax.ShapeDtypeStruct((batch * seq, COLS_A), BF16),
        scratch_shapes=[pltpu.VMEM((H_A, 2 * seq, LANES), F32), pltpu.VMEM((H_A, 2 * seq, LANES), F32),
                        pltpu.VMEM((H_A, 2 * seq, DV_A), F32)],
        compiler_params=_cparams(("parallel",)),
        name="diff_attn_sample",
    )(lam, qa, kcache, vcache, kb, vb, g_subln)


def _mla_sample_kernel(qn_ref, qr_ref, cc_ref, rc_ref, cn_ref, rn_ref, wuk_ref, wz_ref, o_ref,
                       ql_sc, qr_sc, m_sc, l_sc, acc_sc, *, tq, tk, n_cache_tiles):
    qn = qn_ref[...]
    qr = qr_ref[...].astype(F32)
    for h in range(H_B):
        ql = jnp.dot(qn[:, h * QK_NOPE:(h + 1) * QK_NOPE], wuk_ref[h], preferred_element_type=F32)
        ql_sc[h * tq:(h + 1) * tq, :] = ql.astype(BF16)
        qr_sc[h * tq:(h + 1) * tq, :] = qr[:, h * QK_ROPE:(h + 1) * QK_ROPE].astype(BF16)
    _init_flash(m_sc, l_sc, acc_sc)
    n_chains = m_sc.shape[0]
    half = H_B * tq // n_chains
    parts = [slice(c * half, (c + 1) * half) for c in range(n_chains)]

    def cache_tile(j):
        c = cc_ref[0, j * tk:(j + 1) * tk, :].astype(BF16)
        rt = rc_ref[:, j * tk:(j + 1) * tk].astype(BF16)
        return c, [_dot_nt(ql_sc[p, :], c) + jnp.dot(qr_sc[p, :], rt, preferred_element_type=F32) for p in parts]

    def new_tile():
        c = cn_ref[...]
        return c, [_dot_nt(ql_sc[p, :], c) + _dot_nt(qr_sc[p, :], rn_ref[...]) for p in parts]

    c, ss = cache_tile(0)
    for j in range(n_cache_tiles):
        c_next, ss_next = cache_tile(j + 1) if j + 1 < n_cache_tiles else new_tile()
        _flash_steps(ss, [c] * n_chains, m_sc, l_sc, acc_sc)
        c, ss = c_next, ss_next
    _flash_steps(ss, [c] * n_chains, m_sc, l_sc, acc_sc)
    ob = jnp.zeros(o_ref.shape, F32)
    heads_per_chain = H_B // n_chains
    for h in range(H_B):
        c, r = divmod(h, heads_per_chain)
        o_lat = (acc_sc[c, r * tq:(r + 1) * tq, :] / _widen(l_sc[c, r * tq:(r + 1) * tq, :], KV_LORA)).astype(BF16)
        ob += jnp.dot(o_lat, wz_ref[h], preferred_element_type=F32)
    o_ref[...] = ob.astype(o_ref.dtype)


def _mla_sample(qn, qr, ccache, rcache, ckvb, krb, wuk_t, wz, *, layer, batch, seq, past):
    tk = min(512, past)
    rows = H_B * seq
    return pl.pallas_call(
        functools.partial(_mla_sample_kernel, tq=seq, tk=tk, n_cache_tiles=past // tk),
        grid=(batch,),
        in_specs=[pl.BlockSpec((seq, COLS_QN), lambda b: (b, 0)),
                  pl.BlockSpec((seq, COLS_QR), lambda b: (b, 0)),
                  pl.BlockSpec((1, past, KV_LORA), lambda b: (layer * batch + b, 0, 0)),
                  pl.BlockSpec((QK_ROPE, past), lambda b: (layer * batch + b, 0)),
                  pl.BlockSpec((seq, KV_LORA), lambda b: (b, 0)),
                  pl.BlockSpec((seq, QK_ROPE), lambda b: (b, 0)),
                  pl.BlockSpec((H_B, QK_NOPE, KV_LORA), lambda b: (0, 0, 0)),
                  pl.BlockSpec((H_B, KV_LORA, COLS_VB), lambda b: (0, 0, 0))],
        out_specs=pl.BlockSpec((seq, COLS_VB), lambda b: (b, 0)),
        out_shape=jax.ShapeDtypeStruct((batch * seq, COLS_VB), BF16),
        scratch_shapes=[pltpu.VMEM((rows, KV_LORA), BF16), pltpu.VMEM((rows, QK_ROPE), BF16),
                        pltpu.VMEM((2, rows // 2, LANES), F32), pltpu.VMEM((2, rows // 2, LANES), F32),
                        pltpu.VMEM((2, rows // 2, KV_LORA), F32)],
        compiler_params=_cparams(("parallel",)),
        name="mla_attn_sample",
    )(qn, qr, ccache, rcache, ckvb, krb, wuk_t, wz)


def _merge_kernel(oa_ref, ob_ref, gate_ref, x_ref, wa_ref, wb_ref, wo_ref, gf_ref, wr_ref, br_ref, tri_ref,
                  x1_ref, h_ref, route_ref, count_ref, routet_ref, run_sc, *, tiled_t):
    oa = _load_tiles_t(oa_ref) if tiled_t else oa_ref[...]
    ob = _load_tiles_t(ob_ref) if tiled_t else ob_ref[...]
    tm = x_ref.shape[0]
    subs = [slice(r * MERGE_SUB, (r + 1) * MERGE_SUB) for r in range(tm // MERGE_SUB)]
    yas = [jnp.dot(oa[s], wa_ref[...], preferred_element_type=F32) for s in subs]
    ybs = [jnp.dot(ob[s], wb_ref[...], preferred_element_type=F32) for s in subs]
    merged = []
    for s, ya, yb in zip(subs, yas, ybs):
        gates = gate_ref[s, :].astype(F32)
        merged.append((gates[:, :D_MODEL] * ya + gates[:, D_MODEL:] * yb).astype(BF16))
    x1s = [x_ref[s, :] + jnp.dot(mg, wo_ref[...], preferred_element_type=F32) for s, mg in zip(subs, merged)]
    logit_parts = []
    for s, x1 in zip(subs, x1s):
        x1_ref[s, :] = x1
        h = _rms(x1, gf_ref[...])
        h_hi = h.astype(BF16)
        _store_packed_rows(h_ref, s.start, h_hi.astype(F32))
        h_lo = (h - h_hi.astype(F32)).astype(BF16)
        logit_parts.append(jnp.dot(h_hi, wr_ref[0], preferred_element_type=F32)
                           + jnp.dot(h_lo, wr_ref[0], preferred_element_type=F32)
                           + jnp.dot(h_hi, wr_ref[1], preferred_element_type=F32))

    logits = jnp.concatenate(logit_parts, axis=0) + br_ref[...]
    lane = lax.broadcasted_iota(jnp.int32, logits.shape, 1)
    big = jnp.int32(ROUTE_COLS)

    def top1(mask):
        v = jnp.max(jnp.where(mask, logits, NEG_INF), axis=-1, keepdims=True)
        idx = jnp.min(jnp.where(mask & (logits == v), lane, big), axis=-1, keepdims=True)
        return v, idx

    gmask = lane < N_GROUPS
    gmax, gidx = top1(gmask)
    g_w = 1.0 / jnp.sum(jnp.where(gmask, jnp.exp(logits - gmax), 0.0), axis=-1, keepdims=True)
    lo = N_GROUPS + gidx * EXPERTS_PER_GROUP
    emask = (lane >= lo) & (lane < lo + EXPERTS_PER_GROUP)
    v1, i1 = top1(emask)
    v2, i2 = top1(emask & (lane != i1))
    e2 = jnp.exp(v2 - v1)
    w1 = g_w / (1.0 + e2)
    w2 = g_w * e2 / (1.0 + e2)
    @pl.when(pl.program_id(0) == 0)
    def _():
        run_sc[...] = jnp.zeros(run_sc.shape, F32)

    e1 = i1 - N_GROUPS
    e2i = i2 - N_GROUPS
    picks = jnp.where((lane == e1) | (lane == e2i), 1.0, 0.0)
    before = jnp.dot(tri_ref[...], picks.astype(BF16), preferred_element_type=F32) + run_sc[...]
    rank1 = jnp.sum(jnp.where(lane == e1, before, 0.0), axis=-1, keepdims=True)
    rank2 = jnp.sum(jnp.where(lane == e2i, before, 0.0), axis=-1, keepdims=True)
    run_sc[...] = run_sc[...] + jnp.sum(picks, axis=0, keepdims=True)
    count_ref[...] = run_sc[...]

    vals = [e1.astype(F32), e2i.astype(F32), w1, w2, rank1, rank2]
    route = jnp.zeros(logits.shape, F32)
    for n, v in enumerate(vals):
        route = jnp.where(lane == n, v, route)
    route_ref[...] = route
    routet_ref[...] = route.T[:routet_ref.shape[0], :]


def _merge(oa, ob, gates, x, w, *, tm, tiled_t):
    t = x.shape[0]
    tok = lambda i: (i, 0)
    full = lambda i: (0, 0)
    if tiled_t:
        o_specs = [pl.BlockSpec((tm // TQ, COLS_A, TQ), lambda i: (i, 0, 0)),
                   pl.BlockSpec((tm // TQ, COLS_VB, TQ), lambda i: (i, 0, 0))]
    else:
        o_specs = [pl.BlockSpec((tm, COLS_A), tok), pl.BlockSpec((tm, COLS_VB), tok)]
    return pl.pallas_call(
        functools.partial(_merge_kernel, tiled_t=tiled_t),
        grid=(t // tm,),
        in_specs=o_specs + [
            pl.BlockSpec((tm, 2 * D_MODEL), tok), pl.BlockSpec((tm, D_MODEL), tok),
            pl.BlockSpec((COLS_A, D_MODEL), full), pl.BlockSpec((COLS_VB, D_MODEL), full),
            pl.BlockSpec((D_MODEL, D_MODEL), full), pl.BlockSpec((1, D_MODEL), full),
            pl.BlockSpec((2, D_MODEL, ROUTE_COLS), lambda i: (0, 0, 0)), pl.BlockSpec((1, ROUTE_COLS), full),
            pl.BlockSpec((tm, tm), full)],
        out_specs=[pl.BlockSpec((tm, D_MODEL), tok), pl.BlockSpec((tm * ROW_PIECES, LANES), tok),
                   pl.BlockSpec((tm, ROUTE_COLS), tok), pl.BlockSpec((1, ROUTE_COLS), full),
                   pl.BlockSpec((ROUTE_ROWS, tm), lambda i: (0, i))],
        out_shape=[jax.ShapeDtypeStruct((t, D_MODEL), F32),
                   jax.ShapeDtypeStruct((t * ROW_PIECES, LANES), jnp.uint32),
                   jax.ShapeDtypeStruct((t, ROUTE_COLS), F32), jax.ShapeDtypeStruct((1, ROUTE_COLS), F32),
                   jax.ShapeDtypeStruct((ROUTE_ROWS, t), F32)],
        scratch_shapes=[pltpu.VMEM((1, ROUTE_COLS), F32)],
        compiler_params=_cparams(("arbitrary",)),
        name="merge_router",
    )(oa, ob, gates, x, w['wa'], w['wb'], w['wo'], w['g_ffn'], w['w_route'], w['b_route'],
      jnp.tril(jnp.ones((tm, tm), BF16), -1))


def _expert_kernel(blk_e_ref, n_used_ref, blk_valid_ref, x_ref, wgu_ref, wdn_ref, y_ref):
    i = pl.program_id(0)
    bm = y_ref.shape[0] // ROW_PIECES

    @pl.when(i < n_used_ref[0])
    def _():
        x = _load_packed_rows(x_ref, bm)
        row = lax.broadcasted_iota(jnp.int32, (bm, 1), 0)
        x = jnp.where(row < blk_valid_ref[i], x, jnp.zeros_like(x))
        gu = jnp.dot(x, wgu_ref[0].astype(BF16), preferred_element_type=F32)
        gate, up = gu[:, :D_EXPERT], gu[:, D_EXPERT:]
        a = (gate * jax.nn.sigmoid(gate) * up).astype(BF16)
        y = jnp.dot(a, wdn_ref[0].astype(BF16), preferred_element_type=F32)
        _store_packed_rows(y_ref, 0, y.astype(BF16).astype(F32))

    @pl.when(i >= n_used_ref[0])
    def _():
        y_ref[...] = jnp.zeros(y_ref.shape, y_ref.dtype)


def _experts(blk_e, n_used, blk_valid, xs_rows, wgu, wdn, *, layer, bm):
    n_slots = xs_rows.shape[0] // ROW_PIECES
    grid_spec = pltpu.PrefetchScalarGridSpec(
        num_scalar_prefetch=3,
        grid=(n_slots // bm,),
        in_specs=[pl.BlockSpec((bm * ROW_PIECES, LANES), lambda i, be, nu, bv: (i, 0)),
                  pl.BlockSpec((1, D_MODEL, 2 * D_EXPERT),
                               lambda i, be, nu, bv: (layer * N_EXPERTS + be[i], 0, 0)),
                  pl.BlockSpec((1, D_EXPERT, D_MODEL),
                               lambda i, be, nu, bv: (layer * N_EXPERTS + be[i], 0, 0))],
        out_specs=pl.BlockSpec((bm * ROW_PIECES, LANES), lambda i, be, nu, bv: (i, 0)),
    )
    return pl.pallas_call(
        _expert_kernel,
        grid_spec=grid_spec,
        out_shape=jax.ShapeDtypeStruct((n_slots * ROW_PIECES, LANES), jnp.uint32),
        compiler_params=_cparams(("arbitrary",)),
        name="experts",
    )(blk_e, n_used, blk_valid, xs_rows, wgu, wdn)


def _combine_kernel(x_ref, y0_ref, y1_ref, route_ref, g_ref, o_ref, *, final_norm):
    r = route_ref[...]
    tm = x_ref.shape[0]
    y0 = _load_packed_rows(y0_ref, tm).astype(F32)
    y1 = _load_packed_rows(y1_ref, tm).astype(F32)
    y = x_ref[...] + r[:, 2:3] * y0 + r[:, 3:4] * y1
    if final_norm:
        y = _rms(y, g_ref[...])
    o_ref[...] = y


def _combine(x1, y01, route, g_final, *, tm, final_norm):
    t = x1.shape[0]
    nt = t // tm
    tok = lambda i: (i, 0)
    return pl.pallas_call(
        functools.partial(_combine_kernel, final_norm=final_norm),
        grid=(nt,),
        in_specs=[pl.BlockSpec((tm, D_MODEL), tok), pl.BlockSpec((tm * ROW_PIECES, LANES), tok),
                  pl.BlockSpec((tm * ROW_PIECES, LANES), lambda i: (nt + i, 0)),
                  pl.BlockSpec((tm, ROUTE_COLS), tok),
                  pl.BlockSpec((1, D_MODEL), lambda i: (0, 0))],
        out_specs=pl.BlockSpec((tm, D_MODEL), tok),
        out_shape=jax.ShapeDtypeStruct((t, D_MODEL), F32),
        compiler_params=_cparams(("parallel",)),
        name="combine",
    )(x1, y01, y01, route, g_final)


def _transpose_cast_kernel(x_ref, o_ref):
    o_ref[...] = x_ref[...].T.astype(o_ref.dtype)


def _transpose_cast(xt, block=256):
    c, d = xt.shape
    return pl.pallas_call(
        _transpose_cast_kernel,
        grid=(c // block,),
        in_specs=[pl.BlockSpec((block, d), lambda i: (i, 0))],
        out_specs=pl.BlockSpec((d, block), lambda i: (0, i)),
        out_shape=jax.ShapeDtypeStruct((d, c), BF16),
        compiler_params=_cparams(("parallel",)),
        name="weight_transpose",
    )(xt)


def _rope_tables(pos):
    lane = jnp.arange(LANES, dtype=jnp.int32)

    def tab(chunk):
        half = chunk // 2
        inv = ROPE_THETA ** (-(lane % half).astype(F32) / half)
        ang = pos.astype(F32)[:, None] * inv[None, :]
        first = (lane % chunk) < half
        return jnp.cos(ang), jnp.where(first[None, :], -jnp.sin(ang), jnp.sin(ang))

    c64, s64 = tab(DH_A)
    c32, s32 = tab(QK_ROPE)
    rope_lane = ((lane >= QK_NOPE) & (lane < QK_NOPE + QK_ROPE))[None, :]
    return dict(c64=c64, s64=s64, c32=c32, s32=s32,
                cq=jnp.where(rope_lane, c32, 1.0), sq=jnp.where(rope_lane, s32, 0.0))


def _sc_dispatch(rows, idx0, idx1, n_out_rows):
    n = rows.shape[0]
    mesh = plsc.VectorSubcoreMesh(core_axis_name="core", subcore_axis_name="subcore")

    @pl.kernel(out_type=jax.ShapeDtypeStruct((n_out_rows, LANES), rows.dtype), mesh=mesh, scratch_types=[])
    def scatter_rows(x_hbm, i0_hbm, i1_hbm, o_hbm):
        def body(x_vmem, i0_vmem, i1_vmem):
            pltpu.sync_copy(x_vmem, o_hbm.at[i0_vmem.at[0]])
            pltpu.sync_copy(x_vmem, o_hbm.at[i1_vmem.at[0]])

        pltpu.emit_pipeline(
            body,
            grid=(n // SC_WINDOW,),
            in_specs=[pl.BlockSpec((SC_WINDOW, LANES), index_map=lambda i: (i, 0)),
                      pl.BlockSpec((1, SC_WINDOW), index_map=lambda i: (0, i)),
                      pl.BlockSpec((1, SC_WINDOW), index_map=lambda i: (0, i))],
            out_specs=[],
            core_axis_name="subcore",
            dimension_semantics=(pltpu.PARALLEL,),
        )(x_hbm, i0_hbm, i1_hbm)

    return scatter_rows(rows, idx0, idx1)


def _sc_gather(table, idx):
    n = idx.shape[1]
    mesh = plsc.VectorSubcoreMesh(core_axis_name="core", subcore_axis_name="subcore")

    @pl.kernel(out_type=jax.ShapeDtypeStruct((n, LANES), table.dtype), mesh=mesh)
    def gather_rows(x_hbm, i_hbm, o_hbm):
        def body(i_vmem, o_vmem):
            pltpu.sync_copy(x_hbm.at[i_vmem.at[0]], o_vmem)

        pltpu.emit_pipeline(
            body,
            grid=(n // SC_WINDOW,),
            in_specs=[pl.BlockSpec((1, SC_WINDOW), index_map=lambda i: (0, i))],
            out_specs=[pl.BlockSpec((SC_WINDOW, LANES), index_map=lambda i: (i, 0))],
            core_axis_name="subcore",
            dimension_semantics=(pltpu.PARALLEL,),
        )(i_hbm, o_hbm)

    return gather_rows(table, idx)


def _record_rows(idx):
    piece = jnp.arange(idx.shape[1] * ROW_PIECES, dtype=jnp.int32) % ROW_PIECES
    return jnp.repeat(idx, ROW_PIECES, axis=1) * ROW_PIECES + piece[None, :]


def _moe_plan(route_t, counts, bm):
    n_tok = route_t.shape[1]
    n_asg = n_tok * TOP_K
    e = route_t[:TOP_K].astype(jnp.int32)
    rank = route_t[4:4 + TOP_K].astype(jnp.int32)
    counts = counts[0, :N_EXPERTS].astype(jnp.int32)
    padded = (counts + bm - 1) // bm * bm
    pad_end = jnp.cumsum(padded)
    pad_start = pad_end - padded
    start_of = jnp.zeros_like(e)
    for j in range(N_EXPERTS):
        start_of = jnp.where(e == j, pad_start[j], start_of)
    dest = start_of + rank
    n_slots = n_asg + N_EXPERTS * bm
    n_blk = n_slots // bm
    blk_start = jnp.arange(n_blk, dtype=jnp.int32) * bm
    blk_e = jnp.minimum(jnp.sum((pad_end[None, :] <= blk_start[:, None]).astype(jnp.int32), axis=1),
                        N_EXPERTS - 1)
    n_used = (pad_end[-1:] // bm).astype(jnp.int32)
    filled_end = pad_start + counts
    end_of_blk = jnp.sum(jnp.where(blk_e[:, None] == jnp.arange(N_EXPERTS, dtype=jnp.int32)[None, :],
                                   filled_end[None, :], 0), axis=1)
    blk_valid = jnp.clip(end_of_blk - blk_start, 0, bm).astype(jnp.int32)
    return dest, blk_e, n_used, blk_valid, n_slots


def _head_groups(parts):
    rows = parts[0].shape[0]
    used = sum(p.shape[-1] for p in parts)
    pad = jnp.zeros((rows, H_B, LANES - used), parts[0].dtype)
    return jnp.concatenate(list(parts) + [pad], axis=-1).reshape(rows, H_B * LANES)


def _layer_weights(l, g_attn, w_in, g_subln, w_a, g_q_lat, w_uq, g_kv_lat, w_uk, w_uv, w_b, w_o, g_ffn,
                   w_group, b_group, w_router, b_router):
    wt = jnp.swapaxes(w_in, 1, 2)[l]
    o_kr = 3 * COLS_A + Q_LORA + KV_LORA
    w_pack = _transpose_cast(jnp.concatenate(
        [wt[:o_kr], jnp.pad(wt[o_kr:o_kr + QK_ROPE], ((0, KR_PAD - QK_ROPE), (0, 0))), wt[o_kr + QK_ROPE:]],
        axis=0))
    wq = w_uq[l].reshape(Q_LORA, H_B, QK_NOPE + QK_ROPE)
    w_uq_p = jnp.concatenate([wq[:, :, :QK_NOPE].reshape(Q_LORA, COLS_QN),
                              wq[:, :, QK_NOPE:].reshape(Q_LORA, COLS_QR)], axis=1).astype(BF16)
    w_uq_g = _head_groups([wq]).astype(BF16)
    w_kn = _head_groups([w_uk[l]]).astype(BF16)
    place = jnp.pad(jnp.eye(QK_ROPE, dtype=F32), ((0, KR_PAD - QK_ROPE), (0, 0)))
    w_krp = _head_groups([jnp.zeros((KR_PAD, H_B, QK_NOPE), F32),
                          jnp.broadcast_to(place[:, None, :], (KR_PAD, H_B, QK_ROPE))]).astype(BF16)
    w_uv_all = w_uv[l].reshape(KV_LORA, COLS_VB).astype(BF16)
    wuk_t = jnp.transpose(w_uk[l], (1, 2, 0)).astype(BF16)
    wuv = jnp.transpose(w_uv[l], (1, 0, 2))
    eye = jnp.eye(H_B, dtype=F32)
    wz = (wuv[:, :, None, :] * eye[:, None, :, None]).reshape(H_B, KV_LORA, COLS_VB).astype(BF16)
    w_route = jnp.concatenate(
        [w_group[l], jnp.transpose(w_router[l], (1, 0, 2)).reshape(D_MODEL, N_EXPERTS),
         jnp.zeros((D_MODEL, ROUTE_COLS - N_GROUPS - N_EXPERTS), F32)], axis=1)
    w_route_hi = w_route.astype(BF16)
    w_route = jnp.stack([w_route_hi, (w_route - w_route_hi.astype(F32)).astype(BF16)])
    b_route = jnp.concatenate([b_group[l], b_router[l].reshape(N_EXPERTS),
                               jnp.zeros((ROUTE_COLS - N_GROUPS - N_EXPERTS,), F32)])[None, :]
    return dict(g_attn=g_attn[l][None, :], w_pack=w_pack, g_q=g_q_lat[l][None, :], g_kv=g_kv_lat[l][None, :],
                w_uq_p=w_uq_p, w_uq_g=w_uq_g, w_kn=w_kn, w_krp=w_krp, w_uv_all=w_uv_all, wuk_t=wuk_t, wz=wz,
                g_subln=g_subln[l][None, :], g_subln_col=g_subln[l][:, None],
                wa=w_a[l].astype(BF16), wb=w_b[l].astype(BF16), wo=w_o[l].astype(BF16),
                g_ffn=g_ffn[l][None, :], w_route=w_route, b_route=b_route)


def _trunk(x, pos_tile, n_pos_tiles, tm, caches, lws, lams, wgu, wdn, g_final, *, batch, seq, bm):
    depth = len(lws)
    t = batch * seq
    x = x.reshape(t, D_MODEL)
    tabs = _rope_tables(pos_tile)
    prompt = caches is None
    ks, vs, cs, rs = [], [], [], []
    new_cache = None
    for l in range(depth):
        w = lws[l]
        out_scale = 1.0 - (0.8 - 0.6 * math.exp(-0.3 * l))
        outs = _inproj(x, w, tabs, tm=tm, n_pos_tiles=n_pos_tiles, prompt=prompt, layer=l, depth=depth,
                       batch=batch, prev=new_cache)
        k, v, ckv, kr = outs[:4]
        if prompt:
            new_cache = (k, v, ckv, kr)
            qat, kb, vat, qbt, kq, vbt, gates = outs[4:]
            oa = _attn_t(qat, kb, vat, batch=batch, seq=seq, n_heads=H_A, dv=DV_A, n_maps=2, lam=lams[l],
                         g=w['g_subln_col'], out_scale=out_scale, name="diff_attn_prompt")
            ob = _attn_t(qbt, kq, vbt, batch=batch, seq=seq, n_heads=H_B, dv=V_B, n_maps=1,
                         name="mla_attn_prompt")
        else:
            qa, kb, vb, ckvb, krb, qn, qr, gates = outs[4:]
            kc, vc, cc, rc, past = caches
            oa = _diff_sample(lams[l], qa, kc, vc, kb, vb, w['g_subln'], layer=l, batch=batch, seq=seq,
                              past=past, out_scale=out_scale)
            ob = _mla_sample(qn, qr, cc, rc, ckvb, krb, w['wuk_t'], w['wz'], layer=l, batch=batch, seq=seq,
                             past=past)
        x1, h_rows, route, counts, route_t = _merge(oa, ob, gates, x, w, tm=tm, tiled_t=prompt)
        dest, blk_e, n_used, blk_valid, n_slots = _moe_plan(route_t, counts, bm)
        dest_rows = _record_rows(dest)
        xs_rows = _sc_dispatch(h_rows, dest_rows[0:1], dest_rows[1:2], n_slots * ROW_PIECES)
        y_slots = _experts(blk_e, n_used, blk_valid, xs_rows, wgu, wdn, layer=l, bm=bm)
        y01 = _sc_gather(y_slots, _record_rows(dest.reshape(1, TOP_K * t)))
        x = _combine(x1, y01, route, g_final, tm=tm, final_norm=(l == depth - 1))
        ks.append(k)
        vs.append(v)
        cs.append(ckv)
        rs.append(kr)
    y = x.reshape(batch, seq, D_MODEL)
    if prompt:
        k, v, ckv, kr = new_cache
        return (y,
                jnp.transpose(k.reshape(depth, batch, H_A, 2, DH_A, seq), (0, 1, 5, 2, 3, 4)),
                v.reshape(depth, batch, seq, H_A, DV_A),
                ckv.reshape(depth, batch, seq, KV_LORA),
                jnp.transpose(kr.reshape(depth, batch, QK_ROPE, seq), (0, 1, 3, 2)))
    return (y,
            jnp.stack(ks).reshape(depth, batch, seq, H_A, 2, DH_A),
            jnp.stack(vs).reshape(depth, batch, seq, H_A, DV_A),
            jnp.stack(cs).reshape(depth, batch, seq, KV_LORA),
            jnp.stack(rs).reshape(depth, batch, seq, QK_ROPE))


def kernel(x_prompt, x_sample, cache_diff_k, cache_diff_v, cache_mla_ckv, cache_mla_krope, g_attn, w_in,
           lambda_q1, lambda_k1, lambda_q2, lambda_k2, g_subln, w_a, g_q_lat, w_uq, g_kv_lat, w_uk, w_uv, w_b,
           w_o, g_ffn, w_group, b_group, w_router, b_router, w_gate_up, w_down, g_final):
    depth = w_in.shape[0]
    batch, seq, _ = x_prompt.shape
    dec_batch, dec_seq, _ = x_sample.shape
    past = cache_diff_k.shape[2]
    assert dec_seq == CHUNK and past % CHUNK == 0, "sample frames must form exactly one new chunk"
    assert seq % TQ == 0

    lws = [_layer_weights(l, g_attn, w_in, g_subln, w_a, g_q_lat, w_uq, g_kv_lat, w_uk, w_uv, w_b, w_o, g_ffn,
                          w_group, b_group, w_router, b_router) for l in range(depth)]
    lams = []
    for l in range(depth):
        lam_init = 0.8 - 0.6 * math.exp(-0.3 * l)
        lam = (jnp.exp(jnp.sum(lambda_q1[l] * lambda_k1[l])) - jnp.exp(jnp.sum(lambda_q2[l] * lambda_k2[l]))
               + lam_init)
        lams.append(lam.reshape(1).astype(F32))
    wgu = w_gate_up.reshape(depth * N_EXPERTS, D_MODEL, 2 * D_EXPERT)
    wdn = w_down.reshape(depth * N_EXPERTS, D_EXPERT, D_MODEL)
    g_fin = g_final[None, :]

    tm_p = min(512, seq)
    outs_p = _trunk(x_prompt, jnp.arange(seq, dtype=jnp.int32), seq // tm_p, tm_p, None, lws, lams, wgu, wdn,
                    g_fin, batch=batch, seq=seq, bm=min(512, batch * seq))

    tm_s = min(512, dec_batch * dec_seq)
    pos_tile = jnp.tile(past + jnp.arange(dec_seq, dtype=jnp.int32), tm_s // dec_seq)
    caches = (jnp.transpose(cache_diff_k, (0, 1, 3, 4, 5, 2)).reshape(depth * dec_batch * COLS_A, past),
              cache_diff_v.reshape(depth * dec_batch, past * H_A, DV_A),
              cache_mla_ckv.reshape(depth * dec_batch, past, KV_LORA),
              jnp.transpose(cache_mla_krope, (0, 1, 3, 2)).reshape(depth * dec_batch * QK_ROPE, past), past)
    outs_s = _trunk(x_sample, pos_tile, 1, tm_s, caches, lws, lams, wgu, wdn, g_fin,
                    batch=dec_batch, seq=dec_seq, bm=min(128, dec_batch * dec_seq))

    return (outs_p[0], outs_s[0]) + outs_p[1:] + outs_s[1:]
```

```python
import functools
import math

import jax
import jax.numpy as jnp
from jax import lax
from jax.experimental import pallas as pl
from jax.experimental.pallas import tpu as pltpu
from jax.experimental.pallas import tpu_sc as plsc

D_MODEL = 1024
CHUNK = 64
ROPE_THETA = 10000.0
EPS = 1e-6
H_A = 4
DH_A = 64
DV_A = 2 * DH_A
H_B = 8
Q_LORA = 384
KV_LORA = 256
QK_NOPE = 64
QK_ROPE = 32
V_B = 64
N_GROUPS = 4
EXPERTS_PER_GROUP = 8
N_EXPERTS = N_GROUPS * EXPERTS_PER_GROUP
TOP_K = 2
D_EXPERT = 512

COLS_A = H_A * 2 * DH_A
COLS_QR = H_B * QK_ROPE
COLS_QN = H_B * QK_NOPE
COLS_VB = H_B * V_B
LANES = 128
COLS_QB = H_B * LANES
KR_PAD = LANES
PACK_COLS = 3 * COLS_A + Q_LORA + KV_LORA + KR_PAD + 2 * D_MODEL
ROUTE_COLS = LANES
ROUTE_ROWS = 8
PACKED_COLS = D_MODEL // 2
ROW_PIECES = PACKED_COLS // LANES
SC_WINDOW = 128
TQ = 512
TK = 256
ONES_ROWS = 16
MERGE_SUB = 256
TOKEN_TILE = 512
CACHE_TILE = 512
EXPERT_BLOCK_PROMPT = 512
EXPERT_BLOCK_SAMPLE = 128
VMEM_LIMIT = 56 * 1024 * 1024
LOG2E = 1.4426950408889634
SCALE_A = DH_A ** -0.5 * LOG2E
SCALE_B = (QK_NOPE + QK_ROPE) ** -0.5 * LOG2E

F32 = jnp.float32
BF16 = jnp.bfloat16
NEG_INF = float("-inf")


def _cparams(sem):
    return pltpu.CompilerParams(dimension_semantics=sem, vmem_limit_bytes=VMEM_LIMIT)


def _rms(x, g):
    return x * lax.rsqrt(jnp.mean(x * x, axis=-1, keepdims=True) + EPS) * g


def _widen(tab, cols):
    reps = cols // LANES
    return tab if reps == 1 else jnp.concatenate([tab] * reps, axis=-1)


def _rope(x, cos, sin_signed, chunk):
    n = x.shape[-1]
    half = chunk // 2
    lane = lax.broadcasted_iota(jnp.int32, x.shape, 1)
    fwd = pltpu.roll(x, n - half, 1)
    bwd = pltpu.roll(x, half, 1)
    swapped = jnp.where((lane & (chunk - 1)) < half, fwd, bwd)
    return x * _widen(cos, n) + swapped * _widen(sin_signed, n)


def _dot_nt(a, b):
    return lax.dot_general(a, b, (((1,), (1,)), ((), ())), preferred_element_type=F32)


def _store_tiles_t(ref, val):
    tile = ref.shape[2]
    for r in range(ref.shape[0]):
        ref[r] = val[r * tile:(r + 1) * tile, :].T.astype(ref.dtype)


def _store_vt_ones(ref, val, n_heads, dv):
    dva = dv + ONES_ROWS
    ones = jnp.ones((ONES_ROWS, TK), ref.dtype)
    for r in range(ref.shape[0]):
        vt = val[r * TK:(r + 1) * TK, :].T.astype(ref.dtype)
        for h in range(n_heads):
            ref[r, h * dva:h * dva + dv, :] = vt[h * dv:(h + 1) * dv, :]
            ref[r, h * dva + dv:(h + 1) * dva, :] = ones


def _store_packed_rows(ref, row0, xf):
    rows = xf.shape[0]
    bits = pltpu.bitcast(xf, jnp.uint32)
    words = bits[:, :PACKED_COLS] | (bits[:, PACKED_COLS:] >> 16)
    for c in range(ROW_PIECES):
        ref[pl.ds(row0 * ROW_PIECES + c, rows, stride=ROW_PIECES), :] = words[:, c * LANES:(c + 1) * LANES]


def _load_packed_rows(ref, rows):
    pieces = [ref[pl.ds(c, rows, stride=ROW_PIECES), :] for c in range(ROW_PIECES)]
    hi = [pltpu.bitcast(p & jnp.uint32(0xFFFF0000), F32) for p in pieces]
    lo = [pltpu.bitcast(p << 16, F32) for p in pieces]
    return jnp.concatenate(hi + lo, axis=1).astype(BF16)


def _load_tiles_t(ref):
    return jnp.concatenate([ref[r].astype(F32).T for r in range(ref.shape[0])], axis=0).astype(BF16)


def _inproj_kernel(*refs, prompt):
    (x_ref, g_ref, w_ref, gq_ref, gkv_ref, wuq_ref, c64_ref, s64_ref, c32_ref, s32_ref) = refs[:10]
    x = x_ref[...]
    hb = _rms(x, g_ref[...]).astype(BF16)

    def proj(lo, hi):
        return jnp.dot(hb, w_ref[:, lo:hi], preferred_element_type=F32)

    c64, s64 = c64_ref[...], s64_ref[...]
    c32, s32 = c32_ref[...], s32_ref[...]
    o = 0
    qa = _rope(proj(o, o + COLS_A), c64, s64, DH_A) * SCALE_A
    o += COLS_A
    ka = _rope(proj(o, o + COLS_A), c64, s64, DH_A)
    o += COLS_A
    va = proj(o, o + COLS_A)
    o += COLS_A
    cq = _rms(proj(o, o + Q_LORA), gq_ref[...]).astype(BF16)
    o += Q_LORA
    qfull = jnp.dot(cq, wuq_ref[...], preferred_element_type=F32) * SCALE_B
    ckv = _rms(proj(o, o + KV_LORA), gkv_ref[...])
    ckvb = ckv.astype(BF16)
    o += KV_LORA
    kr_pad = _rope(proj(o, o + KR_PAD), c32, s32, QK_ROPE)
    o += KR_PAD
    gates = jax.nn.sigmoid(proj(o, o + 2 * D_MODEL)).astype(BF16)

    if prompt:
        cq_ref, sq_ref, wkn_ref, wkr_ref, wuv_ref = refs[10:15]
        (k_ref, v_ref, ckv_ref, kr_ref, qat_ref, kb_ref, vat_ref, qbt_ref, kq_ref, vbt_ref,
         gate_ref) = refs[len(refs) - 11:]
        k_ref[...] = ka.T
        for h in range(H_A):
            v_ref[pl.ds(h, va.shape[0], stride=H_A), :] = va[:, h * DV_A:(h + 1) * DV_A]
        ckv_ref[...] = ckv
        kr_ref[...] = kr_pad.T[:QK_ROPE, :]
        _store_tiles_t(qat_ref, qa)
        kb_ref[...] = ka.astype(BF16)
        _store_vt_ones(vat_ref, va, H_A, DV_A)
        _store_tiles_t(qbt_ref, _rope(qfull, cq_ref[...], sq_ref[...], QK_ROPE))
        kq = (jnp.dot(ckvb, wkn_ref[...], preferred_element_type=F32)
              + jnp.dot(kr_pad.astype(BF16), wkr_ref[...], preferred_element_type=F32))
        kq_ref[...] = kq.astype(BF16)
        _store_vt_ones(vbt_ref, jnp.dot(ckvb, wuv_ref[...], preferred_element_type=F32), H_B, V_B)
    else:
        (k_ref, v_ref, ckv_ref, kr_ref, qa_ref, kb_ref, vb_ref, ckvb_ref, krb_ref, qn_ref, qr_ref,
         gate_ref) = refs[10:]
        qa_ref[...] = qa.astype(BF16)
        kb_ref[...] = ka.astype(BF16)
        vb_ref[...] = va.astype(BF16)
        ckvb_ref[...] = ckvb
        krb_ref[...] = kr_pad[:, :QK_ROPE].astype(BF16)
        qn_ref[...] = qfull[:, :COLS_QN].astype(BF16)
        qr_ref[...] = _rope(qfull[:, COLS_QN:], c32, s32, QK_ROPE).astype(BF16)
        k_ref[...] = ka
        v_ref[...] = va
        ckv_ref[...] = ckv
        kr_ref[...] = kr_pad[:, :QK_ROPE]
    gate_ref[...] = gates


def _inproj(x, w, tabs, *, tm, n_pos_tiles, prompt, layer=0, depth=1, batch=1, prev=None):
    t = x.shape[0]
    nt = t // tm
    tok = lambda i: (i, 0)
    full = lambda i: (0, 0)
    pos = lambda i: (i % n_pos_tiles, 0)
    tile3 = lambda i: (i, 0, 0)
    wuq = w['w_uq_g'] if prompt else w['w_uq_p']
    ins = [x, w['g_attn'], w['w_pack'], w['g_q'], w['g_kv'], wuq, tabs['c64'], tabs['s64'], tabs['c32'], tabs['s32']]
    in_specs = [pl.BlockSpec((tm, D_MODEL), tok), pl.BlockSpec((1, D_MODEL), full),
                pl.BlockSpec((D_MODEL, PACK_COLS), full), pl.BlockSpec((1, Q_LORA), full),
                pl.BlockSpec((1, KV_LORA), full), pl.BlockSpec(wuq.shape, full)] + [pl.BlockSpec((tm, LANES), pos)] * 4
    if prompt:
        seq = t // batch
        per_b = seq // tm
        seq_minor = lambda i: (layer * batch + i // per_b, i % per_b)
        tok_l = lambda i: (layer * nt + i, 0)
        out_specs = [pl.BlockSpec((COLS_A, tm), seq_minor), pl.BlockSpec((tm * H_A, DV_A), tok_l),
                     pl.BlockSpec((tm, KV_LORA), tok_l), pl.BlockSpec((QK_ROPE, tm), seq_minor)]
        out_shape = [jax.ShapeDtypeStruct((depth * batch * COLS_A, seq), F32),
                     jax.ShapeDtypeStruct((depth * t * H_A, DV_A), F32),
                     jax.ShapeDtypeStruct((depth * t, KV_LORA), F32),
                     jax.ShapeDtypeStruct((depth * batch * QK_ROPE, seq), F32)]
    else:
        leaves = [(COLS_A, F32), (COLS_A, F32), (KV_LORA, F32), (QK_ROPE, F32)]
        out_specs = [pl.BlockSpec((tm, c), tok) for c, _ in leaves]
        out_shape = [jax.ShapeDtypeStruct((t, c), d) for c, d in leaves]

    def add2d(c):
        out_specs.append(pl.BlockSpec((tm, c), tok))
        out_shape.append(jax.ShapeDtypeStruct((t, c), BF16))

    def add3d(c, tile):
        out_specs.append(pl.BlockSpec((tm // tile, c, tile), tile3))
        out_shape.append(jax.ShapeDtypeStruct((t // tile, c, tile), BF16))

    if prompt:
        ins += [tabs['cq'], tabs['sq'], w['w_kn'], w['w_krp'], w['w_uv_all']]
        in_specs += [pl.BlockSpec((tm, LANES), pos)] * 2 + [pl.BlockSpec(w[n].shape, full)
                                                             for n in ('w_kn', 'w_krp', 'w_uv_all')]
        add3d(COLS_A, TQ), add2d(COLS_A), add3d(H_A * (DV_A + ONES_ROWS), TK)
        add3d(COLS_QB, TQ), add2d(COLS_QB), add3d(H_B * (V_B + ONES_ROWS), TK)
    else:
        for c in (COLS_A, COLS_A, COLS_A, KV_LORA, QK_ROPE, COLS_QN, COLS_QR):
            add2d(c)
    add2d(2 * D_MODEL)
    aliases = {}
    if prev is not None:
        aliases = {len(ins) + n: n for n in range(len(prev))}
        ins += list(prev)
        in_specs += [pl.BlockSpec(memory_space=pl.ANY)] * len(prev)
    return pl.pallas_call(
        functools.partial(_inproj_kernel, prompt=prompt),
        grid=(nt,),
        in_specs=in_specs,
        out_specs=out_specs,
        out_shape=out_shape,
        input_output_aliases=aliases,
        compiler_params=_cparams(("parallel",)),
        name="inproj_prompt" if prompt else "inproj_sample",
    )(*ins)


def _attn_t_kernel(*refs, n_heads, dv, n_maps, out_scale):
    if n_maps == 2:
        lam_ref, qt_ref, k_ref, vt_ref, g_ref, o_ref, q_sc, m_sc, acc_sc, sta_sc, stb_sc, mxa_sc, mxb_sc = refs
    else:
        qt_ref, k_ref, vt_ref, o_ref, m_sc, acc_sc, sta_sc, stb_sc, mxa_sc, mxb_sc = refs
    i = pl.program_id(1)
    dva = dv + ONES_ROWS
    width = n_maps * TQ
    heads = range(n_heads)
    if n_maps == 2:
        for h in heads:
            qt = qt_ref[0, h * LANES:(h + 1) * LANES, :]
            row = lax.broadcasted_iota(jnp.int32, qt.shape, 0)
            zero = jnp.zeros_like(qt)
            q_sc[h, :, :TQ] = jnp.where(row < DH_A, qt, zero)
            q_sc[h, :, TQ:] = jnp.where(row >= DH_A, qt, zero)
    m_sc[...] = jnp.full(m_sc.shape, NEG_INF, F32)
    acc_sc[...] = jnp.zeros(acc_sc.shape, F32)
    rel = ((lax.broadcasted_iota(jnp.int32, (TK, width), 1) & (TQ - 1)) // CHUNK
           - lax.broadcasted_iota(jnp.int32, (TK, width), 0) // CHUNK)

    def score_head(j, h, st_ref, mx_ref):
        rows = pl.ds(pl.multiple_of(j * TK, TK), TK)
        st = jnp.dot(k_ref[rows, h * LANES:(h + 1) * LANES],
                     q_sc[h] if n_maps == 2 else qt_ref[0, h * LANES:(h + 1) * LANES, :],
                     preferred_element_type=F32)
        st_ref[h] = st
        mx_ref[h] = jnp.max(st, axis=0, keepdims=True)

    def softmax_head(j, h, st_ref, mx_ref, masked):
        st = st_ref[h]
        if masked:
            st = jnp.where(rel >= j * (TK // CHUNK) - i * (TQ // CHUNK), st, NEG_INF)
            tile_max = jnp.max(st, axis=0, keepdims=True)
        else:
            tile_max = mx_ref[h]
        m = m_sc[h]
        m_new = jnp.maximum(m, tile_max)
        m_sc[h] = m_new
        return jnp.exp2(st - m_new).astype(BF16), jnp.exp2(m - m_new)

    def value_head(j, h, pt, alpha):
        pv = jnp.dot(vt_ref[j, h * dva:(h + 1) * dva, :], pt, preferred_element_type=F32)
        acc_sc[h] = alpha * acc_sc[h] + pv

    def stage(j, cur, masked, nxt=None):
        pending = None
        for h in heads:
            if nxt is not None:
                score_head(j + 1, h, *nxt)
            pt, alpha = softmax_head(j, h, *cur, masked)
            if pending is not None:
                value_head(j, *pending)
            pending = (h, pt, alpha)
        value_head(j, *pending)

    buf_a, buf_b = (sta_sc, mxa_sc), (stb_sc, mxb_sc)

    def full_pair(p, carry):
        j = 2 * p
        stage(j, buf_a, False, buf_b)
        stage(j + 1, buf_b, False, buf_a)
        return carry

    assert TQ == 2 * TK
    for h in heads:
        score_head(0, h, *buf_a)
    lax.fori_loop(0, i, full_pair, 0)
    stage(2 * i, buf_a, True, buf_b)
    stage(2 * i + 1, buf_b, True)


    for h in heads:
        acc = acc_sc[h]
        inv = acc[:dv] / acc[dv:dv + 1]
        if n_maps == 2:
            o = inv[:, :TQ] - lam_ref[0] * inv[:, TQ:]
            o = o * lax.rsqrt(jnp.mean(o * o, axis=0, keepdims=True) + EPS) * g_ref[...] * out_scale
        else:
            o = inv
        o_ref[0, h * dv:(h + 1) * dv, :] = o.astype(o_ref.dtype)


def _attn_t(qt, k, vt, *, batch, seq, n_heads, dv, n_maps, lam=None, g=None, out_scale=1.0, name):
    nq = seq // TQ
    dva = dv + ONES_ROWS
    width = n_maps * TQ
    once = pl.Buffered(1)
    ins, in_specs, scratch = [], [], []
    if n_maps == 2:
        ins.append(lam)
        in_specs.append(pl.BlockSpec(memory_space=pltpu.SMEM))
        scratch.append(pltpu.VMEM((n_heads, LANES, width), BF16))
    ins += [qt, k, vt]
    in_specs += [pl.BlockSpec((1, n_heads * LANES, TQ), lambda b, i: (b * nq + i, 0, 0)),
                 pl.BlockSpec((seq, n_heads * LANES), lambda b, i: (b, 0), pipeline_mode=once),
                 pl.BlockSpec((seq // TK, n_heads * dva, TK), lambda b, i: (b, 0, 0), pipeline_mode=once)]
    if n_maps == 2:
        ins.append(g)
        in_specs.append(pl.BlockSpec((dv, 1), lambda b, i: (0, 0)))
    return pl.pallas_call(
        functools.partial(_attn_t_kernel, n_heads=n_heads, dv=dv, n_maps=n_maps, out_scale=out_scale),
        grid=(batch, nq),
        in_specs=in_specs,
        out_specs=pl.BlockSpec((1, n_heads * dv, TQ), lambda b, i: (b * nq + i, 0, 0)),
        out_shape=jax.ShapeDtypeStruct((batch * nq, n_heads * dv, TQ), BF16),
        scratch_shapes=scratch + [pltpu.VMEM((n_heads, 1, width), F32), pltpu.VMEM((n_heads, dva, width), F32),
                                  pltpu.VMEM((n_heads, TK, width), F32), pltpu.VMEM((n_heads, TK, width), F32),
                                  pltpu.VMEM((n_heads, 1, width), F32), pltpu.VMEM((n_heads, 1, width), F32)],
        compiler_params=_cparams(("parallel", "arbitrary")),
        name=name,
    )(*ins)


def _flash_steps(ss, vs, m_sc, l_sc, acc_sc):
    ps, alphas = [], []
    for c, s in enumerate(ss):
        m_prev = m_sc[c]
        m_new = jnp.maximum(m_prev, jnp.max(s, axis=-1, keepdims=True))
        alpha = jnp.exp2(m_prev - m_new)
        cols = s.shape[-1]
        p = jnp.exp2(s - (_widen(m_new, cols) if cols % LANES == 0 else m_new[:, :cols]))
        l_sc[c] = alpha * l_sc[c] + jnp.sum(p, axis=-1, keepdims=True)
        m_sc[c] = m_new
        ps.append(p.astype(BF16))
        alphas.append(alpha)
    pvs = [jnp.dot(p, v, preferred_element_type=F32) for p, v in zip(ps, vs)]
    for c, pv in enumerate(pvs):
        acc_sc[c] = _widen(alphas[c], acc_sc.shape[-1]) * acc_sc[c] + pv


def _init_flash(m_sc, l_sc, acc_sc):
    m_sc[...] = jnp.full(m_sc.shape, NEG_INF, F32)
    l_sc[...] = jnp.zeros(l_sc.shape, F32)
    acc_sc[...] = jnp.zeros(acc_sc.shape, F32)


def _stack_maps(q):
    lane = lax.broadcasted_iota(jnp.int32, q.shape, 1)
    zero = jnp.zeros_like(q)
    return jnp.concatenate([jnp.where(lane < DH_A, q, zero), jnp.where(lane >= DH_A, q, zero)], axis=0)


def _diff_sample_kernel(lam_ref, q_ref, kc_ref, vc_ref, kn_ref, vn_ref, g_ref, o_ref, m_sc, l_sc, acc_sc,
                        *, tq, tk, n_cache_tiles, out_scale):
    heads = range(H_A)
    lanes = [slice(h * DV_A, (h + 1) * DV_A) for h in heads]
    qss = [_stack_maps(q_ref[:, lanes[h]]) for h in heads]
    _init_flash(m_sc, l_sc, acc_sc)

    def cache_scores(j):
        return [jnp.dot(qss[h], kc_ref[lanes[h], j * tk:(j + 1) * tk].astype(BF16), preferred_element_type=F32)
                for h in heads]

    ss = cache_scores(0)
    for j in range(n_cache_tiles):
        ss_next = (cache_scores(j + 1) if j + 1 < n_cache_tiles
                   else [_dot_nt(qss[h], kn_ref[:, lanes[h]]) for h in heads])
        vs = [vc_ref[0, pl.ds(j * tk * H_A + h, tk, stride=H_A), :].astype(BF16) for h in heads]
        _flash_steps(ss, vs, m_sc, l_sc, acc_sc)
        ss = ss_next
    _flash_steps(ss, [vn_ref[:, lanes[h]] for h in heads], m_sc, l_sc, acc_sc)
    for h in heads:
        inv = acc_sc[h] / l_sc[h]
        o = inv[:tq] - lam_ref[0] * inv[tq:]
        o_ref[:, lanes[h]] = (_rms(o, g_ref[...]) * out_scale).astype(o_ref.dtype)


def _diff_sample(lam, qa, kcache, vcache, kb, vb, g_subln, *, layer, batch, seq, past, out_scale):
    tk = min(CACHE_TILE, past)
    smem = pl.BlockSpec(memory_space=pltpu.SMEM)
    new = pl.BlockSpec((seq, COLS_A), lambda b: (b, 0))
    kcache_spec = pl.BlockSpec((COLS_A, past), lambda b: (layer * batch + b, 0))
    vcache_spec = pl.BlockSpec((1, past * H_A, DV_A), lambda b: (layer * batch + b, 0, 0))
    return pl.pallas_call(
        functools.partial(_diff_sample_kernel, tq=seq, tk=tk, n_cache_tiles=past // tk, out_scale=out_scale),
        grid=(batch,),
        in_specs=[smem, new, kcache_spec, vcache_spec, new, new, pl.BlockSpec((1, DV_A), lambda b: (0, 0))],
        out_specs=new,
        out_shape=jax.ShapeDtypeStruct((batch * seq, COLS_A), BF16),
        scratch_shapes=[pltpu.VMEM((H_A, 2 * seq, LANES), F32), pltpu.VMEM((H_A, 2 * seq, LANES), F32),
                        pltpu.VMEM((H_A, 2 * seq, DV_A), F32)],
        compiler_params=_cparams(("parallel",)),
        name="diff_attn_sample",
    )(lam, qa, kcache, vcache, kb, vb, g_subln)


def _mla_sample_kernel(qn_ref, qr_ref, cc_ref, rc_ref, cn_ref, rn_ref, wuk_ref, wz_ref, o_ref,
                       ql_sc, qr_sc, m_sc, l_sc, acc_sc, *, tq, tk, n_cache_tiles):
    qn = qn_ref[...]
    qr = qr_ref[...].astype(F32)
    for h in range(H_B):
        ql = jnp.dot(qn[:, h * QK_NOPE:(h + 1) * QK_NOPE], wuk_ref[h], preferred_element_type=F32)
        ql_sc[h * tq:(h + 1) * tq, :] = ql.astype(BF16)
        qr_sc[h * tq:(h + 1) * tq, :] = qr[:, h * QK_ROPE:(h + 1) * QK_ROPE].astype(BF16)
    _init_flash(m_sc, l_sc, acc_sc)
    n_chains = m_sc.shape[0]
    half = H_B * tq // n_chains
    parts = [slice(c * half, (c + 1) * half) for c in range(n_chains)]

    def cache_tile(j):
        c = cc_ref[0, j * tk:(j + 1) * tk, :].astype(BF16)
        rt = rc_ref[:, j * tk:(j + 1) * tk].astype(BF16)
        return c, [_dot_nt(ql_sc[p, :], c) + jnp.dot(qr_sc[p, :], rt, preferred_element_type=F32) for p in parts]

    def new_tile():
        c = cn_ref[...]
        return c, [_dot_nt(ql_sc[p, :], c) + _dot_nt(qr_sc[p, :], rn_ref[...]) for p in parts]

    c, ss = cache_tile(0)
    for j in range(n_cache_tiles):
        c_next, ss_next = cache_tile(j + 1) if j + 1 < n_cache_tiles else new_tile()
        _flash_steps(ss, [c] * n_chains, m_sc, l_sc, acc_sc)
        c, ss = c_next, ss_next
    _flash_steps(ss, [c] * n_chains, m_sc, l_sc, acc_sc)
    ob = jnp.zeros(o_ref.shape, F32)
    heads_per_chain = H_B // n_chains
    for h in range(H_B):
        c, r = divmod(h, heads_per_chain)
        o_lat = (acc_sc[c, r * tq:(r + 1) * tq, :] / _widen(l_sc[c, r * tq:(r + 1) * tq, :], KV_LORA)).astype(BF16)
        ob += jnp.dot(o_lat, wz_ref[h], preferred_element_type=F32)
    o_ref[...] = ob.astype(o_ref.dtype)


def _mla_sample(qn, qr, ccache, rcache, ckvb, krb, wuk_t, wz, *, layer, batch, seq, past):
    tk = min(CACHE_TILE, past)
    rows = H_B * seq
    return pl.pallas_call(
        functools.partial(_mla_sample_kernel, tq=seq, tk=tk, n_cache_tiles=past // tk),
        grid=(batch,),
        in_specs=[pl.BlockSpec((seq, COLS_QN), lambda b: (b, 0)),
                  pl.BlockSpec((seq, COLS_QR), lambda b: (b, 0)),
                  pl.BlockSpec((1, past, KV_LORA), lambda b: (layer * batch + b, 0, 0)),
                  pl.BlockSpec((QK_ROPE, past), lambda b: (layer * batch + b, 0)),
                  pl.BlockSpec((seq, KV_LORA), lambda b: (b, 0)),
                  pl.BlockSpec((seq, QK_ROPE), lambda b: (b, 0)),
                  pl.BlockSpec((H_B, QK_NOPE, KV_LORA), lambda b: (0, 0, 0)),
                  pl.BlockSpec((H_B, KV_LORA, COLS_VB), lambda b: (0, 0, 0))],
        out_specs=pl.BlockSpec((seq, COLS_VB), lambda b: (b, 0)),
        out_shape=jax.ShapeDtypeStruct((batch * seq, COLS_VB), BF16),
        scratch_shapes=[pltpu.VMEM((rows, KV_LORA), BF16), pltpu.VMEM((rows, QK_ROPE), BF16),
                        pltpu.VMEM((2, rows // 2, LANES), F32), pltpu.VMEM((2, rows // 2, LANES), F32),
                        pltpu.VMEM((2, rows // 2, KV_LORA), F32)],
        compiler_params=_cparams(("parallel",)),
        name="mla_attn_sample",
    )(qn, qr, ccache, rcache, ckvb, krb, wuk_t, wz)


def _merge_kernel(oa_ref, ob_ref, gate_ref, x_ref, wa_ref, wb_ref, wo_ref, gf_ref, wr_ref, br_ref, tri_ref,
                  x1_ref, h_ref, route_ref, count_ref, routet_ref, run_sc, *, tiled_t):
    oa = _load_tiles_t(oa_ref) if tiled_t else oa_ref[...]
    ob = _load_tiles_t(ob_ref) if tiled_t else ob_ref[...]
    tm = x_ref.shape[0]
    subs = [slice(r * MERGE_SUB, (r + 1) * MERGE_SUB) for r in range(tm // MERGE_SUB)]
    yas = [jnp.dot(oa[s], wa_ref[...], preferred_element_type=F32) for s in subs]
    ybs = [jnp.dot(ob[s], wb_ref[...], preferred_element_type=F32) for s in subs]
    merged = []
    for s, ya, yb in zip(subs, yas, ybs):
        gates = gate_ref[s, :].astype(F32)
        merged.append((gates[:, :D_MODEL] * ya + gates[:, D_MODEL:] * yb).astype(BF16))
    x1s = [x_ref[s, :] + jnp.dot(mg, wo_ref[...], preferred_element_type=F32) for s, mg in zip(subs, merged)]
    logit_parts = []
    for s, x1 in zip(subs, x1s):
        x1_ref[s, :] = x1
        h = _rms(x1, gf_ref[...])
        h_hi = h.astype(BF16)
        _store_packed_rows(h_ref, s.start, h_hi.astype(F32))
        h_lo = (h - h_hi.astype(F32)).astype(BF16)
        logit_parts.append(jnp.dot(h_hi, wr_ref[0], preferred_element_type=F32)
                           + jnp.dot(h_lo, wr_ref[0], preferred_element_type=F32)
                           + jnp.dot(h_hi, wr_ref[1], preferred_element_type=F32))

    logits = jnp.concatenate(logit_parts, axis=0) + br_ref[...]
    lane = lax.broadcasted_iota(jnp.int32, logits.shape, 1)
    big = jnp.int32(ROUTE_COLS)

    def top1(mask):
        v = jnp.max(jnp.where(mask, logits, NEG_INF), axis=-1, keepdims=True)
        idx = jnp.min(jnp.where(mask & (logits == v), lane, big), axis=-1, keepdims=True)
        return v, idx

    gmask = lane < N_GROUPS
    gmax, gidx = top1(gmask)
    g_w = 1.0 / jnp.sum(jnp.where(gmask, jnp.exp(logits - gmax), 0.0), axis=-1, keepdims=True)
    lo = N_GROUPS + gidx * EXPERTS_PER_GROUP
    emask = (lane >= lo) & (lane < lo + EXPERTS_PER_GROUP)
    v1, i1 = top1(emask)
    v2, i2 = top1(emask & (lane != i1))
    e2 = jnp.exp(v2 - v1)
    w1 = g_w / (1.0 + e2)
    w2 = g_w * e2 / (1.0 + e2)
    @pl.when(pl.program_id(0) == 0)
    def _():
        run_sc[...] = jnp.zeros(run_sc.shape, F32)

    e1 = i1 - N_GROUPS
    e2i = i2 - N_GROUPS
    picks = jnp.where((lane == e1) | (lane == e2i), 1.0, 0.0)
    before = jnp.dot(tri_ref[...], picks.astype(BF16), preferred_element_type=F32) + run_sc[...]
    rank1 = jnp.sum(jnp.where(lane == e1, before, 0.0), axis=-1, keepdims=True)
    rank2 = jnp.sum(jnp.where(lane == e2i, before, 0.0), axis=-1, keepdims=True)
    run_sc[...] = run_sc[...] + jnp.sum(picks, axis=0, keepdims=True)
    count_ref[...] = run_sc[...]

    vals = [e1.astype(F32), e2i.astype(F32), w1, w2, rank1, rank2]
    route = jnp.zeros(logits.shape, F32)
    for n, v in enumerate(vals):
        route = jnp.where(lane == n, v, route)
    route_ref[...] = route
    routet_ref[...] = route.T[:routet_ref.shape[0], :]


def _merge(oa, ob, gates, x, w, *, tm, tiled_t):
    t = x.shape[0]
    tok = lambda i: (i, 0)
    full = lambda i: (0, 0)
    if tiled_t:
        o_specs = [pl.BlockSpec((tm // TQ, COLS_A, TQ), lambda i: (i, 0, 0)),
                   pl.BlockSpec((tm // TQ, COLS_VB, TQ), lambda i: (i, 0, 0))]
    else:
        o_specs = [pl.BlockSpec((tm, COLS_A), tok), pl.BlockSpec((tm, COLS_VB), tok)]
    return pl.pallas_call(
        functools.partial(_merge_kernel, tiled_t=tiled_t),
        grid=(t // tm,),
        in_specs=o_specs + [
            pl.BlockSpec((tm, 2 * D_MODEL), tok), pl.BlockSpec((tm, D_MODEL), tok),
            pl.BlockSpec((COLS_A, D_MODEL), full), pl.BlockSpec((COLS_VB, D_MODEL), full),
            pl.BlockSpec((D_MODEL, D_MODEL), full), pl.BlockSpec((1, D_MODEL), full),
            pl.BlockSpec((2, D_MODEL, ROUTE_COLS), lambda i: (0, 0, 0)), pl.BlockSpec((1, ROUTE_COLS), full),
            pl.BlockSpec((tm, tm), full)],
        out_specs=[pl.BlockSpec((tm, D_MODEL), tok), pl.BlockSpec((tm * ROW_PIECES, LANES), tok),
                   pl.BlockSpec((tm, ROUTE_COLS), tok), pl.BlockSpec((1, ROUTE_COLS), full),
                   pl.BlockSpec((ROUTE_ROWS, tm), lambda i: (0, i))],
        out_shape=[jax.ShapeDtypeStruct((t, D_MODEL), F32),
                   jax.ShapeDtypeStruct((t * ROW_PIECES, LANES), jnp.uint32),
                   jax.ShapeDtypeStruct((t, ROUTE_COLS), F32), jax.ShapeDtypeStruct((1, ROUTE_COLS), F32),
                   jax.ShapeDtypeStruct((ROUTE_ROWS, t), F32)],
        scratch_shapes=[pltpu.VMEM((1, ROUTE_COLS), F32)],
        compiler_params=_cparams(("arbitrary",)),
        name="merge_router",
    )(oa, ob, gates, x, w['wa'], w['wb'], w['wo'], w['g_ffn'], w['w_route'], w['b_route'],
      jnp.tril(jnp.ones((tm, tm), BF16), -1))


def _expert_kernel(blk_e_ref, n_used_ref, blk_valid_ref, x_ref, wgu_ref, wdn_ref, y_ref):
    i = pl.program_id(0)
    bm = y_ref.shape[0] // ROW_PIECES

    @pl.when(i < n_used_ref[0])
    def _():
        x = _load_packed_rows(x_ref, bm)
        row = lax.broadcasted_iota(jnp.int32, (bm, 1), 0)
        x = jnp.where(row < blk_valid_ref[i], x, jnp.zeros_like(x))
        gu = jnp.dot(x, wgu_ref[0].astype(BF16), preferred_element_type=F32)
        gate, up = gu[:, :D_EXPERT], gu[:, D_EXPERT:]
        a = (gate * jax.nn.sigmoid(gate) * up).astype(BF16)
        y = jnp.dot(a, wdn_ref[0].astype(BF16), preferred_element_type=F32)
        _store_packed_rows(y_ref, 0, y.astype(BF16).astype(F32))

    @pl.when(i >= n_used_ref[0])
    def _():
        y_ref[...] = jnp.zeros(y_ref.shape, y_ref.dtype)


def _experts(blk_e, n_used, blk_valid, xs_rows, wgu, wdn, *, layer, bm):
    n_slots = xs_rows.shape[0] // ROW_PIECES
    grid_spec = pltpu.PrefetchScalarGridSpec(
        num_scalar_prefetch=3,
        grid=(n_slots // bm,),
        in_specs=[pl.BlockSpec((bm * ROW_PIECES, LANES), lambda i, be, nu, bv: (i, 0)),
                  pl.BlockSpec((1, D_MODEL, 2 * D_EXPERT),
                               lambda i, be, nu, bv: (layer * N_EXPERTS + be[i], 0, 0)),
                  pl.BlockSpec((1, D_EXPERT, D_MODEL),
                               lambda i, be, nu, bv: (layer * N_EXPERTS + be[i], 0, 0))],
        out_specs=pl.BlockSpec((bm * ROW_PIECES, LANES), lambda i, be, nu, bv: (i, 0)),
    )
    return pl.pallas_call(
        _expert_kernel,
        grid_spec=grid_spec,
        out_shape=jax.ShapeDtypeStruct((n_slots * ROW_PIECES, LANES), jnp.uint32),
        compiler_params=_cparams(("arbitrary",)),
        name="experts",
    )(blk_e, n_used, blk_valid, xs_rows, wgu, wdn)


def _combine_kernel(x_ref, y0_ref, y1_ref, route_ref, g_ref, o_ref, *, final_norm):
    r = route_ref[...]
    tm = x_ref.shape[0]
    y0 = _load_packed_rows(y0_ref, tm).astype(F32)
    y1 = _load_packed_rows(y1_ref, tm).astype(F32)
    y = x_ref[...] + r[:, 2:3] * y0 + r[:, 3:4] * y1
    if final_norm:
        y = _rms(y, g_ref[...])
    o_ref[...] = y


def _combine(x1, y01, route, g_final, *, tm, final_norm):
    t = x1.shape[0]
    nt = t // tm
    tok = lambda i: (i, 0)
    return pl.pallas_call(
        functools.partial(_combine_kernel, final_norm=final_norm),
        grid=(nt,),
        in_specs=[pl.BlockSpec((tm, D_MODEL), tok), pl.BlockSpec((tm * ROW_PIECES, LANES), tok),
                  pl.BlockSpec((tm * ROW_PIECES, LANES), lambda i: (nt + i, 0)),
                  pl.BlockSpec((tm, ROUTE_COLS), tok),
                  pl.BlockSpec((1, D_MODEL), lambda i: (0, 0))],
        out_specs=pl.BlockSpec((tm, D_MODEL), tok),
        out_shape=jax.ShapeDtypeStruct((t, D_MODEL), F32),
        compiler_params=_cparams(("parallel",)),
        name="combine",
    )(x1, y01, y01, route, g_final)


def _transpose_cast_kernel(x_ref, o_ref):
    o_ref[...] = x_ref[...].T.astype(o_ref.dtype)


def _transpose_cast(xt, block=256):
    c, d = xt.shape
    return pl.pallas_call(
        _transpose_cast_kernel,
        grid=(c // block,),
        in_specs=[pl.BlockSpec((block, d), lambda i: (i, 0))],
        out_specs=pl.BlockSpec((d, block), lambda i: (0, i)),
        out_shape=jax.ShapeDtypeStruct((d, c), BF16),
        compiler_params=_cparams(("parallel",)),
        name="weight_transpose",
    )(xt)


def _rope_tables(pos):
    lane = jnp.arange(LANES, dtype=jnp.int32)

    def tab(chunk):
        half = chunk // 2
        inv = ROPE_THETA ** (-(lane % half).astype(F32) / half)
        ang = pos.astype(F32)[:, None] * inv[None, :]
        first = (lane % chunk) < half
        return jnp.cos(ang), jnp.where(first[None, :], -jnp.sin(ang), jnp.sin(ang))

    c64, s64 = tab(DH_A)
    c32, s32 = tab(QK_ROPE)
    rope_lane = ((lane >= QK_NOPE) & (lane < QK_NOPE + QK_ROPE))[None, :]
    return dict(c64=c64, s64=s64, c32=c32, s32=s32,
                cq=jnp.where(rope_lane, c32, 1.0), sq=jnp.where(rope_lane, s32, 0.0))


def _sc_dispatch(rows, idx0, idx1, n_out_rows):
    n = rows.shape[0]
    mesh = plsc.VectorSubcoreMesh(core_axis_name="core", subcore_axis_name="subcore")

    @pl.kernel(out_type=jax.ShapeDtypeStruct((n_out_rows, LANES), rows.dtype), mesh=mesh, scratch_types=[])
    def scatter_rows(x_hbm, i0_hbm, i1_hbm, o_hbm):
        def body(x_vmem, i0_vmem, i1_vmem):
            pltpu.sync_copy(x_vmem, o_hbm.at[i0_vmem.at[0]])
            pltpu.sync_copy(x_vmem, o_hbm.at[i1_vmem.at[0]])

        pltpu.emit_pipeline(
            body,
            grid=(n // SC_WINDOW,),
            in_specs=[pl.BlockSpec((SC_WINDOW, LANES), index_map=lambda i: (i, 0)),
                      pl.BlockSpec((1, SC_WINDOW), index_map=lambda i: (0, i)),
                      pl.BlockSpec((1, SC_WINDOW), index_map=lambda i: (0, i))],
            out_specs=[],
            core_axis_name="subcore",
            dimension_semantics=(pltpu.PARALLEL,),
        )(x_hbm, i0_hbm, i1_hbm)

    return scatter_rows(rows, idx0, idx1)


def _sc_gather(table, idx):
    n = idx.shape[1]
    mesh = plsc.VectorSubcoreMesh(core_axis_name="core", subcore_axis_name="subcore")

    @pl.kernel(out_type=jax.ShapeDtypeStruct((n, LANES), table.dtype), mesh=mesh)
    def gather_rows(x_hbm, i_hbm, o_hbm):
        def body(i_vmem, o_vmem):
            pltpu.sync_copy(x_hbm.at[i_vmem.at[0]], o_vmem)

        pltpu.emit_pipeline(
            body,
            grid=(n // SC_WINDOW,),
            in_specs=[pl.BlockSpec((1, SC_WINDOW), index_map=lambda i: (0, i))],
            out_specs=[pl.BlockSpec((SC_WINDOW, LANES), index_map=lambda i: (i, 0))],
            core_axis_name="subcore",
            dimension_semantics=(pltpu.PARALLEL,),
        )(i_hbm, o_hbm)

    return gather_rows(table, idx)


def _record_rows(idx):
    piece = jnp.arange(idx.shape[1] * ROW_PIECES, dtype=jnp.int32) % ROW_PIECES
    return jnp.repeat(idx, ROW_PIECES, axis=1) * ROW_PIECES + piece[None, :]


def _moe_plan(route_t, counts, bm):
    n_tok = route_t.shape[1]
    n_asg = n_tok * TOP_K
    e = route_t[:TOP_K].astype(jnp.int32)
    rank = route_t[4:4 + TOP_K].astype(jnp.int32)
    counts = counts[0, :N_EXPERTS].astype(jnp.int32)
    padded = (counts + bm - 1) // bm * bm
    pad_end = jnp.cumsum(padded)
    pad_start = pad_end - padded
    start_of = jnp.zeros_like(e)
    for j in range(N_EXPERTS):
        start_of = jnp.where(e == j, pad_start[j], start_of)
    dest = start_of + rank
    n_slots = n_asg + N_EXPERTS * bm
    n_blk = n_slots // bm
    blk_start = jnp.arange(n_blk, dtype=jnp.int32) * bm
    blk_e = jnp.minimum(jnp.sum((pad_end[None, :] <= blk_start[:, None]).astype(jnp.int32), axis=1),
                        N_EXPERTS - 1)
    n_used = (pad_end[-1:] // bm).astype(jnp.int32)
    filled_end = pad_start + counts
    end_of_blk = jnp.sum(jnp.where(blk_e[:, None] == jnp.arange(N_EXPERTS, dtype=jnp.int32)[None, :],
                                   filled_end[None, :], 0), axis=1)
    blk_valid = jnp.clip(end_of_blk - blk_start, 0, bm).astype(jnp.int32)
    return dest, blk_e, n_used, blk_valid, n_slots


def _head_groups(parts):
    rows = parts[0].shape[0]
    used = sum(p.shape[-1] for p in parts)
    pad = jnp.zeros((rows, H_B, LANES - used), parts[0].dtype)
    return jnp.concatenate(list(parts) + [pad], axis=-1).reshape(rows, H_B * LANES)


def _layer_weights(l, g_attn, w_in, g_subln, w_a, g_q_lat, w_uq, g_kv_lat, w_uk, w_uv, w_b, w_o, g_ffn,
                   w_group, b_group, w_router, b_router):
    wt = jnp.swapaxes(w_in, 1, 2)[l]
    o_kr = 3 * COLS_A + Q_LORA + KV_LORA
    w_pack = _transpose_cast(jnp.concatenate(
        [wt[:o_kr], jnp.pad(wt[o_kr:o_kr + QK_ROPE], ((0, KR_PAD - QK_ROPE), (0, 0))), wt[o_kr + QK_ROPE:]],
        axis=0))
    wq = w_uq[l].reshape(Q_LORA, H_B, QK_NOPE + QK_ROPE)
    w_uq_p = jnp.concatenate([wq[:, :, :QK_NOPE].reshape(Q_LORA, COLS_QN),
                              wq[:, :, QK_NOPE:].reshape(Q_LORA, COLS_QR)], axis=1).astype(BF16)
    w_uq_g = _head_groups([wq]).astype(BF16)
    w_kn = _head_groups([w_uk[l]]).astype(BF16)
    place = jnp.pad(jnp.eye(QK_ROPE, dtype=F32), ((0, KR_PAD - QK_ROPE), (0, 0)))
    w_krp = _head_groups([jnp.zeros((KR_PAD, H_B, QK_NOPE), F32),
                          jnp.broadcast_to(place[:, None, :], (KR_PAD, H_B, QK_ROPE))]).astype(BF16)
    w_uv_all = w_uv[l].reshape(KV_LORA, COLS_VB).astype(BF16)
    wuk_t = jnp.transpose(w_uk[l], (1, 2, 0)).astype(BF16)
    wuv = jnp.transpose(w_uv[l], (1, 0, 2))
    eye = jnp.eye(H_B, dtype=F32)
    wz = (wuv[:, :, None, :] * eye[:, None, :, None]).reshape(H_B, KV_LORA, COLS_VB).astype(BF16)
    w_route = jnp.concatenate(
        [w_group[l], jnp.transpose(w_router[l], (1, 0, 2)).reshape(D_MODEL, N_EXPERTS),
         jnp.zeros((D_MODEL, ROUTE_COLS - N_GROUPS - N_EXPERTS), F32)], axis=1)
    w_route_hi = w_route.astype(BF16)
    w_route = jnp.stack([w_route_hi, (w_route - w_route_hi.astype(F32)).astype(BF16)])
    b_route = jnp.concatenate([b_group[l], b_router[l].reshape(N_EXPERTS),
                               jnp.zeros((ROUTE_COLS - N_GROUPS - N_EXPERTS,), F32)])[None, :]
    return dict(g_attn=g_attn[l][None, :], w_pack=w_pack, g_q=g_q_lat[l][None, :], g_kv=g_kv_lat[l][None, :],
                w_uq_p=w_uq_p, w_uq_g=w_uq_g, w_kn=w_kn, w_krp=w_krp, w_uv_all=w_uv_all, wuk_t=wuk_t, wz=wz,
                g_subln=g_subln[l][None, :], g_subln_col=g_subln[l][:, None],
                wa=w_a[l].astype(BF16), wb=w_b[l].astype(BF16), wo=w_o[l].astype(BF16),
                g_ffn=g_ffn[l][None, :], w_route=w_route, b_route=b_route)


def _trunk(x, pos_tile, n_pos_tiles, tm, caches, lws, lams, wgu, wdn, g_final, *, batch, seq, bm):
    depth = len(lws)
    t = batch * seq
    x = x.reshape(t, D_MODEL)
    tabs = _rope_tables(pos_tile)
    prompt = caches is None
    ks, vs, cs, rs = [], [], [], []
    new_cache = None
    for l in range(depth):
        w = lws[l]
        out_scale = 1.0 - (0.8 - 0.6 * math.exp(-0.3 * l))
        outs = _inproj(x, w, tabs, tm=tm, n_pos_tiles=n_pos_tiles, prompt=prompt, layer=l, depth=depth,
                       batch=batch, prev=new_cache)
        k, v, ckv, kr = outs[:4]
        if prompt:
            new_cache = (k, v, ckv, kr)
            qat, kb, vat, qbt, kq, vbt, gates = outs[4:]
            oa = _attn_t(qat, kb, vat, batch=batch, seq=seq, n_heads=H_A, dv=DV_A, n_maps=2, lam=lams[l],
                         g=w['g_subln_col'], out_scale=out_scale, name="diff_attn_prompt")
            ob = _attn_t(qbt, kq, vbt, batch=batch, seq=seq, n_heads=H_B, dv=V_B, n_maps=1,
                         name="mla_attn_prompt")
        else:
            qa, kb, vb, ckvb, krb, qn, qr, gates = outs[4:]
            kc, vc, cc, rc, past = caches
            oa = _diff_sample(lams[l], qa, kc, vc, kb, vb, w['g_subln'], layer=l, batch=batch, seq=seq,
                              past=past, out_scale=out_scale)
            ob = _mla_sample(qn, qr, cc, rc, ckvb, krb, w['wuk_t'], w['wz'], layer=l, batch=batch, seq=seq,
                             past=past)
        x1, h_rows, route, counts, route_t = _merge(oa, ob, gates, x, w, tm=tm, tiled_t=prompt)
        dest, blk_e, n_used, blk_valid, n_slots = _moe_plan(route_t, counts, bm)
        dest_rows = _record_rows(dest)
        xs_rows = _sc_dispatch(h_rows, dest_rows[0:1], dest_rows[1:2], n_slots * ROW_PIECES)
        y_slots = _experts(blk_e, n_used, blk_valid, xs_rows, wgu, wdn, layer=l, bm=bm)
        y01 = _sc_gather(y_slots, _record_rows(dest.reshape(1, TOP_K * t)))
        x = _combine(x1, y01, route, g_final, tm=tm, final_norm=(l == depth - 1))
        ks.append(k)
        vs.append(v)
        cs.append(ckv)
        rs.append(kr)
    y = x.reshape(batch, seq, D_MODEL)
    if prompt:
        k, v, ckv, kr = new_cache
        return (y,
                jnp.transpose(k.reshape(depth, batch, H_A, 2, DH_A, seq), (0, 1, 5, 2, 3, 4)),
                v.reshape(depth, batch, seq, H_A, DV_A),
                ckv.reshape(depth, batch, seq, KV_LORA),
                jnp.transpose(kr.reshape(depth, batch, QK_ROPE, seq), (0, 1, 3, 2)))
    return (y,
            jnp.stack(ks).reshape(depth, batch, seq, H_A, 2, DH_A),
            jnp.stack(vs).reshape(depth, batch, seq, H_A, DV_A),
            jnp.stack(cs).reshape(depth, batch, seq, KV_LORA),
            jnp.stack(rs).reshape(depth, batch, seq, QK_ROPE))


def kernel(x_prompt, x_sample, cache_diff_k, cache_diff_v, cache_mla_ckv, cache_mla_krope, g_attn, w_in,
           lambda_q1, lambda_k1, lambda_q2, lambda_k2, g_subln, w_a, g_q_lat, w_uq, g_kv_lat, w_uk, w_uv, w_b,
           w_o, g_ffn, w_group, b_group, w_router, b_router, w_gate_up, w_down, g_final):
    depth = w_in.shape[0]
    batch, seq, _ = x_prompt.shape
    dec_batch, dec_seq, _ = x_sample.shape
    past = cache_diff_k.shape[2]
    assert dec_seq == CHUNK and past % CHUNK == 0, "sample frames must form exactly one new chunk"
    assert seq % TQ == 0

    lws = [_layer_weights(l, g_attn, w_in, g_subln, w_a, g_q_lat, w_uq, g_kv_lat, w_uk, w_uv, w_b, w_o, g_ffn,
                          w_group, b_group, w_router, b_router) for l in range(depth)]
    lams = []
    for l in range(depth):
        lam_init = 0.8 - 0.6 * math.exp(-0.3 * l)
        lam = (jnp.exp(jnp.sum(lambda_q1[l] * lambda_k1[l])) - jnp.exp(jnp.sum(lambda_q2[l] * lambda_k2[l]))
               + lam_init)
        lams.append(lam.reshape(1).astype(F32))
    wgu = w_gate_up.reshape(depth * N_EXPERTS, D_MODEL, 2 * D_EXPERT)
    wdn = w_down.reshape(depth * N_EXPERTS, D_EXPERT, D_MODEL)
    g_fin = g_final[None, :]

    tm_p = min(TOKEN_TILE, seq)
    outs_p = _trunk(x_prompt, jnp.arange(seq, dtype=jnp.int32), seq // tm_p, tm_p, None, lws, lams, wgu, wdn,
                    g_fin, batch=batch, seq=seq, bm=min(EXPERT_BLOCK_PROMPT, batch * seq))

    tm_s = min(TOKEN_TILE, dec_batch * dec_seq)
    pos_tile = jnp.tile(past + jnp.arange(dec_seq, dtype=jnp.int32), tm_s // dec_seq)
    caches = (jnp.transpose(cache_diff_k, (0, 1, 3, 4, 5, 2)).reshape(depth * dec_batch * COLS_A, past),
              cache_diff_v.reshape(depth * dec_batch, past * H_A, DV_A),
              cache_mla_ckv.reshape(depth * dec_batch, past, KV_LORA),
              jnp.transpose(cache_mla_krope, (0, 1, 3, 2)).reshape(depth * dec_batch * QK_ROPE, past), past)
    outs_s = _trunk(x_sample, pos_tile, 1, tm_s, caches, lws, lams, wgu, wdn, g_fin,
                    batch=dec_batch, seq=dec_seq, bm=min(EXPERT_BLOCK_SAMPLE, dec_batch * dec_seq))

    return (outs_p[0], outs_s[0]) + outs_p[1:] + outs_s[1:]
```
